```python
import math
import jax, jax.numpy as jnp
from jax import lax
import numpy as np

D_MODEL = 1024
BATCH = 1
SEQ = 16384
DEPTH = 1

HEAD_DIM = 64
ATT_GROUPS = ((128, 1), (512, 4), (2048, 16))
HEADS_PER_GROUP = 4
N_ATT_HEADS = HEADS_PER_GROUP * len(ATT_GROUPS)
ATT_WIDTH = N_ATT_HEADS * HEAD_DIM
ATT_MERGED = HEADS_PER_GROUP * HEAD_DIM
BLK = 128
POOL_WINDOWS = (2, 4, 8, 16)
POOL_GROUP_WIDTH = 3 * D_MODEL // 16
POOL_WIDTH = POOL_GROUP_WIDTH * len(POOL_WINDOWS)
D_FF = 4 * D_MODEL
N_IN = 3 * ATT_WIDTH + POOL_WIDTH + 2 * D_MODEL
NORM_EPS = 1e-6
ALIBI_MAX_BIAS = 8.0

kernel_name = "hybrid_dilated_attn_pool_gated_block"


def _rmsnorm(x, g):
    xf = x.astype(jnp.float32)
    y = xf * lax.rsqrt(jnp.mean(xf * xf, axis=-1, keepdims=True) + NORM_EPS)
    return (y * g.astype(jnp.float32)).astype(x.dtype)


def _dilated_window_attention(q, k, v, dilation, n_steps, slopes):
    B, S, H, Dh = q.shape
    L = S // dilation
    nb = -(-L // BLK)
    Lp = nb * BLK
    Z = B * dilation

    def to_sub(t):
        t = t.reshape(B, L, dilation, H, Dh).transpose(0, 2, 1, 3, 4).reshape(Z, L, H, Dh)
        return jnp.pad(t, ((0, 0), (0, Lp - L), (0, 0), (0, 0)))

    def band(t):
        prev = jnp.pad(t, ((0, 0), (BLK, 0), (0, 0), (0, 0)))[:, :Lp]
        return jnp.concatenate([prev.reshape(Z, nb, BLK, H, Dh),
                                t.reshape(Z, nb, BLK, H, Dh)], axis=2)

    qb = to_sub(q).reshape(Z, nb, BLK, H, Dh).astype(jnp.float32)
    kb = band(to_sub(k)).astype(jnp.float32)
    vb = band(to_sub(v)).astype(jnp.float32)

    s = jnp.einsum('znqhd,znkhd->znhqk', qb, kb) * (Dh ** -0.5)
    steps = BLK + jnp.arange(BLK)[:, None] - jnp.arange(2 * BLK)[None, :]
    key_idx = (jnp.arange(nb)[:, None, None] * BLK
               + jnp.arange(2 * BLK)[None, None, :] - BLK)
    valid = (steps >= 0) & (steps <= n_steps) & (key_idx >= 0)
    bias = -(slopes[:, None, None] * (steps * dilation).astype(jnp.float32))
    s = jnp.where(valid[None, :, None], s + bias[None, None], -jnp.inf)
    lse = jax.nn.logsumexp(s, axis=-1)
    p = jnp.exp(s - lse[..., None])
    o = jnp.einsum('znhqk,znkhd->znqhd', p, vb).reshape(Z, Lp, H, Dh)[:, :L]
    lse = lse.transpose(0, 1, 3, 2).reshape(Z, Lp, H)[:, :L]
    o = o.reshape(B, dilation, L, H, Dh).transpose(0, 2, 1, 3, 4).reshape(B, S, H, Dh)
    lse = lse.reshape(B, dilation, L, H).transpose(0, 2, 1, 3).reshape(B, S, H)
    return o, lse


def _attention_branch(q, k, v):
    B, S, _ = q.shape
    q = q.reshape(B, S, N_ATT_HEADS, HEAD_DIM)
    k = k.reshape(B, S, N_ATT_HEADS, HEAD_DIM)
    v = v.reshape(B, S, N_ATT_HEADS, HEAD_DIM)
    slopes = 2.0 ** (-ALIBI_MAX_BIAS * (jnp.arange(N_ATT_HEADS, dtype=jnp.float32) + 1.0)
                     / N_ATT_HEADS)
    outs, lses = [], []
    for g, (window, dilation) in enumerate(ATT_GROUPS):
        hs = slice(g * HEADS_PER_GROUP, (g + 1) * HEADS_PER_GROUP)
        o, l = _dilated_window_attention(q[:, :, hs], k[:, :, hs], v[:, :, hs],
                                         dilation, window // dilation, slopes[hs])
        outs.append(o)
        lses.append(l)
    outs = jnp.stack(outs, axis=0)
    wts = jax.nn.softmax(jnp.stack(lses, axis=0), axis=0)
    merged = jnp.sum(wts[..., None] * outs, axis=0)
    return merged.reshape(B, S, ATT_MERGED)


def _pool_branch(pz, w_grp, scale):
    B, S, _ = pz.shape
    pf = pz.astype(jnp.float32).reshape(B, S, len(POOL_WINDOWS), POOL_GROUP_WIDTH)
    c0 = jnp.pad(jnp.cumsum(pf, axis=1), ((0, 0), (1, 0), (0, 0), (0, 0)))
    t = jnp.arange(S)
    pooled = []
    for g, w in enumerate(POOL_WINDOWS):
        lower = jnp.take(c0[:, :, g], jnp.maximum(t + 1 - w, 0), axis=1)
        count = jnp.minimum(t + 1, w).astype(jnp.float32)[None, :, None]
        pooled.append((c0[:, 1:, g] - lower) / count)
    pooled = jnp.stack(pooled, axis=2) - pf
    mixed = jnp.einsum('bsgc,gcd->bsgd', pooled, w_grp.astype(jnp.float32))
    return mixed.reshape(B, S, POOL_WIDTH) * scale.astype(jnp.float32)


def setup_inputs(seed: int = 0) -> dict:
    key = jax.random.key(seed)
    ks = jax.random.split(key, 12)
    f32 = jnp.float32

    def nrm(k, shape, fan_in):
        return jax.random.normal(k, shape, f32) * (fan_in ** -0.5)

    def gain(k, shape):
        return 1.0 + 0.02 * jax.random.normal(k, shape, f32)

    return {
        "x": jax.random.normal(ks[0], (BATCH, SEQ, D_MODEL), f32),
        "norm_mix_g": gain(ks[1], (DEPTH, D_MODEL)),
        "w_in": nrm(ks[2], (DEPTH, D_MODEL, N_IN), D_MODEL),
        "w_att_out": nrm(ks[3], (DEPTH, ATT_MERGED, D_MODEL), ATT_MERGED),
        "w_pool_grp": nrm(ks[4], (DEPTH, len(POOL_WINDOWS), POOL_GROUP_WIDTH, POOL_GROUP_WIDTH),
                          POOL_GROUP_WIDTH),
        "pool_scale": 1.0 + 0.1 * jax.random.normal(ks[5], (DEPTH, POOL_WIDTH), f32),
        "w_pool_out": nrm(ks[6], (DEPTH, POOL_WIDTH, D_MODEL), POOL_WIDTH),
        "w_out": nrm(ks[7], (DEPTH, D_MODEL, D_MODEL), D_MODEL),
        "norm_mlp_g": gain(ks[8], (DEPTH, D_MODEL)),
        "w_mlp_in": nrm(ks[9], (DEPTH, D_MODEL, D_FF), D_MODEL),
        "w_mlp_out": nrm(ks[10], (DEPTH, D_FF, D_MODEL), D_FF),
        "norm_final_g": gain(ks[11], (D_MODEL,)),
    }


def reference(x, norm_mix_g, w_in, w_att_out, w_pool_grp, pool_scale, w_pool_out, w_out,
              norm_mlp_g, w_mlp_in, w_mlp_out, norm_final_g):
    dt = x.dtype
    offs = np.cumsum([ATT_WIDTH, ATT_WIDTH, ATT_WIDTH, POOL_WIDTH, D_MODEL]).tolist()
    h = x
    for l in range(DEPTH):
        u = _rmsnorm(h, norm_mix_g[l])
        z = jnp.einsum('bsd,dn->bsn', u, w_in[l])
        q, k, v, pz, ga, gp = jnp.split(z, offs, axis=-1)
        a = _attention_branch(q, k, v).astype(dt)
        p = _pool_branch(pz, w_pool_grp[l], pool_scale[l]).astype(dt)
        merged = (jax.nn.sigmoid(ga) * jnp.einsum('bsc,cd->bsd', a, w_att_out[l])
                  + jax.nn.sigmoid(gp) * jnp.einsum('bsc,cd->bsd', p, w_pool_out[l]))
        h = h + jnp.einsum('bsd,de->bse', merged, w_out[l])
        m = _rmsnorm(h, norm_mlp_g[l])
        hid = jnp.square(jax.nn.relu(jnp.einsum('bsd,df->bsf', m, w_mlp_in[l])))
        h = h + jnp.einsum('bsf,fd->bsd', hid, w_mlp_out[l])
    return _rmsnorm(h, norm_final_g)
```

```python
import functools
import math

import jax
import jax.numpy as jnp
from jax import lax
from jax.experimental import pallas as pl
from jax.experimental.pallas import tpu as pltpu

D_MODEL = 1024
HEAD_DIM = 64
ATT_GROUPS = ((128, 1), (512, 4), (2048, 16))
HEADS_PER_GROUP = 4
N_ATT_HEADS = HEADS_PER_GROUP * len(ATT_GROUPS)
ATT_WIDTH = N_ATT_HEADS * HEAD_DIM
ATT_MERGED = HEADS_PER_GROUP * HEAD_DIM
BLK = 128
POOL_WINDOWS = (2, 4, 8, 16)
POOL_GROUP_WIDTH = 3 * D_MODEL // 16
POOL_WIDTH = POOL_GROUP_WIDTH * len(POOL_WINDOWS)
D_FF = 4 * D_MODEL
N_IN = 3 * ATT_WIDTH + POOL_WIDTH + 2 * D_MODEL
NORM_EPS = 1e-6
ALIBI_MAX_BIAS = 8.0
POOL_HALO = max(POOL_WINDOWS)

VMEM_LIMIT_BYTES = 56 * 1024 * 1024
MASKED_SCORE = -1e30

ROW_TILE = 512
ATT_Q_ROWS = 1024

BF16 = jnp.bfloat16
F32 = jnp.float32


def _rms_scale(x):
    return lax.rsqrt(jnp.mean(x * x, axis=-1, keepdims=True) + NORM_EPS)


def _in_proj_kernel(x_ref, g_ref, w_ref, z_ref):
    x = x_ref[...]
    u = (x * _rms_scale(x) * g_ref[...]).astype(BF16)
    bounds = (0, ATT_WIDTH, 2 * ATT_WIDTH, 3 * ATT_WIDTH, 3 * ATT_WIDTH + POOL_WIDTH,
              3 * ATT_WIDTH + POOL_WIDTH + D_MODEL, N_IN)
    for c in range(len(bounds) - 1):
        lo, hi = bounds[c], bounds[c + 1]
        acc = jnp.dot(u, w_ref[:, lo:hi], preferred_element_type=F32)
        if c == 0:
            acc = acc * (HEAD_DIM ** -0.5)
        z_ref[:, lo:hi] = acc.astype(BF16)


def _in_proj(x2d, g, w_bf16):
    s = x2d.shape[0]
    return pl.pallas_call(
        _in_proj_kernel,
        out_shape=jax.ShapeDtypeStruct((s, N_IN), BF16),
        grid=(s // ROW_TILE,),
        in_specs=[
            pl.BlockSpec((ROW_TILE, D_MODEL), lambda i: (i, 0)),
            pl.BlockSpec((1, D_MODEL), lambda i: (0, 0)),
            pl.BlockSpec((D_MODEL, N_IN), lambda i: (0, 0)),
        ],
        out_specs=pl.BlockSpec((ROW_TILE, N_IN), lambda i: (i, 0)),
        compiler_params=pltpu.CompilerParams(
            dimension_semantics=("arbitrary",), vmem_limit_bytes=VMEM_LIMIT_BYTES),
        name="in_proj",
    )(x2d, g, w_bf16)


def _attn_kernel(q_ref, k_ref, v_ref, o_ref, lse_ref, kbuf, vbuf, bias_ref, *,
                 dilation, n_steps, slopes, n_blocks):
    first_class = pl.program_id(0) == 0
    step = pl.program_id(1)
    nh = HEADS_PER_GROUP

    @pl.when(first_class & (step == 0))
    def _():
        qi = lax.broadcasted_iota(jnp.int32, (BLK, 2 * BLK), 0)
        kj = lax.broadcasted_iota(jnp.int32, (BLK, 2 * BLK), 1)
        steps = BLK + qi - kj
        valid = (steps >= 0) & (steps <= n_steps)
        dist = (steps * dilation).astype(F32)
        for h in range(nh):
            bias_ref[h * BLK:(h + 1) * BLK, :] = jnp.where(valid, -slopes[h] * dist, MASKED_SCORE)

    @pl.when(step == 0)
    def _():
        kbuf[0:BLK, :] = jnp.zeros((BLK, ATT_MERGED), BF16)
        vbuf[0:BLK, :] = jnp.zeros((BLK, ATT_MERGED), BF16)

    kbuf[BLK:, :] = k_ref[...]
    vbuf[BLK:, :] = v_ref[...]

    lane_head = lax.broadcasted_iota(jnp.int32, (BLK, ATT_MERGED), 1) // HEAD_DIM
    head_masks = [lane_head == h for h in range(nh)]
    head_masks_bf16 = [m.astype(F32).astype(BF16) for m in head_masks]
    seq_start_mask = jnp.where(step == 0, MASKED_SCORE, 0.0).astype(F32)

    for b in range(n_blocks):
        qb = q_ref[b * BLK:(b + 1) * BLK, :]
        q_stack = jnp.concatenate([qb * head_masks_bf16[h] for h in range(nh)], axis=0)
        kk = kbuf[b * BLK:(b + 2) * BLK, :]
        vv = vbuf[b * BLK:(b + 2) * BLK, :]
        s = lax.dot_general(q_stack, kk, (((1,), (1,)), ((), ())),
                            preferred_element_type=F32)
        bias = bias_ref[...]
        if b == 0:
            col = lax.broadcasted_iota(jnp.int32, bias.shape, 1)
            bias = bias + jnp.where(col < BLK, seq_start_mask, 0.0)
        s = s + bias
        m = jnp.max(s, axis=-1, keepdims=True)
        p = jnp.exp(s - m)
        l = jnp.sum(p, axis=-1, keepdims=True)
        pv = jnp.dot(p.astype(BF16), vv, preferred_element_type=F32)
        o_stack = pv / l
        lse_stack = m + jnp.log(l)
        o = jnp.zeros((BLK, ATT_MERGED), F32)
        lse = jnp.zeros((BLK, ATT_MERGED), F32)
        for h in range(nh):
            rows = slice(h * BLK, (h + 1) * BLK)
            o = jnp.where(head_masks[h], o_stack[rows], o)
            lse = jnp.where(head_masks[h], lse_stack[rows], lse)
        o_ref[b * BLK:(b + 1) * BLK, :] = o
        lse_ref[b * BLK:(b + 1) * BLK, :] = lse

    kbuf[0:BLK, :] = kbuf[n_blocks * BLK:, :]
    vbuf[0:BLK, :] = vbuf[n_blocks * BLK:, :]


def _attention_group(z, group, slopes):
    window, dilation = ATT_GROUPS[group]
    s = z.shape[0]
    sub_len = s // dilation
    q_rows = min(ATT_Q_ROWS, sub_len)
    n_blocks = q_rows // BLK
    zv = z.reshape(sub_len, dilation * N_IN)
    blocks_per_row = N_IN // ATT_MERGED
    heads_per_width = ATT_WIDTH // ATT_MERGED

    def col_spec(which):
        off = which * heads_per_width + group
        return pl.BlockSpec((q_rows, ATT_MERGED), lambda r, i: (i, r * blocks_per_row + off))

    out_spec = pl.BlockSpec((q_rows, ATT_MERGED), lambda r, i: (i, r))
    out_sds = jax.ShapeDtypeStruct((sub_len, dilation * ATT_MERGED), F32)
    kern = functools.partial(_attn_kernel, dilation=dilation, n_steps=window // dilation,
                             slopes=slopes, n_blocks=n_blocks)
    o, lse = pl.pallas_call(
        kern,
        out_shape=(out_sds, out_sds),
        grid=(dilation, sub_len // q_rows),
        in_specs=[col_spec(0), col_spec(1), col_spec(2)],
        out_specs=(out_spec, out_spec),
        scratch_shapes=[
            pltpu.VMEM((q_rows + BLK, ATT_MERGED), BF16),
            pltpu.VMEM((q_rows + BLK, ATT_MERGED), BF16),
            pltpu.VMEM((HEADS_PER_GROUP * BLK, 2 * BLK), F32),
        ],
        compiler_params=pltpu.CompilerParams(
            dimension_semantics=("arbitrary", "arbitrary"), vmem_limit_bytes=VMEM_LIMIT_BYTES),
        name=f"attn_d{dilation}",
    )(zv, zv, zv)
    return o.reshape(s, ATT_MERGED), lse.reshape(s, ATT_MERGED)


def _mix_kernel(o0_ref, o1_ref, o2_ref, l0_ref, l1_ref, l2_ref, pz_ref, pzprev_ref,
                ga_ref, gp_ref, x_ref, wao_ref, wgrp_ref, scale_ref, wpo_ref, wout_ref,
                h_ref):
    i = pl.program_id(0)
    l0, l1, l2 = l0_ref[...], l1_ref[...], l2_ref[...]
    lmax = jnp.maximum(jnp.maximum(l0, l1), l2)
    e0, e1, e2 = jnp.exp(l0 - lmax), jnp.exp(l1 - lmax), jnp.exp(l2 - lmax)
    a = (e0 * o0_ref[...] + e1 * o1_ref[...] + e2 * o2_ref[...]) / (e0 + e1 + e2)
    att = jnp.dot(a.astype(BF16), wao_ref[...], preferred_element_type=F32)

    pf = pz_ref[...].astype(F32)
    prev = jnp.where(i > 0, pzprev_ref[...].astype(F32), 0.0)
    ext = jnp.concatenate([prev, pf], axis=0)
    t = i * ROW_TILE + lax.broadcasted_iota(jnp.int32, (ROW_TILE, 1), 0)
    col = lax.broadcasted_iota(jnp.int32, (ROW_TILE, POOL_WIDTH), 1)
    pooled = jnp.zeros((ROW_TILE, POOL_WIDTH), F32)
    acc = ext
    w = 1
    for g, window in enumerate(POOL_WINDOWS):
        while w < window:
            acc = acc + pltpu.roll(acc, w, axis=0)
            w *= 2
        count = jnp.minimum(t + 1, window).astype(F32)
        mean = acc[POOL_HALO:, :] / count
        in_group = (col >= g * POOL_GROUP_WIDTH) & (col < (g + 1) * POOL_GROUP_WIDTH)
        pooled = jnp.where(in_group, mean, pooled)
    pooled = pooled - pf
    mixed = jnp.dot(pooled.astype(BF16), wgrp_ref[...], preferred_element_type=F32)
    p = mixed * scale_ref[...]
    pool = jnp.dot(p.astype(BF16), wpo_ref[...], preferred_element_type=F32)

    merged = (jax.nn.sigmoid(ga_ref[...].astype(F32)) * att
              + jax.nn.sigmoid(gp_ref[...].astype(F32)) * pool)
    h_ref[...] = x_ref[...] + jnp.dot(merged.astype(BF16), wout_ref[...],
                                      preferred_element_type=F32)


def _mix(outs, lses, z, x2d, wao, wgrp_bd, scale, wpo, wout):
    s = x2d.shape[0]
    tm = ROW_TILE
    row = lambda i: (i, 0)
    const = lambda i: (0, 0)
    halo_blocks = tm // POOL_HALO
    pool_col = 3 * ATT_WIDTH // POOL_WIDTH
    gate_col = (3 * ATT_WIDTH + POOL_WIDTH) // D_MODEL
    att_spec = pl.BlockSpec((tm, ATT_MERGED), row)
    return pl.pallas_call(
        _mix_kernel,
        out_shape=jax.ShapeDtypeStruct((s, D_MODEL), F32),
        grid=(s // tm,),
        in_specs=[att_spec] * 6 + [
            pl.BlockSpec((tm, POOL_WIDTH), lambda i: (i, pool_col)),
            pl.BlockSpec((POOL_HALO, POOL_WIDTH),
                         lambda i: (jnp.maximum(i * halo_blocks - 1, 0), pool_col)),
            pl.BlockSpec((tm, D_MODEL), lambda i: (i, gate_col)),
            pl.BlockSpec((tm, D_MODEL), lambda i: (i, gate_col + 1)),
            pl.BlockSpec((tm, D_MODEL), row),
            pl.BlockSpec((ATT_MERGED, D_MODEL), const),
            pl.BlockSpec((POOL_WIDTH, POOL_WIDTH), const),
            pl.BlockSpec((1, POOL_WIDTH), const),
            pl.BlockSpec((POOL_WIDTH, D_MODEL), const),
            pl.BlockSpec((D_MODEL, D_MODEL), const),
        ],
        out_specs=pl.BlockSpec((tm, D_MODEL), row),
        compiler_params=pltpu.CompilerParams(
            dimension_semantics=("arbitrary",), vmem_limit_bytes=VMEM_LIMIT_BYTES),
        name="mix",
    )(*outs, *lses, z, z, z, z, x2d, wao, wgrp_bd, scale, wpo, wout)


MLP_CHUNK = 1024


def _mlp_kernel(h_ref, g_ref, w1_ref, w2_ref, gf_ref, o_ref):
    h = h_ref[...]
    m = (h * _rms_scale(h) * g_ref[...]).astype(BF16)
    y = h
    for c in range(D_FF // MLP_CHUNK):
        cols = slice(c * MLP_CHUNK, (c + 1) * MLP_CHUNK)
        hid = jnp.dot(m, w1_ref[:, cols], preferred_element_type=F32)
        hid = jnp.square(jnp.maximum(hid, 0.0)).astype(BF16)
        y = y + jnp.dot(hid, w2_ref[cols, :], preferred_element_type=F32)
    o_ref[...] = y * _rms_scale(y) * gf_ref[...]


def _mlp(h, g, w1, w2, gf):
    s = h.shape[0]
    tm = ROW_TILE
    row = lambda i: (i, 0)
    const = lambda i: (0, 0)
    return pl.pallas_call(
        _mlp_kernel,
        out_shape=jax.ShapeDtypeStruct((s, D_MODEL), F32),
        grid=(s // tm,),
        in_specs=[
            pl.BlockSpec((tm, D_MODEL), row),
            pl.BlockSpec((1, D_MODEL), const),
            pl.BlockSpec((D_MODEL, D_FF), const),
            pl.BlockSpec((D_FF, D_MODEL), const),
            pl.BlockSpec((1, D_MODEL), const),
        ],
        out_specs=pl.BlockSpec((tm, D_MODEL), row),
        compiler_params=pltpu.CompilerParams(
            dimension_semantics=("arbitrary",), vmem_limit_bytes=VMEM_LIMIT_BYTES),
        name="mlp",
    )(h, g, w1, w2, gf)


def _alibi_slopes():
    return tuple(2.0 ** (-ALIBI_MAX_BIAS * (h + 1.0) / N_ATT_HEADS) for h in range(N_ATT_HEADS))


def _block_diag(w_grp):
    n, c, _ = w_grp.shape
    out = jnp.zeros((n * c, n * c), w_grp.dtype)
    for g in range(n):
        out = lax.dynamic_update_slice(out, w_grp[g], (g * c, g * c))
    return out


def kernel(x, norm_mix_g, w_in, w_att_out, w_pool_grp, pool_scale, w_pool_out, w_out,
           norm_mlp_g, w_mlp_in, w_mlp_out, norm_final_g):
    batch, seq, d = x.shape
    assert d == D_MODEL and norm_mix_g.shape[0] == 1, "one layer of width D_MODEL"
    assert seq % (ATT_Q_ROWS * max(dil for _, dil in ATT_GROUPS)) == 0
    slopes = _alibi_slopes()
    outs = []
    for b in range(batch):
        h = x[b]
        z = _in_proj(h, norm_mix_g[0][None, :], w_in[0].astype(BF16))
        att = [_attention_group(z, g, slopes[g * HEADS_PER_GROUP:(g + 1) * HEADS_PER_GROUP])
               for g in range(len(ATT_GROUPS))]
        h = _mix([o for o, _ in att], [l for _, l in att], z, h,
                 w_att_out[0].astype(BF16), _block_diag(w_pool_grp[0]).astype(BF16),
                 pool_scale[0][None, :], w_pool_out[0].astype(BF16), w_out[0].astype(BF16))
        outs.append(_mlp(h, norm_mlp_g[0][None, :], w_mlp_in[0].astype(BF16),
                         w_mlp_out[0].astype(BF16), norm_final_g[None, :]))
    return outs[0][None] if batch == 1 else jnp.stack(outs, axis=0)
```

```python
import functools

import jax
import jax.numpy as jnp
from jax import lax
from jax.experimental import pallas as pl
from jax.experimental.pallas import tpu as pltpu

D_MODEL = 1024
HEAD_DIM = 64
ATT_GROUPS = ((128, 1), (512, 4), (2048, 16))
N_GROUPS = len(ATT_GROUPS)
HEADS_PER_GROUP = 4
N_ATT_HEADS = HEADS_PER_GROUP * N_GROUPS
ATT_WIDTH = N_ATT_HEADS * HEAD_DIM
ATT_MERGED = HEADS_PER_GROUP * HEAD_DIM
QKV_WIDTH = 3 * ATT_MERGED
BLK = 128
POOL_WINDOWS = (2, 4, 8, 16)
POOL_GROUP_WIDTH = 3 * D_MODEL // 16
POOL_WIDTH = POOL_GROUP_WIDTH * len(POOL_WINDOWS)
D_FF = 4 * D_MODEL
N_IN = 3 * ATT_WIDTH + POOL_WIDTH + 2 * D_MODEL
NORM_EPS = 1e-6
ALIBI_MAX_BIAS = 8.0
POOL_HALO = max(POOL_WINDOWS)

VMEM_LIMIT_BYTES = 56 * 1024 * 1024
LANES = 128
MASKED_SCORE = -1e30

ROW_TILE = 512
MAX_DILATION = max(d for _, d in ATT_GROUPS)
ATT_TILE = BLK * MAX_DILATION
MERGE_ROWS = 256

ZN_WIDTH = QKV_WIDTH + POOL_WIDTH
ZG_WIDTH = 2 * D_MODEL
NAT_WIDTH = ZN_WIDTH + ZG_WIDTH

BF16 = jnp.bfloat16
F32 = jnp.float32


def _rms_scale(x):
    return lax.rsqrt(jnp.mean(x * x, axis=-1, keepdims=True) + NORM_EPS)


def _in_proj_kernel(x_ref, g_ref, w_ref, zn_ref, zg_ref, *rest):
    dil_refs, slab_ref = rest[:-1], rest[-1]
    x = x_ref[...]
    u = (x * _rms_scale(x) * g_ref[...]).astype(BF16)
    lo = 0
    for out_ref, width in ((zn_ref, QKV_WIDTH), (zn_ref, POOL_WIDTH),
                           (zg_ref, D_MODEL), (zg_ref, D_MODEL)):
        off = lo - (0 if out_ref is zn_ref else ZN_WIDTH)
        acc = jnp.dot(u, w_ref[:, lo:lo + width], preferred_element_type=F32)
        out_ref[:, off:off + width] = acc.astype(BF16)
        lo += width
    n_slabs = QKV_WIDTH // LANES
    for gi, z_ref in enumerate(dil_refs):
        dilation = ATT_GROUPS[gi + 1][1]
        lo = NAT_WIDTH + gi * QKV_WIDTH
        acc = jnp.dot(u, w_ref[:, lo:lo + QKV_WIDTH], preferred_element_type=F32)
        for sl in range(n_slabs):
            slab_ref[sl] = acc[:, sl * LANES:(sl + 1) * LANES]
        for r in range(dilation):
            for sl in range(n_slabs):
                rows = slab_ref[sl, pl.ds(r, ROW_TILE // dilation, stride=dilation), :]
                z_ref[r, :, sl * LANES:(sl + 1) * LANES] = rows.astype(BF16)


def _in_proj(x2d, g, w_perm):
    s = x2d.shape[0]
    dils = [d for _, d in ATT_GROUPS[1:]]
    out_shape = [jax.ShapeDtypeStruct((s, ZN_WIDTH), BF16), jax.ShapeDtypeStruct((s, ZG_WIDTH), BF16)]
    out_specs = [pl.BlockSpec((ROW_TILE, ZN_WIDTH), lambda i: (i, 0)),
                 pl.BlockSpec((ROW_TILE, ZG_WIDTH), lambda i: (i, 0))]
    for d in dils:
        out_shape.append(jax.ShapeDtypeStruct((d, s // d, QKV_WIDTH), BF16))
        out_specs.append(pl.BlockSpec((d, ROW_TILE // d, QKV_WIDTH), lambda i: (0, i, 0)))
    return pl.pallas_call(
        _in_proj_kernel,
        out_shape=out_shape,
        grid=(s // ROW_TILE,),
        in_specs=[
            pl.BlockSpec((ROW_TILE, D_MODEL), lambda i: (i, 0)),
            pl.BlockSpec((1, D_MODEL), lambda i: (0, 0)),
            pl.BlockSpec((D_MODEL, N_IN), lambda i: (0, 0)),
        ],
        out_specs=out_specs,
        scratch_shapes=[pltpu.VMEM((QKV_WIDTH // LANES, ROW_TILE, LANES), F32)],
        compiler_params=pltpu.CompilerParams(
            dimension_semantics=("arbitrary",), vmem_limit_bytes=VMEM_LIMIT_BYTES),
        name="in_proj",
    )(x2d, g, w_perm)


def _attend_block(qb, kk, vv, bias, head_masks, head_masks_bf16):
    nh = HEADS_PER_GROUP
    q_stack = jnp.concatenate([qb * head_masks_bf16[h] for h in range(nh)], axis=0)
    s = lax.dot_general(q_stack, kk, (((1,), (1,)), ((), ())), preferred_element_type=F32)
    s = s + bias
    m = jnp.max(s, axis=-1, keepdims=True)
    p = jnp.exp(s - m)
    l = jnp.sum(p, axis=-1, keepdims=True)
    pv = jnp.dot(p.astype(BF16), vv, preferred_element_type=F32)
    o_stack = pv / l
    lse_stack = m + jnp.log(l)
    o = o_stack[0:BLK]
    lse = jnp.broadcast_to(lse_stack[0:BLK], (BLK, ATT_MERGED))
    for h in range(1, nh):
        rows = slice(h * BLK, (h + 1) * BLK)
        o = jnp.where(head_masks[h], o_stack[rows], o)
        lse = jnp.where(head_masks[h], lse_stack[rows], lse)
    return o, lse


def _attn_kernel(*refs, slopes):
    z_refs = refs[0:N_GROUPS]
    kprev_refs = refs[N_GROUPS:3 * N_GROUPS:2]
    vprev_refs = refs[N_GROUPS + 1:3 * N_GROUPS:2]
    a_ref, bias_ref, bias0_ref, o_scr, lse_scr = refs[3 * N_GROUPS:]
    step = pl.program_id(0)
    nh = HEADS_PER_GROUP

    @pl.when(step == 0)
    def _():
        qi = lax.broadcasted_iota(jnp.int32, (BLK, 2 * BLK), 0)
        kj = lax.broadcasted_iota(jnp.int32, (BLK, 2 * BLK), 1)
        steps = BLK + qi - kj
        for g, (window, dilation) in enumerate(ATT_GROUPS):
            valid = (steps >= 0) & (steps <= window // dilation)
            dist = (steps * dilation).astype(F32)
            for h in range(nh):
                bias = jnp.where(valid, -slopes[g * nh + h] * dist, MASKED_SCORE)
                bias_ref[g, h * BLK:(h + 1) * BLK, :] = bias
                bias0_ref[g, h * BLK:(h + 1) * BLK, :] = jnp.where(kj < BLK, MASKED_SCORE, bias)

    @pl.when(step == 1)
    def _():
        bias0_ref[...] = bias_ref[...]

    lane_head = lax.broadcasted_iota(jnp.int32, (BLK, ATT_MERGED), 1) // HEAD_DIM
    head_masks = [lane_head == h for h in range(nh)]
    head_masks_bf16 = [m.astype(F32).astype(BF16) for m in head_masks]
    attend = functools.partial(_attend_block, head_masks=head_masks,
                               head_masks_bf16=head_masks_bf16)
    q_cols, k_cols, v_cols = (slice(c * ATT_MERGED, (c + 1) * ATT_MERGED) for c in range(3))

    def aligned(row):
        return row if isinstance(row, int) else pl.multiple_of(row, BLK)

    def store(g, dilation, r, b, o, lse):
        rows = pl.ds(b * (BLK * dilation) + r, BLK, stride=dilation) if dilation > 1 else \
            pl.ds(aligned(b * BLK), BLK)
        for sl in range(ATT_MERGED // LANES):
            o_scr[g, sl, rows, :] = o[:, sl * LANES:(sl + 1) * LANES]
            lse_scr[g, sl, rows, :] = lse[:, sl * LANES:(sl + 1) * LANES]

    for g, (_, dilation) in enumerate(ATT_GROUPS):
        n_blocks = ATT_TILE // dilation // BLK
        z_ref, kprev_ref, vprev_ref = z_refs[g], kprev_refs[g], vprev_refs[g]

        def load(r, row0, n_rows, cols, z_ref=z_ref, dilation=dilation):
            rows = pl.ds(aligned(row0), n_rows)
            return z_ref[rows, cols] if dilation == 1 else z_ref[r, rows, cols]

        def class_body(r, _, g=g, dilation=dilation, n_blocks=n_blocks, load=load,
                       kprev_ref=kprev_ref, vprev_ref=vprev_ref):
            kprev = kprev_ref[...] if dilation == 1 else kprev_ref[r]
            vprev = vprev_ref[...] if dilation == 1 else vprev_ref[r]
            kk = jnp.concatenate([kprev, load(r, 0, BLK, k_cols)], axis=0)
            vv = jnp.concatenate([vprev, load(r, 0, BLK, v_cols)], axis=0)
            o, lse = attend(load(r, 0, BLK, q_cols), kk, vv, bias0_ref[g])
            store(g, dilation, r, 0, o, lse)

            def block_body(b, _):
                kk = load(r, (b - 1) * BLK, 2 * BLK, k_cols)
                vv = load(r, (b - 1) * BLK, 2 * BLK, v_cols)
                o, lse = attend(load(r, b * BLK, BLK, q_cols), kk, vv, bias_ref[g])
                store(g, dilation, r, b, o, lse)
                return 0

            if n_blocks > 1:
                lax.fori_loop(1, n_blocks, block_body, 0)
            return 0

        if dilation == 1:
            class_body(0, 0)
        else:
            lax.fori_loop(0, dilation, class_body, 0)

    def merge_body(c, _):
        rows = pl.ds(pl.multiple_of(c * MERGE_ROWS, MERGE_ROWS), MERGE_ROWS)
        for sl in range(ATT_MERGED // LANES):
            lses = [lse_scr[g, sl, rows, :] for g in range(N_GROUPS)]
            lmax = functools.reduce(jnp.maximum, lses)
            es = [jnp.exp(x - lmax) for x in lses]
            num = sum(e * o_scr[g, sl, rows, :] for g, e in enumerate(es))
            a_ref[rows, sl * LANES:(sl + 1) * LANES] = (num / sum(es)).astype(BF16)
        return 0

    lax.fori_loop(0, ATT_TILE // MERGE_ROWS, merge_body, 0)


def _attention(z_groups, slopes):
    s = z_groups[0].shape[0]
    cur_specs, prev_specs, prev_args = [], [], []
    for z, (_, d) in zip(z_groups, ATT_GROUPS):
        rows = ATT_TILE // d
        prev_row = lambda i, rows=rows: jnp.maximum(i * (rows // BLK) - 1, 0)
        for col in (1, 2):
            if d == 1:
                prev_specs.append(pl.BlockSpec((BLK, ATT_MERGED),
                                               lambda i, c=col, p=prev_row: (p(i), c)))
            else:
                prev_specs.append(pl.BlockSpec((d, BLK, ATT_MERGED),
                                               lambda i, c=col, p=prev_row: (0, p(i), c)))
            prev_args.append(z)
        if d == 1:
            cur_specs.append(pl.BlockSpec((rows, QKV_WIDTH), lambda i: (i, 0)))
        else:
            cur_specs.append(pl.BlockSpec((d, rows, QKV_WIDTH), lambda i: (0, i, 0)))
    scr_shape = (N_GROUPS, ATT_MERGED // LANES, ATT_TILE, LANES)
    bias_shape = (N_GROUPS, HEADS_PER_GROUP * BLK, 2 * BLK)
    return pl.pallas_call(
        functools.partial(_attn_kernel, slopes=slopes),
        out_shape=jax.ShapeDtypeStruct((s, ATT_MERGED), BF16),
        grid=(s // ATT_TILE,),
        in_specs=cur_specs + prev_specs,
        out_specs=pl.BlockSpec((ATT_TILE, ATT_MERGED), lambda i: (i, 0)),
        scratch_shapes=[pltpu.VMEM(bias_shape, F32), pltpu.VMEM(bias_shape, F32),
                        pltpu.VMEM(scr_shape, F32), pltpu.VMEM(scr_shape, F32)],
        compiler_params=pltpu.CompilerParams(
            dimension_semantics=("arbitrary",), vmem_limit_bytes=VMEM_LIMIT_BYTES),
        name="attn",
    )(*z_groups, *prev_args)


def _mix_kernel(a_ref, pz_ref, pzprev_ref, ga_ref, gp_ref, x_ref, wao_ref, wgrp_ref,
                scale_ref, wpo_ref, wout_ref, h_ref):
    i = pl.program_id(0)
    att = jnp.dot(a_ref[...], wao_ref[...], preferred_element_type=F32)

    pf = pz_ref[...].astype(F32)
    prev = jnp.where(i > 0, pzprev_ref[...].astype(F32), 0.0)
    ext = jnp.concatenate([prev, pf], axis=0)
    t = i * ROW_TILE + lax.broadcasted_iota(jnp.int32, (ROW_TILE, 1), 0)
    col = lax.broadcasted_iota(jnp.int32, (ROW_TILE, POOL_WIDTH), 1)
    pooled = jnp.zeros((ROW_TILE, POOL_WIDTH), F32)
    acc = ext
    w = 1
    for g, window in enumerate(POOL_WINDOWS):
        while w < window:
            acc = acc + pltpu.roll(acc, w, axis=0)
            w *= 2
        count = jnp.minimum(t + 1, window).astype(F32)
        mean = acc[POOL_HALO:, :] / count
        in_group = (col >= g * POOL_GROUP_WIDTH) & (col < (g + 1) * POOL_GROUP_WIDTH)
        pooled = jnp.where(in_group, mean, pooled)
    pooled = pooled - pf
    mixed = jnp.dot(pooled.astype(BF16), wgrp_ref[...], preferred_element_type=F32)
    p = mixed * scale_ref[...]
    pool = jnp.dot(p.astype(BF16), wpo_ref[...], preferred_element_type=F32)

    merged = (jax.nn.sigmoid(ga_ref[...].astype(F32)) * att
              + jax.nn.sigmoid(gp_ref[...].astype(F32)) * pool)
    h_ref[...] = x_ref[...] + jnp.dot(merged.astype(BF16), wout_ref[...],
                                      preferred_element_type=F32)


def _mix(a, zn, zg, x2d, wao, wgrp_bd, scale, wpo, wout):
    s = x2d.shape[0]
    tm = ROW_TILE
    row = lambda i: (i, 0)
    const = lambda i: (0, 0)
    halo_blocks = tm // POOL_HALO
    assert QKV_WIDTH % POOL_WIDTH == 0
    pool_col = QKV_WIDTH // POOL_WIDTH
    return pl.pallas_call(
        _mix_kernel,
        out_shape=jax.ShapeDtypeStruct((s, D_MODEL), F32),
        grid=(s // tm,),
        in_specs=[
            pl.BlockSpec((tm, ATT_MERGED), row),
            pl.BlockSpec((tm, POOL_WIDTH), lambda i: (i, pool_col)),
            pl.BlockSpec((POOL_HALO, POOL_WIDTH),
                         lambda i: (jnp.maximum(i * halo_blocks - 1, 0), pool_col)),
            pl.BlockSpec((tm, D_MODEL), lambda i: (i, 0)),
            pl.BlockSpec((tm, D_MODEL), lambda i: (i, 1)),
            pl.BlockSpec((tm, D_MODEL), row),
            pl.BlockSpec((ATT_MERGED, D_MODEL), const),
            pl.BlockSpec((POOL_WIDTH, POOL_WIDTH), const),
            pl.BlockSpec((1, POOL_WIDTH), const),
            pl.BlockSpec((POOL_WIDTH, D_MODEL), const),
            pl.BlockSpec((D_MODEL, D_MODEL), const),
        ],
        out_specs=pl.BlockSpec((tm, D_MODEL), row),
        compiler_params=pltpu.CompilerParams(
            dimension_semantics=("arbitrary",), vmem_limit_bytes=VMEM_LIMIT_BYTES),
        name="mix",
    )(a, zn, zn, zg, zg, x2d, wao, wgrp_bd, scale, wpo, wout)


MLP_CHUNK = 1024


def _mlp_kernel(h_ref, g_ref, w1_ref, w2_ref, gf_ref, o_ref):
    h = h_ref[...]
    m = (h * _rms_scale(h) * g_ref[...]).astype(BF16)
    y = h
    for c in range(D_FF // MLP_CHUNK):
        cols = slice(c * MLP_CHUNK, (c + 1) * MLP_CHUNK)
        hid = jnp.dot(m, w1_ref[:, cols], preferred_element_type=F32)
        hid = jnp.square(jnp.maximum(hid, 0.0)).astype(BF16)
        y = y + jnp.dot(hid, w2_ref[cols, :], preferred_element_type=F32)
    o_ref[...] = y * _rms_scale(y) * gf_ref[...]


def _mlp(h, g, w1, w2, gf):
    s = h.shape[0]
    tm = ROW_TILE
    row = lambda i: (i, 0)
    const = lambda i: (0, 0)
    return pl.pallas_call(
        _mlp_kernel,
        out_shape=jax.ShapeDtypeStruct((s, D_MODEL), F32),
        grid=(s // tm,),
        in_specs=[
            pl.BlockSpec((tm, D_MODEL), row),
            pl.BlockSpec((1, D_MODEL), const),
            pl.BlockSpec((D_MODEL, D_FF), const),
            pl.BlockSpec((D_FF, D_MODEL), const),
            pl.BlockSpec((1, D_MODEL), const),
        ],
        out_specs=pl.BlockSpec((tm, D_MODEL), row),
        compiler_params=pltpu.CompilerParams(
            dimension_semantics=("arbitrary",), vmem_limit_bytes=VMEM_LIMIT_BYTES),
        name="mlp",
    )(h, g, w1, w2, gf)


def _alibi_slopes():
    return tuple(2.0 ** (-ALIBI_MAX_BIAS * (h + 1.0) / N_ATT_HEADS) for h in range(N_ATT_HEADS))


def _permute_w_in(w):
    q, k, v, rest = (w[:, :ATT_WIDTH] * (HEAD_DIM ** -0.5), w[:, ATT_WIDTH:2 * ATT_WIDTH],
                     w[:, 2 * ATT_WIDTH:3 * ATT_WIDTH], w[:, 3 * ATT_WIDTH:])
    grp = lambda t, g: t[:, g * ATT_MERGED:(g + 1) * ATT_MERGED]
    qkv = [jnp.concatenate([grp(q, g), grp(k, g), grp(v, g)], axis=1) for g in range(N_GROUPS)]
    return jnp.concatenate([qkv[0], rest] + qkv[1:], axis=1).astype(BF16)


def _block_diag(w_grp):
    n, c, _ = w_grp.shape
    out = jnp.zeros((n * c, n * c), w_grp.dtype)
    for g in range(n):
        out = lax.dynamic_update_slice(out, w_grp[g], (g * c, g * c))
    return out


def kernel(x, norm_mix_g, w_in, w_att_out, w_pool_grp, pool_scale, w_pool_out, w_out,
           norm_mlp_g, w_mlp_in, w_mlp_out, norm_final_g):
    batch, seq, d = x.shape
    assert d == D_MODEL and norm_mix_g.shape[0] == 1, "one layer of width D_MODEL"
    assert seq % ATT_TILE == 0 and seq % ROW_TILE == 0
    slopes = _alibi_slopes()
    outs = []
    for b in range(batch):
        h = x[b]
        zn, zg, z1, z2 = _in_proj(h, norm_mix_g[0][None, :], _permute_w_in(w_in[0]))
        a = _attention((zn, z1, z2), slopes)
        h = _mix(a, zn, zg, h, w_att_out[0].astype(BF16), _block_diag(w_pool_grp[0]).astype(BF16),
                 pool_scale[0][None, :], w_pool_out[0].astype(BF16), w_out[0].astype(BF16))
        outs.append(_mlp(h, norm_mlp_g[0][None, :], w_mlp_in[0].astype(BF16),
                         w_mlp_out[0].astype(BF16), norm_final_g[None, :]))
    return outs[0][None] if batch == 1 else jnp.stack(outs, axis=0)
```

```python
import functools

import jax
import jax.numpy as jnp
from jax import lax
from jax.experimental import pallas as pl
from jax.experimental.pallas import tpu as pltpu

D_MODEL = 1024
HEAD_DIM = 64
ATT_GROUPS = ((128, 1), (512, 4), (2048, 16))
N_GROUPS = len(ATT_GROUPS)
HEADS_PER_GROUP = 4
N_ATT_HEADS = HEADS_PER_GROUP * N_GROUPS
ATT_WIDTH = N_ATT_HEADS * HEAD_DIM
ATT_MERGED = HEADS_PER_GROUP * HEAD_DIM
QKV_WIDTH = 3 * ATT_MERGED
BLK = 128
POOL_WINDOWS = (2, 4, 8, 16)
POOL_GROUP_WIDTH = 3 * D_MODEL // 16
POOL_WIDTH = POOL_GROUP_WIDTH * len(POOL_WINDOWS)
D_FF = 4 * D_MODEL
N_IN = 3 * ATT_WIDTH + POOL_WIDTH + 2 * D_MODEL
NORM_EPS = 1e-6
ALIBI_MAX_BIAS = 8.0
POOL_HALO = max(POOL_WINDOWS)

VMEM_LIMIT_BYTES = 56 * 1024 * 1024
LANES = 128
MASKED_SCORE = -1e30

ROW_TILE = 512
MAX_DILATION = max(d for _, d in ATT_GROUPS)
ATT_TILE = BLK * MAX_DILATION
MERGE_ROWS = 256
ATT_UNROLL = 4

ZN_WIDTH = QKV_WIDTH + POOL_WIDTH
ZG_WIDTH = 2 * D_MODEL
NAT_WIDTH = ZN_WIDTH + ZG_WIDTH

BF16 = jnp.bfloat16
F32 = jnp.float32


def _rms_scale(x):
    return lax.rsqrt(jnp.mean(x * x, axis=-1, keepdims=True) + NORM_EPS)


def _in_proj_kernel(x_ref, g_ref, w_ref, zn_ref, zg_ref, *rest):
    dil_refs, slab_ref = rest[:-1], rest[-1]
    x = x_ref[...]
    u = (x * _rms_scale(x) * g_ref[...]).astype(BF16)
    lo = 0
    for out_ref, width in ((zn_ref, QKV_WIDTH), (zn_ref, POOL_WIDTH),
                           (zg_ref, D_MODEL), (zg_ref, D_MODEL)):
        off = lo - (0 if out_ref is zn_ref else ZN_WIDTH)
        acc = jnp.dot(u, w_ref[:, lo:lo + width], preferred_element_type=F32)
        out_ref[:, off:off + width] = acc.astype(BF16)
        lo += width
    n_slabs = QKV_WIDTH // LANES
    for gi, z_ref in enumerate(dil_refs):
        dilation = ATT_GROUPS[gi + 1][1]
        lo = NAT_WIDTH + gi * QKV_WIDTH
        acc = jnp.dot(u, w_ref[:, lo:lo + QKV_WIDTH], preferred_element_type=F32)
        for sl in range(n_slabs):
            slab_ref[sl] = acc[:, sl * LANES:(sl + 1) * LANES]
        for r in range(dilation):
            for sl in range(n_slabs):
                rows = slab_ref[sl, pl.ds(r, ROW_TILE // dilation, stride=dilation), :]
                z_ref[r, :, sl * LANES:(sl + 1) * LANES] = rows.astype(BF16)


def _in_proj(x2d, g, w_perm):
    s = x2d.shape[0]
    dils = [d for _, d in ATT_GROUPS[1:]]
    out_shape = [jax.ShapeDtypeStruct((s, ZN_WIDTH), BF16), jax.ShapeDtypeStruct((s, ZG_WIDTH), BF16)]
    out_specs = [pl.BlockSpec((ROW_TILE, ZN_WIDTH), lambda i: (i, 0)),
                 pl.BlockSpec((ROW_TILE, ZG_WIDTH), lambda i: (i, 0))]
    for d in dils:
        out_shape.append(jax.ShapeDtypeStruct((d, s // d, QKV_WIDTH), BF16))
        out_specs.append(pl.BlockSpec((d, ROW_TILE // d, QKV_WIDTH), lambda i: (0, i, 0)))
    return pl.pallas_call(
        _in_proj_kernel,
        out_shape=out_shape,
        grid=(s // ROW_TILE,),
        in_specs=[
            pl.BlockSpec((ROW_TILE, D_MODEL), lambda i: (i, 0)),
            pl.BlockSpec((1, D_MODEL), lambda i: (0, 0)),
            pl.BlockSpec((D_MODEL, N_IN), lambda i: (0, 0)),
        ],
        out_specs=out_specs,
        scratch_shapes=[pltpu.VMEM((QKV_WIDTH // LANES, ROW_TILE, LANES), F32)],
        compiler_params=pltpu.CompilerParams(
            dimension_semantics=("arbitrary",), vmem_limit_bytes=VMEM_LIMIT_BYTES),
        name="in_proj",
    )(x2d, g, w_perm)


def _attend_block(qb, kk, vv, bias, head_masks, head_masks_bf16):
    nh = HEADS_PER_GROUP
    q_stack = jnp.concatenate([qb * head_masks_bf16[h] for h in range(nh)], axis=0)
    s = lax.dot_general(q_stack, kk, (((1,), (1,)), ((), ())), preferred_element_type=F32)
    s = s + bias
    m = jnp.max(s, axis=-1, keepdims=True)
    p = jnp.exp(s - m)
    l = jnp.sum(p, axis=-1, keepdims=True)
    pv = jnp.dot(p.astype(BF16), vv, preferred_element_type=F32)
    o_stack = pv / l
    lse_stack = m + jnp.log(l)
    o = o_stack[0:BLK]
    lse = jnp.broadcast_to(lse_stack[0:BLK], (BLK, ATT_MERGED))
    for h in range(1, nh):
        rows = slice(h * BLK, (h + 1) * BLK)
        o = jnp.where(head_masks[h], o_stack[rows], o)
        lse = jnp.where(head_masks[h], lse_stack[rows], lse)
    return o, lse


def _attn_kernel(*refs, slopes):
    z_refs = refs[0:N_GROUPS]
    kprev_refs = refs[N_GROUPS:3 * N_GROUPS:2]
    vprev_refs = refs[N_GROUPS + 1:3 * N_GROUPS:2]
    a_ref, bias_ref, bias0_ref, o_scr, lse_scr = refs[3 * N_GROUPS:]
    step = pl.program_id(0)
    nh = HEADS_PER_GROUP

    @pl.when(step == 0)
    def _():
        qi = lax.broadcasted_iota(jnp.int32, (BLK, 2 * BLK), 0)
        kj = lax.broadcasted_iota(jnp.int32, (BLK, 2 * BLK), 1)
        steps = BLK + qi - kj
        for g, (window, dilation) in enumerate(ATT_GROUPS):
            valid = (steps >= 0) & (steps <= window // dilation)
            dist = (steps * dilation).astype(F32)
            for h in range(nh):
                bias = jnp.where(valid, -slopes[g * nh + h] * dist, MASKED_SCORE)
                bias_ref[g, h * BLK:(h + 1) * BLK, :] = bias
                bias0_ref[g, h * BLK:(h + 1) * BLK, :] = jnp.where(kj < BLK, MASKED_SCORE, bias)

    @pl.when(step == 1)
    def _():
        bias0_ref[...] = bias_ref[...]

    lane_head = lax.broadcasted_iota(jnp.int32, (BLK, ATT_MERGED), 1) // HEAD_DIM
    head_masks = [lane_head == h for h in range(nh)]
    head_masks_bf16 = [m.astype(F32).astype(BF16) for m in head_masks]
    attend = functools.partial(_attend_block, head_masks=head_masks,
                               head_masks_bf16=head_masks_bf16)
    q_cols, k_cols, v_cols = (slice(c * ATT_MERGED, (c + 1) * ATT_MERGED) for c in range(3))

    def aligned(row):
        return row if isinstance(row, int) else pl.multiple_of(row, BLK)

    def store(g, dilation, r, b, o, lse):
        rows = pl.ds(b * (BLK * dilation) + r, BLK, stride=dilation) if dilation > 1 else \
            pl.ds(aligned(b * BLK), BLK)
        for sl in range(ATT_MERGED // LANES):
            o_scr[g, sl, rows, :] = o[:, sl * LANES:(sl + 1) * LANES]
            lse_scr[g, sl, rows, :] = lse[:, sl * LANES:(sl + 1) * LANES]

    for g, (_, dilation) in enumerate(ATT_GROUPS):
        n_blocks = ATT_TILE // dilation // BLK
        z_ref, kprev_ref, vprev_ref = z_refs[g], kprev_refs[g], vprev_refs[g]

        def load(r, row0, n_rows, cols, z_ref=z_ref, dilation=dilation):
            rows = pl.ds(aligned(row0), n_rows)
            return z_ref[rows, cols] if dilation == 1 else z_ref[r, rows, cols]

        def class_body(r, _, g=g, dilation=dilation, n_blocks=n_blocks, load=load,
                       kprev_ref=kprev_ref, vprev_ref=vprev_ref):
            kprev = kprev_ref[...] if dilation == 1 else kprev_ref[r]
            vprev = vprev_ref[...] if dilation == 1 else vprev_ref[r]
            kk = jnp.concatenate([kprev, load(r, 0, BLK, k_cols)], axis=0)
            vv = jnp.concatenate([vprev, load(r, 0, BLK, v_cols)], axis=0)
            o, lse = attend(load(r, 0, BLK, q_cols), kk, vv, bias0_ref[g])
            store(g, dilation, r, 0, o, lse)

            for b in range(1, n_blocks):
                kk = load(r, (b - 1) * BLK, 2 * BLK, k_cols)
                vv = load(r, (b - 1) * BLK, 2 * BLK, v_cols)
                o, lse = attend(load(r, b * BLK, BLK, q_cols), kk, vv, bias_ref[g])
                store(g, dilation, r, b, o, lse)
            return 0

        if dilation == 1:
            class_body(0, 0)
        else:
            lax.fori_loop(0, dilation, class_body, 0, unroll=max(1, ATT_UNROLL // n_blocks))

    def merge_body(c, _):
        rows = pl.ds(pl.multiple_of(c * MERGE_ROWS, MERGE_ROWS), MERGE_ROWS)
        for sl in range(ATT_MERGED // LANES):
            lses = [lse_scr[g, sl, rows, :] for g in range(N_GROUPS)]
            lmax = functools.reduce(jnp.maximum, lses)
            es = [jnp.exp(x - lmax) for x in lses]
            num = sum(e * o_scr[g, sl, rows, :] for g, e in enumerate(es))
            a_ref[rows, sl * LANES:(sl + 1) * LANES] = (num / sum(es)).astype(BF16)
        return 0

    lax.fori_loop(0, ATT_TILE // MERGE_ROWS, merge_body, 0)


def _attention(z_groups, slopes):
    s = z_groups[0].shape[0]
    cur_specs, prev_specs, prev_args = [], [], []
    for z, (_, d) in zip(z_groups, ATT_GROUPS):
        rows = ATT_TILE // d
        prev_row = lambda i, rows=rows: jnp.maximum(i * (rows // BLK) - 1, 0)
        for col in (1, 2):
            if d == 1:
                prev_specs.append(pl.BlockSpec((BLK, ATT_MERGED),
                                               lambda i, c=col, p=prev_row: (p(i), c)))
            else:
                prev_specs.append(pl.BlockSpec((d, BLK, ATT_MERGED),
                                               lambda i, c=col, p=prev_row: (0, p(i), c)))
            prev_args.append(z)
        if d == 1:
            cur_specs.append(pl.BlockSpec((rows, QKV_WIDTH), lambda i: (i, 0)))
        else:
            cur_specs.append(pl.BlockSpec((d, rows, QKV_WIDTH), lambda i: (0, i, 0)))
    scr_shape = (N_GROUPS, ATT_MERGED // LANES, ATT_TILE, LANES)
    bias_shape = (N_GROUPS, HEADS_PER_GROUP * BLK, 2 * BLK)
    return pl.pallas_call(
        functools.partial(_attn_kernel, slopes=slopes),
        out_shape=jax.ShapeDtypeStruct((s, ATT_MERGED), BF16),
        grid=(s // ATT_TILE,),
        in_specs=cur_specs + prev_specs,
        out_specs=pl.BlockSpec((ATT_TILE, ATT_MERGED), lambda i: (i, 0)),
        scratch_shapes=[pltpu.VMEM(bias_shape, F32), pltpu.VMEM(bias_shape, F32),
                        pltpu.VMEM(scr_shape, F32), pltpu.VMEM(scr_shape, F32)],
        compiler_params=pltpu.CompilerParams(
            dimension_semantics=("arbitrary",), vmem_limit_bytes=VMEM_LIMIT_BYTES),
        name="attn",
    )(*z_groups, *prev_args)


def _mix_kernel(a_ref, pz_ref, pzprev_ref, ga_ref, gp_ref, x_ref, wao_ref, wgrp_ref,
                scale_ref, wpo_ref, wout_ref, h_ref):
    i = pl.program_id(0)
    att = jnp.dot(a_ref[...], wao_ref[...], preferred_element_type=F32)

    pf = pz_ref[...].astype(F32)
    prev = jnp.where(i > 0, pzprev_ref[...].astype(F32), 0.0)
    ext = jnp.concatenate([prev, pf], axis=0)
    t = i * ROW_TILE + lax.broadcasted_iota(jnp.int32, (ROW_TILE, 1), 0)
    col = lax.broadcasted_iota(jnp.int32, (ROW_TILE, POOL_WIDTH), 1)
    pooled = jnp.zeros((ROW_TILE, POOL_WIDTH), F32)
    acc = ext
    w = 1
    for g, window in enumerate(POOL_WINDOWS):
        while w < window:
            acc = acc + pltpu.roll(acc, w, axis=0)
            w *= 2
        count = jnp.minimum(t + 1, window).astype(F32)
        mean = acc[POOL_HALO:, :] / count
        in_group = (col >= g * POOL_GROUP_WIDTH) & (col < (g + 1) * POOL_GROUP_WIDTH)
        pooled = jnp.where(in_group, mean, pooled)
    pooled = pooled - pf
    mixed = jnp.dot(pooled.astype(BF16), wgrp_ref[...], preferred_element_type=F32)
    p = mixed * scale_ref[...]
    pool = jnp.dot(p.astype(BF16), wpo_ref[...], preferred_element_type=F32)

    merged = (jax.nn.sigmoid(ga_ref[...].astype(F32)) * att
              + jax.nn.sigmoid(gp_ref[...].astype(F32)) * pool)
    h_ref[...] = x_ref[...] + jnp.dot(merged.astype(BF16), wout_ref[...],
                                      preferred_element_type=F32)


def _mix(a, zn, zg, x2d, wao, wgrp_bd, scale, wpo, wout):
    s = x2d.shape[0]
    tm = ROW_TILE
    row = lambda i: (i, 0)
    const = lambda i: (0, 0)
    halo_blocks = tm // POOL_HALO
    assert QKV_WIDTH % POOL_WIDTH == 0
    pool_col = QKV_WIDTH // POOL_WIDTH
    return pl.pallas_call(
        _mix_kernel,
        out_shape=jax.ShapeDtypeStruct((s, D_MODEL), F32),
        grid=(s // tm,),
        in_specs=[
            pl.BlockSpec((tm, ATT_MERGED), row),
            pl.BlockSpec((tm, POOL_WIDTH), lambda i: (i, pool_col)),
            pl.BlockSpec((POOL_HALO, POOL_WIDTH),
                         lambda i: (jnp.maximum(i * halo_blocks - 1, 0), pool_col)),
            pl.BlockSpec((tm, D_MODEL), lambda i: (i, 0)),
            pl.BlockSpec((tm, D_MODEL), lambda i: (i, 1)),
            pl.BlockSpec((tm, D_MODEL), row),
            pl.BlockSpec((ATT_MERGED, D_MODEL), const),
            pl.BlockSpec((POOL_WIDTH, POOL_WIDTH), const),
            pl.BlockSpec((1, POOL_WIDTH), const),
            pl.BlockSpec((POOL_WIDTH, D_MODEL), const),
            pl.BlockSpec((D_MODEL, D_MODEL), const),
        ],
        out_specs=pl.BlockSpec((tm, D_MODEL), row),
        compiler_params=pltpu.CompilerParams(
            dimension_semantics=("arbitrary",), vmem_limit_bytes=VMEM_LIMIT_BYTES),
        name="mix",
    )(a, zn, zn, zg, zg, x2d, wao, wgrp_bd, scale, wpo, wout)


MLP_CHUNK = 1024


def _mlp_kernel(h_ref, g_ref, w1_ref, w2_ref, gf_ref, o_ref):
    h = h_ref[...]
    m = (h * _rms_scale(h) * g_ref[...]).astype(BF16)
    y = h
    for c in range(D_FF // MLP_CHUNK):
        cols = slice(c * MLP_CHUNK, (c + 1) * MLP_CHUNK)
        hid = jnp.dot(m, w1_ref[:, cols], preferred_element_type=F32)
        hid = jnp.square(jnp.maximum(hid, 0.0)).astype(BF16)
        y = y + jnp.dot(hid, w2_ref[cols, :], preferred_element_type=F32)
    o_ref[...] = y * _rms_scale(y) * gf_ref[...]


def _mlp(h, g, w1, w2, gf):
    s = h.shape[0]
    tm = ROW_TILE
    row = lambda i: (i, 0)
    const = lambda i: (0, 0)
    return pl.pallas_call(
        _mlp_kernel,
        out_shape=jax.ShapeDtypeStruct((s, D_MODEL), F32),
        grid=(s // tm,),
        in_specs=[
            pl.BlockSpec((tm, D_MODEL), row),
            pl.BlockSpec((1, D_MODEL), const),
            pl.BlockSpec((D_MODEL, D_FF), const),
            pl.BlockSpec((D_FF, D_MODEL), const),
            pl.BlockSpec((1, D_MODEL), const),
        ],
        out_specs=pl.BlockSpec((tm, D_MODEL), row),
        compiler_params=pltpu.CompilerParams(
            dimension_semantics=("arbitrary",), vmem_limit_bytes=VMEM_LIMIT_BYTES),
        name="mlp",
    )(h, g, w1, w2, gf)


def _alibi_slopes():
    return tuple(2.0 ** (-ALIBI_MAX_BIAS * (h + 1.0) / N_ATT_HEADS) for h in range(N_ATT_HEADS))


def _permute_w_in(w):
    q, k, v, rest = (w[:, :ATT_WIDTH] * (HEAD_DIM ** -0.5), w[:, ATT_WIDTH:2 * ATT_WIDTH],
                     w[:, 2 * ATT_WIDTH:3 * ATT_WIDTH], w[:, 3 * ATT_WIDTH:])
    grp = lambda t, g: t[:, g * ATT_MERGED:(g + 1) * ATT_MERGED]
    qkv = [jnp.concatenate([grp(q, g), grp(k, g), grp(v, g)], axis=1) for g in range(N_GROUPS)]
    return jnp.concatenate([qkv[0], rest] + qkv[1:], axis=1).astype(BF16)


def _block_diag(w_grp):
    n, c, _ = w_grp.shape
    out = jnp.zeros((n * c, n * c), w_grp.dtype)
    for g in range(n):
        out = lax.dynamic_update_slice(out, w_grp[g], (g * c, g * c))
    return out


def kernel(x, norm_mix_g, w_in, w_att_out, w_pool_grp, pool_scale, w_pool_out, w_out,
           norm_mlp_g, w_mlp_in, w_mlp_out, norm_final_g):
    batch, seq, d = x.shape
    assert d == D_MODEL and norm_mix_g.shape[0] == 1, "one layer of width D_MODEL"
    assert seq % ATT_TILE == 0 and seq % ROW_TILE == 0
    slopes = _alibi_slopes()
    outs = []
    for b in range(batch):
        h = x[b]
        zn, zg, z1, z2 = _in_proj(h, norm_mix_g[0][None, :], _permute_w_in(w_in[0]))
        a = _attention((zn, z1, z2), slopes)
        h = _mix(a, zn, zg, h, w_att_out[0].astype(BF16), _block_diag(w_pool_grp[0]).astype(BF16),
                 pool_scale[0][None, :], w_pool_out[0].astype(BF16), w_out[0].astype(BF16))
        outs.append(_mlp(h, norm_mlp_g[0][None, :], w_mlp_in[0].astype(BF16),
                         w_mlp_out[0].astype(BF16), norm_final_g[None, :]))
    return outs[0][None] if batch == 1 else jnp.stack(outs, axis=0)
```

```python
import functools

import jax
import jax.numpy as jnp
from jax import lax
from jax.experimental import pallas as pl
from jax.experimental.pallas import tpu as pltpu

D_MODEL = 1024
HEAD_DIM = 64
ATT_GROUPS = ((128, 1), (512, 4), (2048, 16))
N_GROUPS = len(ATT_GROUPS)
HEADS_PER_GROUP = 4
N_ATT_HEADS = HEADS_PER_GROUP * N_GROUPS
ATT_WIDTH = N_ATT_HEADS * HEAD_DIM
ATT_MERGED = HEADS_PER_GROUP * HEAD_DIM
QKV_WIDTH = 3 * ATT_MERGED
BLK = 128
POOL_WINDOWS = (2, 4, 8, 16)
POOL_GROUP_WIDTH = 3 * D_MODEL // 16
POOL_WIDTH = POOL_GROUP_WIDTH * len(POOL_WINDOWS)
D_FF = 4 * D_MODEL
N_IN = 3 * ATT_WIDTH + POOL_WIDTH + 2 * D_MODEL
NORM_EPS = 1e-6
ALIBI_MAX_BIAS = 8.0
POOL_HALO = max(POOL_WINDOWS)

VMEM_LIMIT_BYTES = 56 * 1024 * 1024
LANES = 128
MASKED_SCORE = -1e30

ROW_TILE = 512
MAX_DILATION = max(d for _, d in ATT_GROUPS)
ATT_TILE = BLK * MAX_DILATION
MERGE_MOD = 4
MERGE_ROWS = 128
LOG2_E = 1.4426950408889634
ATT_UNROLL = 4

ZN_WIDTH = QKV_WIDTH + POOL_WIDTH
ZG_WIDTH = 2 * D_MODEL
NAT_WIDTH = ZN_WIDTH + ZG_WIDTH

BF16 = jnp.bfloat16
F32 = jnp.float32


def _rms_scale(x):
    return lax.rsqrt(jnp.mean(x * x, axis=-1, keepdims=True) + NORM_EPS)


def _in_proj_kernel(x_ref, g_ref, w_ref, zn_ref, zg_ref, *rest):
    dil_refs, slab_ref = rest[:-1], rest[-1]
    x = x_ref[...]
    u = (x * _rms_scale(x) * g_ref[...]).astype(BF16)
    lo = 0
    for out_ref, width in ((zn_ref, QKV_WIDTH), (zn_ref, POOL_WIDTH),
                           (zg_ref, D_MODEL), (zg_ref, D_MODEL)):
        off = lo - (0 if out_ref is zn_ref else ZN_WIDTH)
        acc = jnp.dot(u, w_ref[:, lo:lo + width], preferred_element_type=F32)
        out_ref[:, off:off + width] = acc.astype(BF16)
        lo += width
    n_slabs = QKV_WIDTH // LANES
    for gi, z_ref in enumerate(dil_refs):
        dilation = ATT_GROUPS[gi + 1][1]
        lo = NAT_WIDTH + gi * QKV_WIDTH
        acc = jnp.dot(u, w_ref[:, lo:lo + QKV_WIDTH], preferred_element_type=F32)
        for sl in range(n_slabs):
            slab_ref[sl] = acc[:, sl * LANES:(sl + 1) * LANES]
        for r in range(dilation):
            for sl in range(n_slabs):
                rows = slab_ref[sl, pl.ds(r, ROW_TILE // dilation, stride=dilation), :]
                z_ref[r, :, sl * LANES:(sl + 1) * LANES] = rows.astype(BF16)


def _in_proj(x2d, g, w_perm):
    s = x2d.shape[0]
    dils = [d for _, d in ATT_GROUPS[1:]]
    out_shape = [jax.ShapeDtypeStruct((s, ZN_WIDTH), BF16), jax.ShapeDtypeStruct((s, ZG_WIDTH), BF16)]
    out_specs = [pl.BlockSpec((ROW_TILE, ZN_WIDTH), lambda i: (i, 0)),
                 pl.BlockSpec((ROW_TILE, ZG_WIDTH), lambda i: (i, 0))]
    for d in dils:
        out_shape.append(jax.ShapeDtypeStruct((d, s // d, QKV_WIDTH), BF16))
        out_specs.append(pl.BlockSpec((d, ROW_TILE // d, QKV_WIDTH), lambda i: (0, i, 0)))
    return pl.pallas_call(
        _in_proj_kernel,
        out_shape=out_shape,
        grid=(s // ROW_TILE,),
        in_specs=[
            pl.BlockSpec((ROW_TILE, D_MODEL), lambda i: (i, 0)),
            pl.BlockSpec((1, D_MODEL), lambda i: (0, 0)),
            pl.BlockSpec((D_MODEL, N_IN), lambda i: (0, 0)),
        ],
        out_specs=out_specs,
        scratch_shapes=[pltpu.VMEM((QKV_WIDTH // LANES, ROW_TILE, LANES), F32)],
        compiler_params=pltpu.CompilerParams(
            dimension_semantics=("arbitrary",), vmem_limit_bytes=VMEM_LIMIT_BYTES),
        name="in_proj",
    )(x2d, g, w_perm)


def _attend_block(qb, kk, vv, bias, head_masks_bf16, low_half):
    nh = HEADS_PER_GROUP
    q_stack = jnp.concatenate([qb * head_masks_bf16[h] for h in range(nh)], axis=0)
    s = lax.dot_general(q_stack, kk, (((1,), (1,)), ((), ())), preferred_element_type=F32)
    s = s + bias
    m = jnp.max(s, axis=-1, keepdims=True)
    p = jnp.exp2(s - m)
    l = jnp.sum(p, axis=-1, keepdims=True)
    pv = jnp.dot(p.astype(BF16), vv, preferred_element_type=F32)
    slabs = []
    for sl in range(ATT_MERGED // LANES):
        lanes = slice(sl * LANES, (sl + 1) * LANES)
        rows_a = slice(2 * sl * BLK, (2 * sl + 1) * BLK)
        rows_b = slice((2 * sl + 1) * BLK, (2 * sl + 2) * BLK)
        slabs.append((jnp.where(low_half, pv[rows_a, lanes], pv[rows_b, lanes]),
                      jnp.where(low_half, m[rows_a], m[rows_b]),
                      jnp.where(low_half, l[rows_a], l[rows_b])))
    return slabs


def _attn_kernel(*refs, slopes):
    z_refs = refs[0:N_GROUPS]
    kprev_refs = refs[N_GROUPS:3 * N_GROUPS:2]
    vprev_refs = refs[N_GROUPS + 1:3 * N_GROUPS:2]
    a_ref, bias_ref, bias0_ref, o_scr, m_scr, l_scr, nat_scr = refs[3 * N_GROUPS:]
    step = pl.program_id(0)
    nh = HEADS_PER_GROUP
    n_slabs = ATT_MERGED // LANES

    @pl.when(step == 0)
    def _():
        qi = lax.broadcasted_iota(jnp.int32, (BLK, 2 * BLK), 0)
        kj = lax.broadcasted_iota(jnp.int32, (BLK, 2 * BLK), 1)
        steps = BLK + qi - kj
        for g, (window, dilation) in enumerate(ATT_GROUPS):
            valid = (steps >= 0) & (steps <= window // dilation)
            dist = (steps * dilation).astype(F32)
            for h in range(nh):
                bias = jnp.where(valid, -(slopes[g * nh + h] * LOG2_E) * dist, MASKED_SCORE)
                bias_ref[g, h * BLK:(h + 1) * BLK, :] = bias
                bias0_ref[g, h * BLK:(h + 1) * BLK, :] = jnp.where(kj < BLK, MASKED_SCORE, bias)

    @pl.when(step == 1)
    def _():
        bias0_ref[...] = bias_ref[...]

    lane_head = lax.broadcasted_iota(jnp.int32, (BLK, ATT_MERGED), 1) // HEAD_DIM
    head_masks_bf16 = [(lane_head == h).astype(F32).astype(BF16) for h in range(nh)]
    low_half = lax.broadcasted_iota(jnp.int32, (BLK, LANES), 1) < HEAD_DIM
    attend = functools.partial(_attend_block, head_masks_bf16=head_masks_bf16, low_half=low_half)
    q_cols, k_cols, v_cols = (slice(c * ATT_MERGED, (c + 1) * ATT_MERGED) for c in range(3))

    def aligned(row):
        return row if isinstance(row, int) else pl.multiple_of(row, BLK)

    sub_rows = ATT_TILE // MERGE_MOD

    def store(g, dilation, r, b, slabs):
        if dilation in (1, MERGE_MOD):
            rows = pl.ds(aligned(r * (ATT_TILE // dilation) + b * BLK), BLK)
        else:
            per = dilation // MERGE_MOD
            rows = pl.ds((r % MERGE_MOD) * sub_rows + b * BLK * per + r // MERGE_MOD, BLK,
                         stride=per)
        for sl, (o, m, l) in enumerate(slabs):
            o_scr[g, sl, rows, :] = o
            m_scr[g, sl, rows, :] = m
            l_scr[g, sl, rows, :] = l

    for g, (_, dilation) in enumerate(ATT_GROUPS):
        n_blocks = ATT_TILE // dilation // BLK
        z_ref, kprev_ref, vprev_ref = z_refs[g], kprev_refs[g], vprev_refs[g]

        def load(r, row0, n_rows, cols, z_ref=z_ref, dilation=dilation):
            rows = pl.ds(aligned(row0), n_rows)
            return z_ref[rows, cols] if dilation == 1 else z_ref[r, rows, cols]

        def class_body(r, _, g=g, dilation=dilation, n_blocks=n_blocks, load=load,
                       kprev_ref=kprev_ref, vprev_ref=vprev_ref):
            kprev = kprev_ref[...] if dilation == 1 else kprev_ref[r]
            vprev = vprev_ref[...] if dilation == 1 else vprev_ref[r]
            kk = jnp.concatenate([kprev, load(r, 0, BLK, k_cols)], axis=0)
            vv = jnp.concatenate([vprev, load(r, 0, BLK, v_cols)], axis=0)
            store(g, dilation, r, 0, attend(load(r, 0, BLK, q_cols), kk, vv, bias0_ref[g]))
            for b in range(1, n_blocks):
                kk = load(r, (b - 1) * BLK, 2 * BLK, k_cols)
                vv = load(r, (b - 1) * BLK, 2 * BLK, v_cols)
                store(g, dilation, r, b, attend(load(r, b * BLK, BLK, q_cols), kk, vv, bias_ref[g]))
            return 0

        if dilation == 1:
            class_body(0, 0)
        else:
            lax.fori_loop(0, dilation, class_body, 0, unroll=max(1, ATT_UNROLL // n_blocks))

    chunks = sub_rows // MERGE_ROWS

    def merge_body(idx, _):
        c = idx // chunks
        row0 = (idx % chunks) * MERGE_ROWS
        nat_rows = pl.ds(row0 * MERGE_MOD + c, MERGE_ROWS, stride=MERGE_MOD)
        cls_rows = pl.ds(pl.multiple_of(c * sub_rows + row0, MERGE_ROWS), MERGE_ROWS)
        rows_of = lambda g: nat_rows if ATT_GROUPS[g][1] == 1 else cls_rows
        for sl in range(n_slabs):
            ms = [m_scr[g, sl, rows_of(g), :] for g in range(N_GROUPS)]
            mmax = functools.reduce(jnp.maximum, ms)
            ws = [jnp.exp2(m - mmax) for m in ms]
            num = sum(w * o_scr[g, sl, rows_of(g), :] for g, w in enumerate(ws))
            den = sum(w * l_scr[g, sl, rows_of(g), :] for g, w in enumerate(ws))
            nat_scr[sl, nat_rows, :] = num / den
        return 0

    lax.fori_loop(0, MERGE_MOD * chunks, merge_body, 0)
    for sl in range(n_slabs):
        a_ref[:, sl * LANES:(sl + 1) * LANES] = nat_scr[sl].astype(BF16)


def _attention(z_groups, slopes):
    s = z_groups[0].shape[0]
    cur_specs, prev_specs, prev_args = [], [], []
    for z, (_, d) in zip(z_groups, ATT_GROUPS):
        rows = ATT_TILE // d
        prev_row = lambda i, rows=rows: jnp.maximum(i * (rows // BLK) - 1, 0)
        for col in (1, 2):
            if d == 1:
                prev_specs.append(pl.BlockSpec((BLK, ATT_MERGED),
                                               lambda i, c=col, p=prev_row: (p(i), c)))
            else:
                prev_specs.append(pl.BlockSpec((d, BLK, ATT_MERGED),
                                               lambda i, c=col, p=prev_row: (0, p(i), c)))
            prev_args.append(z)
        if d == 1:
            cur_specs.append(pl.BlockSpec((rows, QKV_WIDTH), lambda i: (i, 0)))
        else:
            cur_specs.append(pl.BlockSpec((d, rows, QKV_WIDTH), lambda i: (0, i, 0)))
    scr_shape = (N_GROUPS, ATT_MERGED // LANES, ATT_TILE, LANES)
    bias_shape = (N_GROUPS, HEADS_PER_GROUP * BLK, 2 * BLK)
    return pl.pallas_call(
        functools.partial(_attn_kernel, slopes=slopes),
        out_shape=jax.ShapeDtypeStruct((s, ATT_MERGED), BF16),
        grid=(s // ATT_TILE,),
        in_specs=cur_specs + prev_specs,
        out_specs=pl.BlockSpec((ATT_TILE, ATT_MERGED), lambda i: (i, 0)),
        scratch_shapes=[pltpu.VMEM(bias_shape, F32), pltpu.VMEM(bias_shape, F32),
                        pltpu.VMEM(scr_shape, F32), pltpu.VMEM(scr_shape, F32),
                        pltpu.VMEM(scr_shape, F32), pltpu.VMEM(scr_shape[1:], F32)],
        compiler_params=pltpu.CompilerParams(
            dimension_semantics=("arbitrary",), vmem_limit_bytes=VMEM_LIMIT_BYTES),
        name="attn",
    )(*z_groups, *prev_args)


def _mix_kernel(a_ref, pz_ref, pzprev_ref, ga_ref, gp_ref, x_ref, wao_ref, wgrp_ref,
                scale_ref, wpo_ref, wout_ref, h_ref):
    i = pl.program_id(0)
    att = jnp.dot(a_ref[...], wao_ref[...], preferred_element_type=F32)

    pf = pz_ref[...].astype(F32)
    prev = jnp.where(i > 0, pzprev_ref[...].astype(F32), 0.0)
    ext = jnp.concatenate([prev, pf], axis=0)
    t = i * ROW_TILE + lax.broadcasted_iota(jnp.int32, (ROW_TILE, 1), 0)
    col = lax.broadcasted_iota(jnp.int32, (ROW_TILE, POOL_WIDTH), 1)
    pooled = jnp.zeros((ROW_TILE, POOL_WIDTH), F32)
    acc = ext
    w = 1
    for g, window in enumerate(POOL_WINDOWS):
        while w < window:
            acc = acc + pltpu.roll(acc, w, axis=0)
            w *= 2
        count = jnp.minimum(t + 1, window).astype(F32)
        mean = acc[POOL_HALO:, :] / count
        in_group = (col >= g * POOL_GROUP_WIDTH) & (col < (g + 1) * POOL_GROUP_WIDTH)
        pooled = jnp.where(in_group, mean, pooled)
    pooled = pooled - pf
    mixed = jnp.dot(pooled.astype(BF16), wgrp_ref[...], preferred_element_type=F32)
    p = mixed * scale_ref[...]
    pool = jnp.dot(p.astype(BF16), wpo_ref[...], preferred_element_type=F32)

    merged = (jax.nn.sigmoid(ga_ref[...].astype(F32)) * att
              + jax.nn.sigmoid(gp_ref[...].astype(F32)) * pool)
    h_ref[...] = x_ref[...] + jnp.dot(merged.astype(BF16), wout_ref[...],
                                      preferred_element_type=F32)


def _mix(a, zn, zg, x2d, wao, wgrp_bd, scale, wpo, wout):
    s = x2d.shape[0]
    tm = ROW_TILE
    row = lambda i: (i, 0)
    const = lambda i: (0, 0)
    halo_blocks = tm // POOL_HALO
    assert QKV_WIDTH % POOL_WIDTH == 0
    pool_col = QKV_WIDTH // POOL_WIDTH
    return pl.pallas_call(
        _mix_kernel,
        out_shape=jax.ShapeDtypeStruct((s, D_MODEL), F32),
        grid=(s // tm,),
        in_specs=[
            pl.BlockSpec((tm, ATT_MERGED), row),
            pl.BlockSpec((tm, POOL_WIDTH), lambda i: (i, pool_col)),
            pl.BlockSpec((POOL_HALO, POOL_WIDTH),
                         lambda i: (jnp.maximum(i * halo_blocks - 1, 0), pool_col)),
            pl.BlockSpec((tm, D_MODEL), lambda i: (i, 0)),
            pl.BlockSpec((tm, D_MODEL), lambda i: (i, 1)),
            pl.BlockSpec((tm, D_MODEL), row),
            pl.BlockSpec((ATT_MERGED, D_MODEL), const),
            pl.BlockSpec((POOL_WIDTH, POOL_WIDTH), const),
            pl.BlockSpec((1, POOL_WIDTH), const),
            pl.BlockSpec((POOL_WIDTH, D_MODEL), const),
            pl.BlockSpec((D_MODEL, D_MODEL), const),
        ],
        out_specs=pl.BlockSpec((tm, D_MODEL), row),
        compiler_params=pltpu.CompilerParams(
            dimension_semantics=("arbitrary",), vmem_limit_bytes=VMEM_LIMIT_BYTES),
        name="mix",
    )(a, zn, zn, zg, zg, x2d, wao, wgrp_bd, scale, wpo, wout)


MLP_CHUNK = 1024


def _mlp_kernel(h_ref, g_ref, w1_ref, w2_ref, gf_ref, o_ref):
    h = h_ref[...]
    m = (h * _rms_scale(h) * g_ref[...]).astype(BF16)
    y = h
    for c in range(D_FF // MLP_CHUNK):
        cols = slice(c * MLP_CHUNK, (c + 1) * MLP_CHUNK)
        hid = jnp.dot(m, w1_ref[:, cols], preferred_element_type=F32)
        hid = jnp.square(jnp.maximum(hid, 0.0)).astype(BF16)
        y = y + jnp.dot(hid, w2_ref[cols, :], preferred_element_type=F32)
    o_ref[...] = y * _rms_scale(y) * gf_ref[...]


def _mlp(h, g, w1, w2, gf):
    s = h.shape[0]
    tm = ROW_TILE
    row = lambda i: (i, 0)
    const = lambda i: (0, 0)
    return pl.pallas_call(
        _mlp_kernel,
        out_shape=jax.ShapeDtypeStruct((s, D_MODEL), F32),
        grid=(s // tm,),
        in_specs=[
            pl.BlockSpec((tm, D_MODEL), row),
            pl.BlockSpec((1, D_MODEL), const),
            pl.BlockSpec((D_MODEL, D_FF), const),
            pl.BlockSpec((D_FF, D_MODEL), const),
            pl.BlockSpec((1, D_MODEL), const),
        ],
        out_specs=pl.BlockSpec((tm, D_MODEL), row),
        compiler_params=pltpu.CompilerParams(
            dimension_semantics=("arbitrary",), vmem_limit_bytes=VMEM_LIMIT_BYTES),
        name="mlp",
    )(h, g, w1, w2, gf)


def _alibi_slopes():
    return tuple(2.0 ** (-ALIBI_MAX_BIAS * (h + 1.0) / N_ATT_HEADS) for h in range(N_ATT_HEADS))


def _permute_w_in(w):
    q, k, v, rest = (w[:, :ATT_WIDTH] * (HEAD_DIM ** -0.5 * LOG2_E), w[:, ATT_WIDTH:2 * ATT_WIDTH],
                     w[:, 2 * ATT_WIDTH:3 * ATT_WIDTH], w[:, 3 * ATT_WIDTH:])
    grp = lambda t, g: t[:, g * ATT_MERGED:(g + 1) * ATT_MERGED]
    qkv = [jnp.concatenate([grp(q, g), grp(k, g), grp(v, g)], axis=1) for g in range(N_GROUPS)]
    return jnp.concatenate([qkv[0], rest] + qkv[1:], axis=1).astype(BF16)


def _block_diag(w_grp):
    n, c, _ = w_grp.shape
    out = jnp.zeros((n * c, n * c), w_grp.dtype)
    for g in range(n):
        out = lax.dynamic_update_slice(out, w_grp[g], (g * c, g * c))
    return out


def kernel(x, norm_mix_g, w_in, w_att_out, w_pool_grp, pool_scale, w_pool_out, w_out,
           norm_mlp_g, w_mlp_in, w_mlp_out, norm_final_g):
    batch, seq, d = x.shape
    assert d == D_MODEL and norm_mix_g.shape[0] == 1, "one layer of width D_MODEL"
    assert seq % ATT_TILE == 0 and seq % ROW_TILE == 0
    slopes = _alibi_slopes()
    outs = []
    for b in range(batch):
        h = x[b]
        zn, zg, z1, z2 = _in_proj(h, norm_mix_g[0][None, :], _permute_w_in(w_in[0]))
        a = _attention((zn, z1, z2), slopes)
        h = _mix(a, zn, zg, h, w_att_out[0].astype(BF16), _block_diag(w_pool_grp[0]).astype(BF16),
                 pool_scale[0][None, :], w_pool_out[0].astype(BF16), w_out[0].astype(BF16))
        outs.append(_mlp(h, norm_mlp_g[0][None, :], w_mlp_in[0].astype(BF16),
                         w_mlp_out[0].astype(BF16), norm_final_g[None, :]))
    return outs[0][None] if batch == 1 else jnp.stack(outs, axis=0)
```

```python
import functools

import jax
import jax.numpy as jnp
from jax import lax
from jax.experimental import pallas as pl
from jax.experimental.pallas import tpu as pltpu

D_MODEL = 1024
HEAD_DIM = 64
ATT_GROUPS = ((128, 1), (512, 4), (2048, 16))
N_GROUPS = len(ATT_GROUPS)
HEADS_PER_GROUP = 4
N_ATT_HEADS = HEADS_PER_GROUP * N_GROUPS
ATT_WIDTH = N_ATT_HEADS * HEAD_DIM
ATT_MERGED = HEADS_PER_GROUP * HEAD_DIM
QKV_WIDTH = 3 * ATT_MERGED
BLK = 128
POOL_WINDOWS = (2, 4, 8, 16)
POOL_GROUP_WIDTH = 3 * D_MODEL // 16
POOL_WIDTH = POOL_GROUP_WIDTH * len(POOL_WINDOWS)
D_FF = 4 * D_MODEL
N_IN = 3 * ATT_WIDTH + POOL_WIDTH + 2 * D_MODEL
NORM_EPS = 1e-6
ALIBI_MAX_BIAS = 8.0
POOL_HALO = max(POOL_WINDOWS)

VMEM_LIMIT_BYTES = 56 * 1024 * 1024
LANES = 128
MASKED_SCORE = -1e30

ROW_TILE = 512
MAX_DILATION = max(d for _, d in ATT_GROUPS)
ATT_TILE = BLK * MAX_DILATION
MERGE_MOD = 4
MERGE_ROWS = 128
LOG2_E = 1.4426950408889634
ATT_UNROLL = 8

ZN_WIDTH = QKV_WIDTH + POOL_WIDTH
ZG_WIDTH = 2 * D_MODEL
NAT_WIDTH = ZN_WIDTH + ZG_WIDTH

BF16 = jnp.bfloat16
F32 = jnp.float32


def _rms_scale(x):
    return lax.rsqrt(jnp.mean(x * x, axis=-1, keepdims=True) + NORM_EPS)


def _in_proj_kernel(x_ref, g_ref, w_ref, zn_ref, zg_ref, *rest):
    dil_refs, slab_ref = rest[:-1], rest[-1]
    x = x_ref[...]
    u = (x * _rms_scale(x) * g_ref[...]).astype(BF16)
    lo = 0
    for out_ref, width in ((zn_ref, QKV_WIDTH), (zn_ref, POOL_WIDTH),
                           (zg_ref, D_MODEL), (zg_ref, D_MODEL)):
        off = lo - (0 if out_ref is zn_ref else ZN_WIDTH)
        acc = jnp.dot(u, w_ref[:, lo:lo + width], preferred_element_type=F32)
        out_ref[:, off:off + width] = acc.astype(BF16)
        lo += width
    n_slabs = QKV_WIDTH // LANES
    for gi, z_ref in enumerate(dil_refs):
        dilation = ATT_GROUPS[gi + 1][1]
        lo = NAT_WIDTH + gi * QKV_WIDTH
        acc = jnp.dot(u, w_ref[:, lo:lo + QKV_WIDTH], preferred_element_type=F32)
        for sl in range(n_slabs):
            slab_ref[sl] = acc[:, sl * LANES:(sl + 1) * LANES]
        for r in range(dilation):
            for sl in range(n_slabs):
                rows = slab_ref[sl, pl.ds(r, ROW_TILE // dilation, stride=dilation), :]
                z_ref[r, :, sl * LANES:(sl + 1) * LANES] = rows.astype(BF16)


def _in_proj(x2d, g, w_perm):
    s = x2d.shape[0]
    dils = [d for _, d in ATT_GROUPS[1:]]
    out_shape = [jax.ShapeDtypeStruct((s, ZN_WIDTH), BF16), jax.ShapeDtypeStruct((s, ZG_WIDTH), BF16)]
    out_specs = [pl.BlockSpec((ROW_TILE, ZN_WIDTH), lambda i: (i, 0)),
                 pl.BlockSpec((ROW_TILE, ZG_WIDTH), lambda i: (i, 0))]
    for d in dils:
        out_shape.append(jax.ShapeDtypeStruct((d, s // d, QKV_WIDTH), BF16))
        out_specs.append(pl.BlockSpec((d, ROW_TILE // d, QKV_WIDTH), lambda i: (0, i, 0)))
    return pl.pallas_call(
        _in_proj_kernel,
        out_shape=out_shape,
        grid=(s // ROW_TILE,),
        in_specs=[
            pl.BlockSpec((ROW_TILE, D_MODEL), lambda i: (i, 0)),
            pl.BlockSpec((1, D_MODEL), lambda i: (0, 0)),
            pl.BlockSpec((D_MODEL, N_IN), lambda i: (0, 0)),
        ],
        out_specs=out_specs,
        scratch_shapes=[pltpu.VMEM((QKV_WIDTH // LANES, ROW_TILE, LANES), F32)],
        compiler_params=pltpu.CompilerParams(
            dimension_semantics=("arbitrary",), vmem_limit_bytes=VMEM_LIMIT_BYTES),
        name="in_proj",
    )(x2d, g, w_perm)


def _attend_block(qb, kk, vv, bias, head_masks_bf16, low_half):
    nh = HEADS_PER_GROUP
    q_stack = jnp.concatenate([qb * head_masks_bf16[h] for h in range(nh)], axis=0)
    s = lax.dot_general(q_stack, kk, (((1,), (1,)), ((), ())), preferred_element_type=F32)
    s = s + bias
    m = jnp.max(s, axis=-1, keepdims=True)
    p = jnp.exp2(s - m)
    l = jnp.sum(p, axis=-1, keepdims=True)
    pv = jnp.dot(p.astype(BF16), vv, preferred_element_type=F32)
    slabs = []
    for sl in range(ATT_MERGED // LANES):
        lanes = slice(sl * LANES, (sl + 1) * LANES)
        rows_a = slice(2 * sl * BLK, (2 * sl + 1) * BLK)
        rows_b = slice((2 * sl + 1) * BLK, (2 * sl + 2) * BLK)
        slabs.append((jnp.where(low_half, pv[rows_a, lanes], pv[rows_b, lanes]),
                      jnp.where(low_half, m[rows_a], m[rows_b]),
                      jnp.where(low_half, l[rows_a], l[rows_b])))
    return slabs


def _attn_kernel(*refs, slopes):
    z_refs = refs[0:N_GROUPS]
    kprev_refs = refs[N_GROUPS:3 * N_GROUPS:2]
    vprev_refs = refs[N_GROUPS + 1:3 * N_GROUPS:2]
    a_ref, bias_ref, bias0_ref, o_scr, m_scr, l_scr, nat_scr = refs[3 * N_GROUPS:]
    step = pl.program_id(0)
    nh = HEADS_PER_GROUP
    n_slabs = ATT_MERGED // LANES

    @pl.when(step == 0)
    def _():
        qi = lax.broadcasted_iota(jnp.int32, (BLK, 2 * BLK), 0)
        kj = lax.broadcasted_iota(jnp.int32, (BLK, 2 * BLK), 1)
        steps = BLK + qi - kj
        for g, (window, dilation) in enumerate(ATT_GROUPS):
            valid = (steps >= 0) & (steps <= window // dilation)
            dist = (steps * dilation).astype(F32)
            for h in range(nh):
                bias = jnp.where(valid, -(slopes[g * nh + h] * LOG2_E) * dist, MASKED_SCORE)
                bias_ref[g, h * BLK:(h + 1) * BLK, :] = bias
                bias0_ref[g, h * BLK:(h + 1) * BLK, :] = jnp.where(kj < BLK, MASKED_SCORE, bias)

    @pl.when(step == 1)
    def _():
        bias0_ref[...] = bias_ref[...]

    lane_head = lax.broadcasted_iota(jnp.int32, (BLK, ATT_MERGED), 1) // HEAD_DIM
    head_masks_bf16 = [(lane_head == h).astype(F32).astype(BF16) for h in range(nh)]
    low_half = lax.broadcasted_iota(jnp.int32, (BLK, LANES), 1) < HEAD_DIM
    attend = functools.partial(_attend_block, head_masks_bf16=head_masks_bf16, low_half=low_half)
    q_cols, k_cols, v_cols = (slice(c * ATT_MERGED, (c + 1) * ATT_MERGED) for c in range(3))

    def aligned(row):
        return row if isinstance(row, int) else pl.multiple_of(row, BLK)

    sub_rows = ATT_TILE // MERGE_MOD

    def store(g, dilation, r, b, slabs):
        if dilation in (1, MERGE_MOD):
            rows = pl.ds(aligned(r * (ATT_TILE // dilation) + b * BLK), BLK)
        else:
            per = dilation // MERGE_MOD
            rows = pl.ds((r % MERGE_MOD) * sub_rows + b * BLK * per + r // MERGE_MOD, BLK,
                         stride=per)
        for sl, (o, m, l) in enumerate(slabs):
            o_scr[g, sl, rows, :] = o
            m_scr[g, sl, rows, :] = m
            l_scr[g, sl, rows, :] = l

    for g, (_, dilation) in enumerate(ATT_GROUPS):
        n_blocks = ATT_TILE // dilation // BLK
        z_ref, kprev_ref, vprev_ref = z_refs[g], kprev_refs[g], vprev_refs[g]

        def load(r, row0, n_rows, cols, z_ref=z_ref, dilation=dilation):
            rows = pl.ds(aligned(row0), n_rows)
            return z_ref[rows, cols] if dilation == 1 else z_ref[r, rows, cols]

        def class_body(r, _, g=g, dilation=dilation, n_blocks=n_blocks, load=load,
                       kprev_ref=kprev_ref, vprev_ref=vprev_ref):
            kprev = kprev_ref[...] if dilation == 1 else kprev_ref[r]
            vprev = vprev_ref[...] if dilation == 1 else vprev_ref[r]
            kk = jnp.concatenate([kprev, load(r, 0, BLK, k_cols)], axis=0)
            vv = jnp.concatenate([vprev, load(r, 0, BLK, v_cols)], axis=0)
            store(g, dilation, r, 0, attend(load(r, 0, BLK, q_cols), kk, vv, bias0_ref[g]))
            for b in range(1, n_blocks):
                kk = load(r, (b - 1) * BLK, 2 * BLK, k_cols)
                vv = load(r, (b - 1) * BLK, 2 * BLK, v_cols)
                store(g, dilation, r, b, attend(load(r, b * BLK, BLK, q_cols), kk, vv, bias_ref[g]))
            return 0

        if dilation == 1:
            class_body(0, 0)
        else:
            lax.fori_loop(0, dilation, class_body, 0, unroll=max(1, ATT_UNROLL // n_blocks))

    chunks = sub_rows // MERGE_ROWS

    def merge_body(idx, _):
        c = idx // chunks
        row0 = (idx % chunks) * MERGE_ROWS
        nat_rows = pl.ds(row0 * MERGE_MOD + c, MERGE_ROWS, stride=MERGE_MOD)
        cls_rows = pl.ds(pl.multiple_of(c * sub_rows + row0, MERGE_ROWS), MERGE_ROWS)
        rows_of = lambda g: nat_rows if ATT_GROUPS[g][1] == 1 else cls_rows
        for sl in range(n_slabs):
            ms = [m_scr[g, sl, rows_of(g), :] for g in range(N_GROUPS)]
            mmax = functools.reduce(jnp.maximum, ms)
            ws = [jnp.exp2(m - mmax) for m in ms]
            num = sum(w * o_scr[g, sl, rows_of(g), :] for g, w in enumerate(ws))
            den = sum(w * l_scr[g, sl, rows_of(g), :] for g, w in enumerate(ws))
            nat_scr[sl, nat_rows, :] = num / den
        return 0

    lax.fori_loop(0, MERGE_MOD * chunks, merge_body, 0)
    for sl in range(n_slabs):
        a_ref[:, sl * LANES:(sl + 1) * LANES] = nat_scr[sl].astype(BF16)


def _attention(z_groups, slopes):
    s = z_groups[0].shape[0]
    cur_specs, prev_specs, prev_args = [], [], []
    for z, (_, d) in zip(z_groups, ATT_GROUPS):
        rows = ATT_TILE // d
        prev_row = lambda i, rows=rows: jnp.maximum(i * (rows // BLK) - 1, 0)
        for col in (1, 2):
            if d == 1:
                prev_specs.append(pl.BlockSpec((BLK, ATT_MERGED),
                                               lambda i, c=col, p=prev_row: (p(i), c)))
            else:
                prev_specs.append(pl.BlockSpec((d, BLK, ATT_MERGED),
                                               lambda i, c=col, p=prev_row: (0, p(i), c)))
            prev_args.append(z)
        if d == 1:
            cur_specs.append(pl.BlockSpec((rows, QKV_WIDTH), lambda i: (i, 0)))
        else:
            cur_specs.append(pl.BlockSpec((d, rows, QKV_WIDTH), lambda i: (0, i, 0)))
    scr_shape = (N_GROUPS, ATT_MERGED // LANES, ATT_TILE, LANES)
    bias_shape = (N_GROUPS, HEADS_PER_GROUP * BLK, 2 * BLK)
    return pl.pallas_call(
        functools.partial(_attn_kernel, slopes=slopes),
        out_shape=jax.ShapeDtypeStruct((s, ATT_MERGED), BF16),
        grid=(s // ATT_TILE,),
        in_specs=cur_specs + prev_specs,
        out_specs=pl.BlockSpec((ATT_TILE, ATT_MERGED), lambda i: (i, 0)),
        scratch_shapes=[pltpu.VMEM(bias_shape, F32), pltpu.VMEM(bias_shape, F32),
                        pltpu.VMEM(scr_shape, F32), pltpu.VMEM(scr_shape, F32),
                        pltpu.VMEM(scr_shape, F32), pltpu.VMEM(scr_shape[1:], F32)],
        compiler_params=pltpu.CompilerParams(
            dimension_semantics=("arbitrary",), vmem_limit_bytes=VMEM_LIMIT_BYTES),
        name="attn",
    )(*z_groups, *prev_args)


def _pool_weight_kernel(wgrp_ref, scale_ref, wpo_ref, wc_ref):
    c = POOL_GROUP_WIDTH
    for g in range(len(POOL_WINDOWS)):
        rows = slice(g * c, (g + 1) * c)
        wc = jnp.dot(wgrp_ref[g], scale_ref[rows, :] * wpo_ref[rows, :],
                     precision=lax.Precision.HIGHEST, preferred_element_type=F32)
        wc_ref[rows, :] = wc.astype(BF16)


def _pool_weight(w_grp, scale_col, wpo):
    return pl.pallas_call(
        _pool_weight_kernel,
        out_shape=jax.ShapeDtypeStruct((POOL_WIDTH, D_MODEL), BF16),
        compiler_params=pltpu.CompilerParams(vmem_limit_bytes=VMEM_LIMIT_BYTES),
        name="pool_weight",
    )(w_grp, scale_col, wpo)


def _pooled_minus_token(pf, prev, t):
    out = []
    for c in range(POOL_WIDTH // LANES):
        lanes = slice(c * LANES, (c + 1) * LANES)
        groups = [g for g in range(len(POOL_WINDOWS))
                  if g * POOL_GROUP_WIDTH < (c + 1) * LANES and (g + 1) * POOL_GROUP_WIDTH > c * LANES]
        acc = jnp.concatenate([prev[:, lanes], pf[:, lanes]], axis=0)
        w, means = 1, []
        for g in groups:
            while w < POOL_WINDOWS[g]:
                acc = acc + pltpu.roll(acc, w, axis=0)
                w *= 2
            count = jnp.minimum(t + 1, POOL_WINDOWS[g]).astype(F32)
            means.append(acc[POOL_HALO:, :] / count)
        pooled = means[-1]
        for g, mean in zip(groups[-2::-1], means[-2::-1]):
            lane = lax.broadcasted_iota(jnp.int32, mean.shape, 1)
            pooled = jnp.where(lane < (g + 1) * POOL_GROUP_WIDTH - c * LANES, mean, pooled)
        out.append(pooled - pf[:, lanes])
    return jnp.concatenate(out, axis=1)


def _mix_kernel(a_ref, pz_ref, pzprev_ref, ga_ref, gp_ref, x_ref, wao_ref, wc_ref, wout_ref,
                h_ref):
    i = pl.program_id(0)
    att = jnp.dot(a_ref[...], wao_ref[...], preferred_element_type=F32)

    pf = pz_ref[...].astype(F32)
    prev = jnp.where(i > 0, pzprev_ref[...].astype(F32), 0.0)
    t = i * ROW_TILE + lax.broadcasted_iota(jnp.int32, (ROW_TILE, 1), 0)
    pooled = _pooled_minus_token(pf, prev, t)
    pool = jnp.dot(pooled.astype(BF16), wc_ref[...], preferred_element_type=F32)

    merged = (jax.nn.sigmoid(ga_ref[...].astype(F32)) * att
              + jax.nn.sigmoid(gp_ref[...].astype(F32)) * pool)
    h_ref[...] = x_ref[...] + jnp.dot(merged.astype(BF16), wout_ref[...],
                                      preferred_element_type=F32)


def _mix(a, zn, zg, x2d, wao, wc, wout):
    s = x2d.shape[0]
    tm = ROW_TILE
    row = lambda i: (i, 0)
    const = lambda i: (0, 0)
    halo_blocks = tm // POOL_HALO
    assert QKV_WIDTH % POOL_WIDTH == 0
    pool_col = QKV_WIDTH // POOL_WIDTH
    return pl.pallas_call(
        _mix_kernel,
        out_shape=jax.ShapeDtypeStruct((s, D_MODEL), F32),
        grid=(s // tm,),
        in_specs=[
            pl.BlockSpec((tm, ATT_MERGED), row),
            pl.BlockSpec((tm, POOL_WIDTH), lambda i: (i, pool_col)),
            pl.BlockSpec((POOL_HALO, POOL_WIDTH),
                         lambda i: (jnp.maximum(i * halo_blocks - 1, 0), pool_col)),
            pl.BlockSpec((tm, D_MODEL), lambda i: (i, 0)),
            pl.BlockSpec((tm, D_MODEL), lambda i: (i, 1)),
            pl.BlockSpec((tm, D_MODEL), row),
            pl.BlockSpec((ATT_MERGED, D_MODEL), const),
            pl.BlockSpec((POOL_WIDTH, D_MODEL), const),
            pl.BlockSpec((D_MODEL, D_MODEL), const),
        ],
        out_specs=pl.BlockSpec((tm, D_MODEL), row),
        compiler_params=pltpu.CompilerParams(
            dimension_semantics=("arbitrary",), vmem_limit_bytes=VMEM_LIMIT_BYTES),
        name="mix",
    )(a, zn, zn, zg, zg, x2d, wao, wc, wout)


MLP_CHUNK = 1024


def _mlp_kernel(h_ref, g_ref, w1_ref, w2_ref, gf_ref, o_ref):
    h = h_ref[...]
    m = (h * _rms_scale(h) * g_ref[...]).astype(BF16)
    y = h
    for c in range(D_FF // MLP_CHUNK):
        cols = slice(c * MLP_CHUNK, (c + 1) * MLP_CHUNK)
        hid = jnp.dot(m, w1_ref[:, cols], preferred_element_type=F32)
        hid = jnp.square(jnp.maximum(hid, 0.0)).astype(BF16)
        y = y + jnp.dot(hid, w2_ref[cols, :], preferred_element_type=F32)
    o_ref[...] = y * _rms_scale(y) * gf_ref[...]


def _mlp(h, g, w1, w2, gf):
    s = h.shape[0]
    tm = ROW_TILE
    row = lambda i: (i, 0)
    const = lambda i: (0, 0)
    return pl.pallas_call(
        _mlp_kernel,
        out_shape=jax.ShapeDtypeStruct((s, D_MODEL), F32),
        grid=(s // tm,),
        in_specs=[
            pl.BlockSpec((tm, D_MODEL), row),
            pl.BlockSpec((1, D_MODEL), const),
            pl.BlockSpec((D_MODEL, D_FF), const),
            pl.BlockSpec((D_FF, D_MODEL), const),
            pl.BlockSpec((1, D_MODEL), const),
        ],
        out_specs=pl.BlockSpec((tm, D_MODEL), row),
        compiler_params=pltpu.CompilerParams(
            dimension_semantics=("arbitrary",), vmem_limit_bytes=VMEM_LIMIT_BYTES),
        name="mlp",
    )(h, g, w1, w2, gf)


def _alibi_slopes():
    return tuple(2.0 ** (-ALIBI_MAX_BIAS * (h + 1.0) / N_ATT_HEADS) for h in range(N_ATT_HEADS))


def _permute_w_in(w):
    q, k, v, rest = (w[:, :ATT_WIDTH] * (HEAD_DIM ** -0.5 * LOG2_E), w[:, ATT_WIDTH:2 * ATT_WIDTH],
                     w[:, 2 * ATT_WIDTH:3 * ATT_WIDTH], w[:, 3 * ATT_WIDTH:])
    grp = lambda t, g: t[:, g * ATT_MERGED:(g + 1) * ATT_MERGED]
    qkv = [jnp.concatenate([grp(q, g), grp(k, g), grp(v, g)], axis=1) for g in range(N_GROUPS)]
    return jnp.concatenate([qkv[0], rest] + qkv[1:], axis=1).astype(BF16)


def kernel(x, norm_mix_g, w_in, w_att_out, w_pool_grp, pool_scale, w_pool_out, w_out,
           norm_mlp_g, w_mlp_in, w_mlp_out, norm_final_g):
    batch, seq, d = x.shape
    assert d == D_MODEL and norm_mix_g.shape[0] == 1, "one layer of width D_MODEL"
    assert seq % ATT_TILE == 0 and seq % ROW_TILE == 0
    slopes = _alibi_slopes()
    outs = []
    for b in range(batch):
        h = x[b]
        zn, zg, z1, z2 = _in_proj(h, norm_mix_g[0][None, :], _permute_w_in(w_in[0]))
        a = _attention((zn, z1, z2), slopes)
        wc = _pool_weight(w_pool_grp[0], pool_scale[0][:, None], w_pool_out[0])
        h = _mix(a, zn, zg, h, w_att_out[0].astype(BF16), wc, w_out[0].astype(BF16))
        outs.append(_mlp(h, norm_mlp_g[0][None, :], w_mlp_in[0].astype(BF16),
                         w_mlp_out[0].astype(BF16), norm_final_g[None, :]))
    return outs[0][None] if batch == 1 else jnp.stack(outs, axis=0)
```

```python
import functools

import jax
import jax.numpy as jnp
from jax import lax
from jax.experimental import pallas as pl
from jax.experimental.pallas import tpu as pltpu

D_MODEL = 1024
HEAD_DIM = 64
ATT_GROUPS = ((128, 1), (512, 4), (2048, 16))
N_GROUPS = len(ATT_GROUPS)
HEADS_PER_GROUP = 4
N_ATT_HEADS = HEADS_PER_GROUP * N_GROUPS
ATT_WIDTH = N_ATT_HEADS * HEAD_DIM
ATT_MERGED = HEADS_PER_GROUP * HEAD_DIM
QKV_WIDTH = 3 * ATT_MERGED
BLK = 128
POOL_WINDOWS = (2, 4, 8, 16)
POOL_GROUP_WIDTH = 3 * D_MODEL // 16
POOL_WIDTH = POOL_GROUP_WIDTH * len(POOL_WINDOWS)
D_FF = 4 * D_MODEL
N_IN = 3 * ATT_WIDTH + POOL_WIDTH + 2 * D_MODEL
NORM_EPS = 1e-6
ALIBI_MAX_BIAS = 8.0
POOL_HALO = max(POOL_WINDOWS)

VMEM_LIMIT_BYTES = 56 * 1024 * 1024
LANES = 128
MASKED_SCORE = -1e30

ROW_TILE = 512
MAX_DILATION = max(d for _, d in ATT_GROUPS)
ATT_TILE = BLK * MAX_DILATION
MERGE_MOD = 4
MERGE_ROWS = 128
LOG2_E = 1.4426950408889634
ATT_UNROLL = 8

ZN_WIDTH = QKV_WIDTH + POOL_WIDTH
ZG_WIDTH = 2 * D_MODEL
NAT_WIDTH = ZN_WIDTH + ZG_WIDTH

BF16 = jnp.bfloat16
F32 = jnp.float32


def _rms_scale(x):
    return lax.rsqrt(jnp.mean(x * x, axis=-1, keepdims=True) + NORM_EPS)


def _stream_cast(pieces, stage_ref, sem_ref):
    def copy(k):
        return pltpu.make_async_copy(pieces[k][0], stage_ref.at[k % 2], sem_ref.at[k % 2])

    copy(0).start()
    for k, (_, dst, scale) in enumerate(pieces):
        if k + 1 < len(pieces):
            copy(k + 1).start()
        copy(k).wait()
        piece = stage_ref[k % 2]
        dst[...] = (piece if scale is None else piece * scale).astype(BF16)


HBM_SPEC = pl.BlockSpec(memory_space=pl.ANY)


def _w_in_pieces(w_hbm, w_ref):
    pieces = []
    for j in range(N_IN // ATT_MERGED):
        which, g = divmod(j, N_GROUPS)
        if which < 3:
            dst = which * ATT_MERGED + (0 if g == 0 else NAT_WIDTH + (g - 1) * QKV_WIDTH)
        else:
            dst = QKV_WIDTH + (j - 3 * N_GROUPS) * ATT_MERGED
        scale = HEAD_DIM ** -0.5 * LOG2_E if which == 0 else None
        pieces.append((w_hbm.at[:, j * ATT_MERGED:(j + 1) * ATT_MERGED],
                       w_ref.at[:, dst:dst + ATT_MERGED], scale))
    return pieces


def _in_proj_kernel(x_ref, g_ref, w_hbm, zn_ref, zg_ref, *rest):
    dil_refs, (slab_ref, w_ref, stage_ref, sem_ref) = rest[:-4], rest[-4:]

    @pl.when(pl.program_id(0) == 0)
    def _():
        _stream_cast(_w_in_pieces(w_hbm, w_ref), stage_ref, sem_ref)

    x = x_ref[...]
    u = (x * _rms_scale(x) * g_ref[...]).astype(BF16)
    lo = 0
    for out_ref, width in ((zn_ref, QKV_WIDTH), (zn_ref, POOL_WIDTH),
                           (zg_ref, D_MODEL), (zg_ref, D_MODEL)):
        off = lo - (0 if out_ref is zn_ref else ZN_WIDTH)
        acc = jnp.dot(u, w_ref[:, lo:lo + width], preferred_element_type=F32)
        out_ref[:, off:off + width] = acc.astype(BF16)
        lo += width
    n_slabs = QKV_WIDTH // LANES
    for gi, z_ref in enumerate(dil_refs):
        dilation = ATT_GROUPS[gi + 1][1]
        lo = NAT_WIDTH + gi * QKV_WIDTH
        acc = jnp.dot(u, w_ref[:, lo:lo + QKV_WIDTH], preferred_element_type=F32)
        for sl in range(n_slabs):
            slab_ref[sl] = acc[:, sl * LANES:(sl + 1) * LANES]
        for r in range(dilation):
            for sl in range(n_slabs):
                rows = slab_ref[sl, pl.ds(r, ROW_TILE // dilation, stride=dilation), :]
                z_ref[r, :, sl * LANES:(sl + 1) * LANES] = rows.astype(BF16)


def _in_proj(x2d, g, w_in):
    s = x2d.shape[0]
    dils = [d for _, d in ATT_GROUPS[1:]]
    out_shape = [jax.ShapeDtypeStruct((s, ZN_WIDTH), BF16), jax.ShapeDtypeStruct((s, ZG_WIDTH), BF16)]
    out_specs = [pl.BlockSpec((ROW_TILE, ZN_WIDTH), lambda i: (i, 0)),
                 pl.BlockSpec((ROW_TILE, ZG_WIDTH), lambda i: (i, 0))]
    for d in dils:
        out_shape.append(jax.ShapeDtypeStruct((d, s // d, QKV_WIDTH), BF16))
        out_specs.append(pl.BlockSpec((d, ROW_TILE // d, QKV_WIDTH), lambda i: (0, i, 0)))
    return pl.pallas_call(
        _in_proj_kernel,
        out_shape=out_shape,
        grid=(s // ROW_TILE,),
        in_specs=[
            pl.BlockSpec((ROW_TILE, D_MODEL), lambda i: (i, 0)),
            pl.BlockSpec((1, D_MODEL), lambda i: (0, 0)),
            HBM_SPEC,
        ],
        out_specs=out_specs,
        scratch_shapes=[pltpu.VMEM((QKV_WIDTH // LANES, ROW_TILE, LANES), F32),
                        pltpu.VMEM((D_MODEL, N_IN), BF16),
                        pltpu.VMEM((2, D_MODEL, ATT_MERGED), F32),
                        pltpu.SemaphoreType.DMA((2,))],
        compiler_params=pltpu.CompilerParams(
            dimension_semantics=("arbitrary",), vmem_limit_bytes=VMEM_LIMIT_BYTES),
        name="in_proj",
    )(x2d, g, w_in)


def _attend_block(qb, kk, vv, bias, head_masks_bf16, low_half):
    nh = HEADS_PER_GROUP
    q_stack = jnp.concatenate([qb * head_masks_bf16[h] for h in range(nh)], axis=0)
    s = lax.dot_general(q_stack, kk, (((1,), (1,)), ((), ())), preferred_element_type=F32)
    s = s + bias
    m = jnp.max(s, axis=-1, keepdims=True)
    p = jnp.exp2(s - m)
    l = jnp.sum(p, axis=-1, keepdims=True)
    pv = jnp.dot(p.astype(BF16), vv, preferred_element_type=F32)
    slabs = []
    for sl in range(ATT_MERGED // LANES):
        lanes = slice(sl * LANES, (sl + 1) * LANES)
        rows_a = slice(2 * sl * BLK, (2 * sl + 1) * BLK)
        rows_b = slice((2 * sl + 1) * BLK, (2 * sl + 2) * BLK)
        slabs.append((jnp.where(low_half, pv[rows_a, lanes], pv[rows_b, lanes]),
                      jnp.where(low_half, m[rows_a], m[rows_b]),
                      jnp.where(low_half, l[rows_a], l[rows_b])))
    return slabs


def _attn_kernel(*refs, slopes):
    z_refs = refs[0:N_GROUPS]
    kprev_refs = refs[N_GROUPS:3 * N_GROUPS:2]
    vprev_refs = refs[N_GROUPS + 1:3 * N_GROUPS:2]
    a_ref, bias_ref, bias0_ref, o_scr, m_scr, l_scr, nat_scr = refs[3 * N_GROUPS:]
    step = pl.program_id(0)
    nh = HEADS_PER_GROUP
    n_slabs = ATT_MERGED // LANES

    @pl.when(step == 0)
    def _():
        qi = lax.broadcasted_iota(jnp.int32, (BLK, 2 * BLK), 0)
        kj = lax.broadcasted_iota(jnp.int32, (BLK, 2 * BLK), 1)
        steps = BLK + qi - kj
        for g, (window, dilation) in enumerate(ATT_GROUPS):
            valid = (steps >= 0) & (steps <= window // dilation)
            dist = (steps * dilation).astype(F32)
            for h in range(nh):
                bias = jnp.where(valid, -(slopes[g * nh + h] * LOG2_E) * dist, MASKED_SCORE)
                bias_ref[g, h * BLK:(h + 1) * BLK, :] = bias
                bias0_ref[g, h * BLK:(h + 1) * BLK, :] = jnp.where(kj < BLK, MASKED_SCORE, bias)

    @pl.when(step == 1)
    def _():
        bias0_ref[...] = bias_ref[...]

    lane_head = lax.broadcasted_iota(jnp.int32, (BLK, ATT_MERGED), 1) // HEAD_DIM
    head_masks_bf16 = [(lane_head == h).astype(F32).astype(BF16) for h in range(nh)]
    low_half = lax.broadcasted_iota(jnp.int32, (BLK, LANES), 1) < HEAD_DIM
    attend = functools.partial(_attend_block, head_masks_bf16=head_masks_bf16, low_half=low_half)
    q_cols, k_cols, v_cols = (slice(c * ATT_MERGED, (c + 1) * ATT_MERGED) for c in range(3))

    def aligned(row):
        return row if isinstance(row, int) else pl.multiple_of(row, BLK)

    sub_rows = ATT_TILE // MERGE_MOD

    def store(g, dilation, r, b, slabs):
        if dilation in (1, MERGE_MOD):
            rows = pl.ds(aligned(r * (ATT_TILE // dilation) + b * BLK), BLK)
        else:
            per = dilation // MERGE_MOD
            rows = pl.ds((r % MERGE_MOD) * sub_rows + b * BLK * per + r // MERGE_MOD, BLK,
                         stride=per)
        for sl, (o, m, l) in enumerate(slabs):
            o_scr[g, sl, rows, :] = o
            m_scr[g, sl, rows, :] = m
            l_scr[g, sl, rows, :] = l

    for g, (_, dilation) in enumerate(ATT_GROUPS):
        n_blocks = ATT_TILE // dilation // BLK
        z_ref, kprev_ref, vprev_ref = z_refs[g], kprev_refs[g], vprev_refs[g]

        def load(r, row0, n_rows, cols, z_ref=z_ref, dilation=dilation):
            rows = pl.ds(aligned(row0), n_rows)
            return z_ref[rows, cols] if dilation == 1 else z_ref[r, rows, cols]

        def class_body(r, _, g=g, dilation=dilation, n_blocks=n_blocks, load=load,
                       kprev_ref=kprev_ref, vprev_ref=vprev_ref):
            kprev = kprev_ref[...] if dilation == 1 else kprev_ref[r]
            vprev = vprev_ref[...] if dilation == 1 else vprev_ref[r]
            kk = jnp.concatenate([kprev, load(r, 0, BLK, k_cols)], axis=0)
            vv = jnp.concatenate([vprev, load(r, 0, BLK, v_cols)], axis=0)
            store(g, dilation, r, 0, attend(load(r, 0, BLK, q_cols), kk, vv, bias0_ref[g]))
            for b in range(1, n_blocks):
                kk = load(r, (b - 1) * BLK, 2 * BLK, k_cols)
                vv = load(r, (b - 1) * BLK, 2 * BLK, v_cols)
                store(g, dilation, r, b, attend(load(r, b * BLK, BLK, q_cols), kk, vv, bias_ref[g]))
            return 0

        if dilation == 1:
            class_body(0, 0)
        else:
            lax.fori_loop(0, dilation, class_body, 0, unroll=max(1, ATT_UNROLL // n_blocks))

    chunks = sub_rows // MERGE_ROWS

    def merge_body(idx, _):
        c = idx // chunks
        row0 = (idx % chunks) * MERGE_ROWS
        nat_rows = pl.ds(row0 * MERGE_MOD + c, MERGE_ROWS, stride=MERGE_MOD)
        cls_rows = pl.ds(pl.multiple_of(c * sub_rows + row0, MERGE_ROWS), MERGE_ROWS)
        rows_of = lambda g: nat_rows if ATT_GROUPS[g][1] == 1 else cls_rows
        for sl in range(n_slabs):
            ms = [m_scr[g, sl, rows_of(g), :] for g in range(N_GROUPS)]
            mmax = functools.reduce(jnp.maximum, ms)
            ws = [jnp.exp2(m - mmax) for m in ms]
            num = sum(w * o_scr[g, sl, rows_of(g), :] for g, w in enumerate(ws))
            den = sum(w * l_scr[g, sl, rows_of(g), :] for g, w in enumerate(ws))
            nat_scr[sl, nat_rows, :] = num / den
        return 0

    lax.fori_loop(0, MERGE_MOD * chunks, merge_body, 0)
    for sl in range(n_slabs):
        a_ref[:, sl * LANES:(sl + 1) * LANES] = nat_scr[sl].astype(BF16)


def _attention(z_groups, slopes):
    s = z_groups[0].shape[0]
    cur_specs, prev_specs, prev_args = [], [], []
    for z, (_, d) in zip(z_groups, ATT_GROUPS):
        rows = ATT_TILE // d
        prev_row = lambda i, rows=rows: jnp.maximum(i * (rows // BLK) - 1, 0)
        for col in (1, 2):
            if d == 1:
                prev_specs.append(pl.BlockSpec((BLK, ATT_MERGED),
                                               lambda i, c=col, p=prev_row: (p(i), c)))
            else:
                prev_specs.append(pl.BlockSpec((d, BLK, ATT_MERGED),
                                               lambda i, c=col, p=prev_row: (0, p(i), c)))
            prev_args.append(z)
        if d == 1:
            cur_specs.append(pl.BlockSpec((rows, QKV_WIDTH), lambda i: (i, 0)))
        else:
            cur_specs.append(pl.BlockSpec((d, rows, QKV_WIDTH), lambda i: (0, i, 0)))
    scr_shape = (N_GROUPS, ATT_MERGED // LANES, ATT_TILE, LANES)
    bias_shape = (N_GROUPS, HEADS_PER_GROUP * BLK, 2 * BLK)
    return pl.pallas_call(
        functools.partial(_attn_kernel, slopes=slopes),
        out_shape=jax.ShapeDtypeStruct((s, ATT_MERGED), BF16),
        grid=(s // ATT_TILE,),
        in_specs=cur_specs + prev_specs,
        out_specs=pl.BlockSpec((ATT_TILE, ATT_MERGED), lambda i: (i, 0)),
        scratch_shapes=[pltpu.VMEM(bias_shape, F32), pltpu.VMEM(bias_shape, F32),
                        pltpu.VMEM(scr_shape, F32), pltpu.VMEM(scr_shape, F32),
                        pltpu.VMEM(scr_shape, F32), pltpu.VMEM(scr_shape[1:], F32)],
        compiler_params=pltpu.CompilerParams(
            dimension_semantics=("arbitrary",), vmem_limit_bytes=VMEM_LIMIT_BYTES),
        name="attn",
    )(*z_groups, *prev_args)


def _pool_weight_kernel(wgrp_ref, scale_ref, wpo_ref, wc_ref):
    c = POOL_GROUP_WIDTH
    for g in range(len(POOL_WINDOWS)):
        rows = slice(g * c, (g + 1) * c)
        wc = jnp.dot(wgrp_ref[g], scale_ref[rows, :] * wpo_ref[rows, :],
                     precision=lax.Precision.HIGHEST, preferred_element_type=F32)
        wc_ref[rows, :] = wc.astype(BF16)


def _pool_weight(w_grp, scale_col, wpo):
    return pl.pallas_call(
        _pool_weight_kernel,
        out_shape=jax.ShapeDtypeStruct((POOL_WIDTH, D_MODEL), BF16),
        compiler_params=pltpu.CompilerParams(vmem_limit_bytes=VMEM_LIMIT_BYTES),
        name="pool_weight",
    )(w_grp, scale_col, wpo)


def _pooled_minus_token(pf, prev, t):
    out = []
    for c in range(POOL_WIDTH // LANES):
        lanes = slice(c * LANES, (c + 1) * LANES)
        groups = [g for g in range(len(POOL_WINDOWS))
                  if g * POOL_GROUP_WIDTH < (c + 1) * LANES and (g + 1) * POOL_GROUP_WIDTH > c * LANES]
        acc = jnp.concatenate([prev[:, lanes], pf[:, lanes]], axis=0)
        w, means = 1, []
        for g in groups:
            while w < POOL_WINDOWS[g]:
                acc = acc + pltpu.roll(acc, w, axis=0)
                w *= 2
            count = jnp.minimum(t + 1, POOL_WINDOWS[g]).astype(F32)
            means.append(acc[POOL_HALO:, :] / count)
        pooled = means[-1]
        for g, mean in zip(groups[-2::-1], means[-2::-1]):
            lane = lax.broadcasted_iota(jnp.int32, mean.shape, 1)
            pooled = jnp.where(lane < (g + 1) * POOL_GROUP_WIDTH - c * LANES, mean, pooled)
        out.append(pooled - pf[:, lanes])
    return jnp.concatenate(out, axis=1)


def _mix_kernel(a_ref, pz_ref, pzprev_ref, ga_ref, gp_ref, x_ref, wao_hbm, wc_ref, wout_hbm,
                h_ref, wao_ref, wout_ref, stage_ref, sem_ref):
    i = pl.program_id(0)

    @pl.when(i == 0)
    def _():
        rows = stage_ref.shape[1]
        pieces = [(wao_hbm, wao_ref, None)]
        pieces += [(wout_hbm.at[r:r + rows, :], wout_ref.at[r:r + rows, :], None)
                   for r in range(0, D_MODEL, rows)]
        _stream_cast(pieces, stage_ref, sem_ref)

    att = jnp.dot(a_ref[...], wao_ref[...], preferred_element_type=F32)

    pf = pz_ref[...].astype(F32)
    prev = jnp.where(i > 0, pzprev_ref[...].astype(F32), 0.0)
    t = i * ROW_TILE + lax.broadcasted_iota(jnp.int32, (ROW_TILE, 1), 0)
    pooled = _pooled_minus_token(pf, prev, t)
    pool = jnp.dot(pooled.astype(BF16), wc_ref[...], preferred_element_type=F32)

    merged = (jax.nn.sigmoid(ga_ref[...].astype(F32)) * att
              + jax.nn.sigmoid(gp_ref[...].astype(F32)) * pool)
    h_ref[...] = x_ref[...] + jnp.dot(merged.astype(BF16), wout_ref[...],
                                      preferred_element_type=F32)


def _mix(a, zn, zg, x2d, wao, wc, wout):
    s = x2d.shape[0]
    tm = ROW_TILE
    row = lambda i: (i, 0)
    const = lambda i: (0, 0)
    halo_blocks = tm // POOL_HALO
    assert QKV_WIDTH % POOL_WIDTH == 0
    pool_col = QKV_WIDTH // POOL_WIDTH
    return pl.pallas_call(
        _mix_kernel,
        out_shape=jax.ShapeDtypeStruct((s, D_MODEL), F32),
        grid=(s // tm,),
        in_specs=[
            pl.BlockSpec((tm, ATT_MERGED), row),
            pl.BlockSpec((tm, POOL_WIDTH), lambda i: (i, pool_col)),
            pl.BlockSpec((POOL_HALO, POOL_WIDTH),
                         lambda i: (jnp.maximum(i * halo_blocks - 1, 0), pool_col)),
            pl.BlockSpec((tm, D_MODEL), lambda i: (i, 0)),
            pl.BlockSpec((tm, D_MODEL), lambda i: (i, 1)),
            pl.BlockSpec((tm, D_MODEL), row),
            HBM_SPEC,
            pl.BlockSpec((POOL_WIDTH, D_MODEL), const),
            HBM_SPEC,
        ],
        out_specs=pl.BlockSpec((tm, D_MODEL), row),
        scratch_shapes=[pltpu.VMEM((ATT_MERGED, D_MODEL), BF16),
                        pltpu.VMEM((D_MODEL, D_MODEL), BF16),
                        pltpu.VMEM((2, ATT_MERGED, D_MODEL), F32),
                        pltpu.SemaphoreType.DMA((2,))],
        compiler_params=pltpu.CompilerParams(
            dimension_semantics=("arbitrary",), vmem_limit_bytes=VMEM_LIMIT_BYTES),
        name="mix",
    )(a, zn, zn, zg, zg, x2d, wao, wc, wout)


MLP_CHUNK = 1024
WEIGHT_STAGE_ROWS = 512


def _mlp_kernel(h_ref, g_ref, w1_hbm, w2_hbm, gf_ref, o_ref, w1_ref, w2_ref, stage_ref, sem_ref):
    @pl.when(pl.program_id(0) == 0)
    def _():
        rows, cols = stage_ref.shape[1:]
        pieces = [(w.at[r:r + rows, c:c + cols], w_bf.at[r:r + rows, c:c + cols], None)
                  for w, w_bf in ((w1_hbm, w1_ref), (w2_hbm, w2_ref))
                  for r in range(0, w.shape[0], rows) for c in range(0, w.shape[1], cols)]
        _stream_cast(pieces, stage_ref, sem_ref)

    h = h_ref[...]
    m = (h * _rms_scale(h) * g_ref[...]).astype(BF16)
    y = h
    for c in range(D_FF // MLP_CHUNK):
        cols = slice(c * MLP_CHUNK, (c + 1) * MLP_CHUNK)
        hid = jnp.dot(m, w1_ref[:, cols], preferred_element_type=F32)
        hid = jnp.square(jnp.maximum(hid, 0.0)).astype(BF16)
        y = y + jnp.dot(hid, w2_ref[cols, :], preferred_element_type=F32)
    o_ref[...] = y * _rms_scale(y) * gf_ref[...]


def _mlp(h, g, w1, w2, gf):
    s = h.shape[0]
    tm = ROW_TILE
    row = lambda i: (i, 0)
    const = lambda i: (0, 0)
    return pl.pallas_call(
        _mlp_kernel,
        out_shape=jax.ShapeDtypeStruct((s, D_MODEL), F32),
        grid=(s // tm,),
        in_specs=[
            pl.BlockSpec((tm, D_MODEL), row),
            pl.BlockSpec((1, D_MODEL), const),
            HBM_SPEC,
            HBM_SPEC,
            pl.BlockSpec((1, D_MODEL), const),
        ],
        out_specs=pl.BlockSpec((tm, D_MODEL), row),
        scratch_shapes=[pltpu.VMEM((D_MODEL, D_FF), BF16),
                        pltpu.VMEM((D_FF, D_MODEL), BF16),
                        pltpu.VMEM((2, WEIGHT_STAGE_ROWS, D_MODEL), F32),
                        pltpu.SemaphoreType.DMA((2,))],
        compiler_params=pltpu.CompilerParams(
            dimension_semantics=("arbitrary",), vmem_limit_bytes=VMEM_LIMIT_BYTES),
        name="mlp",
    )(h, g, w1, w2, gf)


def _alibi_slopes():
    return tuple(2.0 ** (-ALIBI_MAX_BIAS * (h + 1.0) / N_ATT_HEADS) for h in range(N_ATT_HEADS))


def kernel(x, norm_mix_g, w_in, w_att_out, w_pool_grp, pool_scale, w_pool_out, w_out,
           norm_mlp_g, w_mlp_in, w_mlp_out, norm_final_g):
    batch, seq, d = x.shape
    assert d == D_MODEL and norm_mix_g.shape[0] == 1, "one layer of width D_MODEL"
    assert seq % ATT_TILE == 0 and seq % ROW_TILE == 0
    slopes = _alibi_slopes()
    outs = []
    for b in range(batch):
        h = x[b]
        zn, zg, z1, z2 = _in_proj(h, norm_mix_g[0][None, :], w_in[0])
        a = _attention((zn, z1, z2), slopes)
        wc = _pool_weight(w_pool_grp[0], pool_scale[0][:, None], w_pool_out[0])
        h = _mix(a, zn, zg, h, w_att_out[0], wc, w_out[0])
        outs.append(_mlp(h, norm_mlp_g[0][None, :], w_mlp_in[0], w_mlp_out[0],
                         norm_final_g[None, :]))
    return outs[0][None] if batch == 1 else jnp.stack(outs, axis=0)
```

```python
import functools

import jax
import jax.numpy as jnp
from jax import lax
from jax.experimental import pallas as pl
from jax.experimental.pallas import tpu as pltpu

D_MODEL = 1024
HEAD_DIM = 64
ATT_GROUPS = ((128, 1), (512, 4), (2048, 16))
N_GROUPS = len(ATT_GROUPS)
HEADS_PER_GROUP = 4
N_ATT_HEADS = HEADS_PER_GROUP * N_GROUPS
ATT_WIDTH = N_ATT_HEADS * HEAD_DIM
ATT_MERGED = HEADS_PER_GROUP * HEAD_DIM
QKV_WIDTH = 3 * ATT_MERGED
BLK = 128
POOL_WINDOWS = (2, 4, 8, 16)
POOL_GROUP_WIDTH = 3 * D_MODEL // 16
POOL_WIDTH = POOL_GROUP_WIDTH * len(POOL_WINDOWS)
D_FF = 4 * D_MODEL
N_IN = 3 * ATT_WIDTH + POOL_WIDTH + 2 * D_MODEL
NORM_EPS = 1e-6
ALIBI_MAX_BIAS = 8.0
POOL_HALO = max(POOL_WINDOWS)

VMEM_LIMIT_BYTES = 56 * 1024 * 1024
LANES = 128
MASKED_SCORE = -1e30

ROW_TILE = 512
MLP_CHUNK = 1024
WEIGHT_STAGE_ROWS = 512
MAX_DILATION = max(d for _, d in ATT_GROUPS)
ATT_TILE = BLK * MAX_DILATION
MERGE_MOD = 4
MERGE_ROWS = 128
LOG2_E = 1.4426950408889634
ATT_UNROLL = 8

ZN_WIDTH = QKV_WIDTH + POOL_WIDTH
ZG_WIDTH = 2 * D_MODEL
NAT_WIDTH = ZN_WIDTH + ZG_WIDTH

BF16 = jnp.bfloat16
F32 = jnp.float32


def _rms_scale(x):
    return lax.rsqrt(jnp.mean(x * x, axis=-1, keepdims=True) + NORM_EPS)


def _stream_cast(pieces, stage_ref, sem_ref):
    def copy(k):
        return pltpu.make_async_copy(pieces[k][0], stage_ref.at[k % 2], sem_ref.at[k % 2])

    copy(0).start()
    for k, (_, dst, scale) in enumerate(pieces):
        if k + 1 < len(pieces):
            copy(k + 1).start()
        copy(k).wait()
        piece = stage_ref[k % 2]
        dst[...] = (piece if scale is None else piece * scale).astype(BF16)


HBM_SPEC = pl.BlockSpec(memory_space=pl.ANY)


def _w_in_pieces(w_hbm, w_ref):
    pieces = []
    for j in range(N_IN // ATT_MERGED):
        which, g = divmod(j, N_GROUPS)
        if which < 3:
            dst = which * ATT_MERGED + (0 if g == 0 else NAT_WIDTH + (g - 1) * QKV_WIDTH)
        else:
            dst = QKV_WIDTH + (j - 3 * N_GROUPS) * ATT_MERGED
        scale = HEAD_DIM ** -0.5 * LOG2_E if which == 0 else None
        pieces.append((w_hbm.at[:, j * ATT_MERGED:(j + 1) * ATT_MERGED],
                       w_ref.at[:, dst:dst + ATT_MERGED], scale))
    return pieces


def _pooled_minus_token(pf, prev, t):
    out = []
    for c in range(POOL_WIDTH // LANES):
        lanes = slice(c * LANES, (c + 1) * LANES)
        groups = [g for g in range(len(POOL_WINDOWS))
                  if g * POOL_GROUP_WIDTH < (c + 1) * LANES and (g + 1) * POOL_GROUP_WIDTH > c * LANES]
        acc = jnp.concatenate([prev[:, lanes], pf[:, lanes]], axis=0)
        w, means = 1, []
        for g in groups:
            while w < POOL_WINDOWS[g]:
                acc = acc + pltpu.roll(acc, w, axis=0)
                w *= 2
            count = jnp.minimum(t + 1, POOL_WINDOWS[g]).astype(F32)
            means.append(acc[POOL_HALO:, :] / count)
        pooled = means[-1]
        for g, mean in zip(groups[-2::-1], means[-2::-1]):
            lane = lax.broadcasted_iota(jnp.int32, mean.shape, 1)
            pooled = jnp.where(lane < (g + 1) * POOL_GROUP_WIDTH - c * LANES, mean, pooled)
        out.append(pooled - pf[:, lanes])
    return jnp.concatenate(out, axis=1)


def _in_proj_kernel(x_ref, g_ref, w_hbm, zn_ref, zg_ref, *rest):
    dil_refs, (slab_ref, halo_ref, w_ref, stage_ref, sem_ref) = rest[:-5], rest[-5:]
    i = pl.program_id(0)

    @pl.when(i == 0)
    def _():
        _stream_cast(_w_in_pieces(w_hbm, w_ref), stage_ref, sem_ref)
        halo_ref[0] = jnp.zeros(halo_ref.shape[1:], F32)

    x = x_ref[...]
    u = (x * _rms_scale(x) * g_ref[...]).astype(BF16)
    n_slabs = QKV_WIDTH // LANES
    for gi, z_ref in reversed(list(enumerate(dil_refs))):
        dilation = ATT_GROUPS[gi + 1][1]
        lo = NAT_WIDTH + gi * QKV_WIDTH
        acc = jnp.dot(u, w_ref[:, lo:lo + QKV_WIDTH], preferred_element_type=F32)
        for sl in range(n_slabs):
            slab_ref[gi, sl] = acc[:, sl * LANES:(sl + 1) * LANES]
        for r in range(dilation):
            for sl in range(n_slabs):
                rows = slab_ref[gi, sl, pl.ds(r, ROW_TILE // dilation, stride=dilation), :]
                z_ref[r, :, sl * LANES:(sl + 1) * LANES] = rows.astype(BF16)
    project = lambda lo, width: jnp.dot(u, w_ref[:, lo:lo + width], preferred_element_type=F32)
    zn_ref[:, 0:QKV_WIDTH] = project(0, QKV_WIDTH).astype(BF16)
    pz = project(QKV_WIDTH, POOL_WIDTH)
    t = i * ROW_TILE + lax.broadcasted_iota(jnp.int32, (ROW_TILE, 1), 0)
    zn_ref[:, QKV_WIDTH:ZN_WIDTH] = _pooled_minus_token(pz, halo_ref[i % 2], t).astype(BF16)
    halo_ref[(i + 1) % 2] = pz[ROW_TILE - POOL_HALO:, :]
    for c in range(ZG_WIDTH // D_MODEL):
        gate = jax.nn.sigmoid(project(ZN_WIDTH + c * D_MODEL, D_MODEL))
        zg_ref[:, c * D_MODEL:(c + 1) * D_MODEL] = gate.astype(BF16)


def _in_proj(x2d, g, w_in):
    s = x2d.shape[0]
    dils = [d for _, d in ATT_GROUPS[1:]]
    out_shape = [jax.ShapeDtypeStruct((s, ZN_WIDTH), BF16), jax.ShapeDtypeStruct((s, ZG_WIDTH), BF16)]
    out_specs = [pl.BlockSpec((ROW_TILE, ZN_WIDTH), lambda i: (i, 0)),
                 pl.BlockSpec((ROW_TILE, ZG_WIDTH), lambda i: (i, 0))]
    for d in dils:
        out_shape.append(jax.ShapeDtypeStruct((d, s // d, QKV_WIDTH), BF16))
        out_specs.append(pl.BlockSpec((d, ROW_TILE // d, QKV_WIDTH), lambda i: (0, i, 0)))
    return pl.pallas_call(
        _in_proj_kernel,
        out_shape=out_shape,
        grid=(s // ROW_TILE,),
        in_specs=[
            pl.BlockSpec((ROW_TILE, D_MODEL), lambda i: (i, 0)),
            pl.BlockSpec((1, D_MODEL), lambda i: (0, 0)),
            HBM_SPEC,
        ],
        out_specs=out_specs,
        scratch_shapes=[pltpu.VMEM((len(dils), QKV_WIDTH // LANES, ROW_TILE, LANES), F32),
                        pltpu.VMEM((2, POOL_HALO, POOL_WIDTH), F32),
                        pltpu.VMEM((D_MODEL, N_IN), BF16),
                        pltpu.VMEM((2, D_MODEL, ATT_MERGED), F32),
                        pltpu.SemaphoreType.DMA((2,))],
        compiler_params=pltpu.CompilerParams(
            dimension_semantics=("arbitrary",), vmem_limit_bytes=VMEM_LIMIT_BYTES),
        name="in_proj",
    )(x2d, g, w_in)


def _attend_block(qb, kk, vv, bias, head_masks_bf16, low_half):
    nh = HEADS_PER_GROUP
    q_stack = jnp.concatenate([qb * head_masks_bf16[h] for h in range(nh)], axis=0)
    s = lax.dot_general(q_stack, kk, (((1,), (1,)), ((), ())), preferred_element_type=F32)
    s = s + bias
    m = jnp.max(s, axis=-1, keepdims=True)
    p = jnp.exp2(s - m)
    l = jnp.sum(p, axis=-1, keepdims=True)
    pv = jnp.dot(p.astype(BF16), vv, preferred_element_type=F32)
    slabs = []
    for sl in range(ATT_MERGED // LANES):
        lanes = slice(sl * LANES, (sl + 1) * LANES)
        rows_a = slice(2 * sl * BLK, (2 * sl + 1) * BLK)
        rows_b = slice((2 * sl + 1) * BLK, (2 * sl + 2) * BLK)
        slabs.append((jnp.where(low_half, pv[rows_a, lanes], pv[rows_b, lanes]),
                      jnp.where(low_half, m[rows_a], m[rows_b]),
                      jnp.where(low_half, l[rows_a], l[rows_b])))
    return slabs


def _attn_kernel(*refs, slopes):
    z_refs = refs[0:N_GROUPS]
    kprev_refs = refs[N_GROUPS:3 * N_GROUPS:2]
    vprev_refs = refs[N_GROUPS + 1:3 * N_GROUPS:2]
    a_ref, bias_ref, bias0_ref, o_scr, m_scr, l_scr, nat_scr = refs[3 * N_GROUPS:]
    step = pl.program_id(0)
    nh = HEADS_PER_GROUP
    n_slabs = ATT_MERGED // LANES

    @pl.when(step == 0)
    def _():
        qi = lax.broadcasted_iota(jnp.int32, (BLK, 2 * BLK), 0)
        kj = lax.broadcasted_iota(jnp.int32, (BLK, 2 * BLK), 1)
        steps = BLK + qi - kj
        for g, (window, dilation) in enumerate(ATT_GROUPS):
            valid = (steps >= 0) & (steps <= window // dilation)
            dist = (steps * dilation).astype(F32)
            for h in range(nh):
                bias = jnp.where(valid, -(slopes[g * nh + h] * LOG2_E) * dist, MASKED_SCORE)
                bias_ref[g, h * BLK:(h + 1) * BLK, :] = bias
                bias0_ref[g, h * BLK:(h + 1) * BLK, :] = jnp.where(kj < BLK, MASKED_SCORE, bias)

    @pl.when(step == 1)
    def _():
        bias0_ref[...] = bias_ref[...]

    lane_head = lax.broadcasted_iota(jnp.int32, (BLK, ATT_MERGED), 1) // HEAD_DIM
    head_masks_bf16 = [(lane_head == h).astype(F32).astype(BF16) for h in range(nh)]
    low_half = lax.broadcasted_iota(jnp.int32, (BLK, LANES), 1) < HEAD_DIM
    attend = functools.partial(_attend_block, head_masks_bf16=head_masks_bf16, low_half=low_half)
    q_cols, k_cols, v_cols = (slice(c * ATT_MERGED, (c + 1) * ATT_MERGED) for c in range(3))

    def aligned(row):
        return row if isinstance(row, int) else pl.multiple_of(row, BLK)

    sub_rows = ATT_TILE // MERGE_MOD

    def store(g, dilation, r, b, slabs):
        if dilation in (1, MERGE_MOD):
            rows = pl.ds(aligned(r * (ATT_TILE // dilation) + b * BLK), BLK)
        else:
            per = dilation // MERGE_MOD
            rows = pl.ds((r % MERGE_MOD) * sub_rows + b * BLK * per + r // MERGE_MOD, BLK,
                         stride=per)
        for sl, (o, m, l) in enumerate(slabs):
            o_scr[g, sl, rows, :] = o
            m_scr[g, sl, rows, :] = m
            l_scr[g, sl, rows, :] = l

    for g, (_, dilation) in enumerate(ATT_GROUPS):
        n_blocks = ATT_TILE // dilation // BLK
        z_ref, kprev_ref, vprev_ref = z_refs[g], kprev_refs[g], vprev_refs[g]

        def load(r, row0, n_rows, cols, z_ref=z_ref, dilation=dilation):
            rows = pl.ds(aligned(row0), n_rows)
            return z_ref[rows, cols] if dilation == 1 else z_ref[r, rows, cols]

        def class_body(r, _, g=g, dilation=dilation, n_blocks=n_blocks, load=load,
                       kprev_ref=kprev_ref, vprev_ref=vprev_ref):
            kprev = kprev_ref[...] if dilation == 1 else kprev_ref[r]
            vprev = vprev_ref[...] if dilation == 1 else vprev_ref[r]
            kk = jnp.concatenate([kprev, load(r, 0, BLK, k_cols)], axis=0)
            vv = jnp.concatenate([vprev, load(r, 0, BLK, v_cols)], axis=0)
            store(g, dilation, r, 0, attend(load(r, 0, BLK, q_cols), kk, vv, bias0_ref[g]))
            for b in range(1, n_blocks):
                kk = load(r, (b - 1) * BLK, 2 * BLK, k_cols)
                vv = load(r, (b - 1) * BLK, 2 * BLK, v_cols)
                store(g, dilation, r, b, attend(load(r, b * BLK, BLK, q_cols), kk, vv, bias_ref[g]))
            return 0

        if dilation == 1:
            class_body(0, 0)
        else:
            lax.fori_loop(0, dilation, class_body, 0, unroll=max(1, ATT_UNROLL // n_blocks))

    chunks = sub_rows // MERGE_ROWS

    def merge_body(idx, _):
        c = idx // chunks
        row0 = (idx % chunks) * MERGE_ROWS
        nat_rows = pl.ds(row0 * MERGE_MOD + c, MERGE_ROWS, stride=MERGE_MOD)
        cls_rows = pl.ds(pl.multiple_of(c * sub_rows + row0, MERGE_ROWS), MERGE_ROWS)
        rows_of = lambda g: nat_rows if ATT_GROUPS[g][1] == 1 else cls_rows
        for sl in range(n_slabs):
            ms = [m_scr[g, sl, rows_of(g), :] for g in range(N_GROUPS)]
            mmax = functools.reduce(jnp.maximum, ms)
            ws = [jnp.exp2(m - mmax) for m in ms]
            num = sum(w * o_scr[g, sl, rows_of(g), :] for g, w in enumerate(ws))
            den = sum(w * l_scr[g, sl, rows_of(g), :] for g, w in enumerate(ws))
            nat_scr[sl, nat_rows, :] = num / den
        return 0

    lax.fori_loop(0, MERGE_MOD * chunks, merge_body, 0)
    for sl in range(n_slabs):
        a_ref[:, sl * LANES:(sl + 1) * LANES] = nat_scr[sl].astype(BF16)


def _attention(z_groups, slopes):
    s = z_groups[0].shape[0]
    cur_specs, prev_specs, prev_args = [], [], []
    for z, (_, d) in zip(z_groups, ATT_GROUPS):
        rows = ATT_TILE // d
        prev_row = lambda i, rows=rows: jnp.maximum(i * (rows // BLK) - 1, 0)
        for col in (1, 2):
            if d == 1:
                prev_specs.append(pl.BlockSpec((BLK, ATT_MERGED),
                                               lambda i, c=col, p=prev_row: (p(i), c)))
            else:
                prev_specs.append(pl.BlockSpec((d, BLK, ATT_MERGED),
                                               lambda i, c=col, p=prev_row: (0, p(i), c)))
            prev_args.append(z)
        if d == 1:
            cur_specs.append(pl.BlockSpec((rows, QKV_WIDTH), lambda i: (i, 0)))
        else:
            cur_specs.append(pl.BlockSpec((d, rows, QKV_WIDTH), lambda i: (0, i, 0)))
    scr_shape = (N_GROUPS, ATT_MERGED // LANES, ATT_TILE, LANES)
    bias_shape = (N_GROUPS, HEADS_PER_GROUP * BLK, 2 * BLK)
    return pl.pallas_call(
        functools.partial(_attn_kernel, slopes=slopes),
        out_shape=jax.ShapeDtypeStruct((s, ATT_MERGED), BF16),
        grid=(s // ATT_TILE,),
        in_specs=cur_specs + prev_specs,
        out_specs=pl.BlockSpec((ATT_TILE, ATT_MERGED), lambda i: (i, 0)),
        scratch_shapes=[pltpu.VMEM(bias_shape, F32), pltpu.VMEM(bias_shape, F32),
                        pltpu.VMEM(scr_shape, F32), pltpu.VMEM(scr_shape, F32),
                        pltpu.VMEM(scr_shape, F32), pltpu.VMEM(scr_shape[1:], F32)],
        compiler_params=pltpu.CompilerParams(
            dimension_semantics=("arbitrary",), vmem_limit_bytes=VMEM_LIMIT_BYTES),
        name="attn",
    )(*z_groups, *prev_args)


def _pool_weight_kernel(wgrp_ref, scale_ref, wpo_ref, wc_ref):
    c = POOL_GROUP_WIDTH
    for g in range(len(POOL_WINDOWS)):
        rows = slice(g * c, (g + 1) * c)
        wc = jnp.dot(wgrp_ref[g], scale_ref[rows, :] * wpo_ref[rows, :],
                     precision=lax.Precision.HIGHEST, preferred_element_type=F32)
        wc_ref[rows, :] = wc.astype(BF16)


def _pool_weight(w_grp, scale_col, wpo):
    return pl.pallas_call(
        _pool_weight_kernel,
        out_shape=jax.ShapeDtypeStruct((POOL_WIDTH, D_MODEL), BF16),
        compiler_params=pltpu.CompilerParams(vmem_limit_bytes=VMEM_LIMIT_BYTES),
        name="pool_weight",
    )(w_grp, scale_col, wpo)


def _mix_kernel(a_ref, pooled_ref, ga_ref, gp_ref, x_ref, wao_hbm, wc_ref, wout_hbm,
                h_ref, wao_ref, wout_ref, stage_ref, sem_ref):
    i = pl.program_id(0)

    @pl.when(i == 0)
    def _():
        rows = stage_ref.shape[1]
        pieces = [(wao_hbm, wao_ref, None)]
        pieces += [(wout_hbm.at[r:r + rows, :], wout_ref.at[r:r + rows, :], None)
                   for r in range(0, D_MODEL, rows)]
        _stream_cast(pieces, stage_ref, sem_ref)

    att = jnp.dot(a_ref[...], wao_ref[...], preferred_element_type=F32)
    pool = jnp.dot(pooled_ref[...], wc_ref[...], preferred_element_type=F32)
    merged = ga_ref[...].astype(F32) * att + gp_ref[...].astype(F32) * pool
    h_ref[...] = x_ref[...] + jnp.dot(merged.astype(BF16), wout_ref[...],
                                      preferred_element_type=F32)


def _mix(a, zn, zg, x2d, wao, wc, wout):
    s = x2d.shape[0]
    tm = ROW_TILE
    row = lambda i: (i, 0)
    const = lambda i: (0, 0)
    assert QKV_WIDTH % POOL_WIDTH == 0
    pool_col = QKV_WIDTH // POOL_WIDTH
    return pl.pallas_call(
        _mix_kernel,
        out_shape=jax.ShapeDtypeStruct((s, D_MODEL), F32),
        grid=(s // tm,),
        in_specs=[
            pl.BlockSpec((tm, ATT_MERGED), row),
            pl.BlockSpec((tm, POOL_WIDTH), lambda i: (i, pool_col)),
            pl.BlockSpec((tm, D_MODEL), lambda i: (i, 0)),
            pl.BlockSpec((tm, D_MODEL), lambda i: (i, 1)),
            pl.BlockSpec((tm, D_MODEL), row),
            HBM_SPEC,
            pl.BlockSpec((POOL_WIDTH, D_MODEL), const),
            HBM_SPEC,
        ],
        out_specs=pl.BlockSpec((tm, D_MODEL), row),
        scratch_shapes=[pltpu.VMEM((ATT_MERGED, D_MODEL), BF16),
                        pltpu.VMEM((D_MODEL, D_MODEL), BF16),
                        pltpu.VMEM((2, ATT_MERGED, D_MODEL), F32),
                        pltpu.SemaphoreType.DMA((2,))],
        compiler_params=pltpu.CompilerParams(
            dimension_semantics=("arbitrary",), vmem_limit_bytes=VMEM_LIMIT_BYTES),
        name="mix",
    )(a, zn, zg, zg, x2d, wao, wc, wout)


def _mlp_kernel(h_ref, g_ref, w1_hbm, w2_hbm, gf_ref, o_ref, w1_ref, w2_ref, stage_ref, sem_ref):
    @pl.when(pl.program_id(0) == 0)
    def _():
        rows, cols = stage_ref.shape[1:]
        pieces = [(w.at[r:r + rows, c:c + cols], w_bf.at[r:r + rows, c:c + cols], None)
                  for w, w_bf in ((w1_hbm, w1_ref), (w2_hbm, w2_ref))
                  for r in range(0, w.shape[0], rows) for c in range(0, w.shape[1], cols)]
        _stream_cast(pieces, stage_ref, sem_ref)

    h = h_ref[...]
    m = (h * _rms_scale(h) * g_ref[...]).astype(BF16)
    y = h
    for c in range(D_FF // MLP_CHUNK):
        cols = slice(c * MLP_CHUNK, (c + 1) * MLP_CHUNK)
        hid = jnp.dot(m, w1_ref[:, cols], preferred_element_type=F32)
        hid = jnp.square(jnp.maximum(hid, 0.0)).astype(BF16)
        y = y + jnp.dot(hid, w2_ref[cols, :], preferred_element_type=F32)
    o_ref[...] = y * _rms_scale(y) * gf_ref[...]


def _mlp(h, g, w1, w2, gf):
    s = h.shape[0]
    tm = ROW_TILE
    row = lambda i: (i, 0)
    const = lambda i: (0, 0)
    return pl.pallas_call(
        _mlp_kernel,
        out_shape=jax.ShapeDtypeStruct((s, D_MODEL), F32),
        grid=(s // tm,),
        in_specs=[
            pl.BlockSpec((tm, D_MODEL), row),
            pl.BlockSpec((1, D_MODEL), const),
            HBM_SPEC,
            HBM_SPEC,
            pl.BlockSpec((1, D_MODEL), const),
        ],
        out_specs=pl.BlockSpec((tm, D_MODEL), row),
        scratch_shapes=[pltpu.VMEM((D_MODEL, D_FF), BF16),
                        pltpu.VMEM((D_FF, D_MODEL), BF16),
                        pltpu.VMEM((2, WEIGHT_STAGE_ROWS, D_MODEL), F32),
                        pltpu.SemaphoreType.DMA((2,))],
        compiler_params=pltpu.CompilerParams(
            dimension_semantics=("arbitrary",), vmem_limit_bytes=VMEM_LIMIT_BYTES),
        name="mlp",
    )(h, g, w1, w2, gf)


def _alibi_slopes():
    return tuple(2.0 ** (-ALIBI_MAX_BIAS * (h + 1.0) / N_ATT_HEADS) for h in range(N_ATT_HEADS))


def kernel(x, norm_mix_g, w_in, w_att_out, w_pool_grp, pool_scale, w_pool_out, w_out,
           norm_mlp_g, w_mlp_in, w_mlp_out, norm_final_g):
    batch, seq, d = x.shape
    assert d == D_MODEL and norm_mix_g.shape[0] == 1, "one layer of width D_MODEL"
    assert seq % ATT_TILE == 0 and seq % ROW_TILE == 0
    slopes = _alibi_slopes()
    outs = []
    for b in range(batch):
        h = x[b]
        zn, zg, z1, z2 = _in_proj(h, norm_mix_g[0][None, :], w_in[0])
        a = _attention((zn, z1, z2), slopes)
        wc = _pool_weight(w_pool_grp[0], pool_scale[0][:, None], w_pool_out[0])
        h = _mix(a, zn, zg, h, w_att_out[0], wc, w_out[0])
        outs.append(_mlp(h, norm_mlp_g[0][None, :], w_mlp_in[0], w_mlp_out[0],
                         norm_final_g[None, :]))
    return outs[0][None] if batch == 1 else jnp.stack(outs, axis=0)
```

```python
import functools

import jax
import jax.numpy as jnp
from jax import lax
from jax.experimental import pallas as pl
from jax.experimental.pallas import tpu as pltpu

D_MODEL = 1024
HEAD_DIM = 64
ATT_GROUPS = ((128, 1), (512, 4), (2048, 16))
N_GROUPS = len(ATT_GROUPS)
HEADS_PER_GROUP = 4
N_ATT_HEADS = HEADS_PER_GROUP * N_GROUPS
ATT_WIDTH = N_ATT_HEADS * HEAD_DIM
ATT_MERGED = HEADS_PER_GROUP * HEAD_DIM
QKV_WIDTH = 3 * ATT_MERGED
BLK = 128
POOL_WINDOWS = (2, 4, 8, 16)
POOL_GROUP_WIDTH = 3 * D_MODEL // 16
POOL_WIDTH = POOL_GROUP_WIDTH * len(POOL_WINDOWS)
D_FF = 4 * D_MODEL
N_IN = 3 * ATT_WIDTH + POOL_WIDTH + 2 * D_MODEL
NORM_EPS = 1e-6
ALIBI_MAX_BIAS = 8.0
POOL_HALO = max(POOL_WINDOWS)

VMEM_LIMIT_BYTES = 56 * 1024 * 1024
LANES = 128
MASKED_SCORE = -1e30

ROW_TILE = 512
MLP_CHUNK = 1024
WEIGHT_STAGE_ROWS = 512
MAX_DILATION = max(d for _, d in ATT_GROUPS)
ATT_TILE = BLK * MAX_DILATION
MERGE_MOD = 4
MERGE_ROWS = 128
LOG2_E = 1.4426950408889634
ATT_UNROLL = 8

ZN_WIDTH = QKV_WIDTH + POOL_WIDTH
ZG_WIDTH = 2 * D_MODEL
NAT_WIDTH = ZN_WIDTH + ZG_WIDTH

BF16 = jnp.bfloat16
F32 = jnp.float32


def _rms_scale(x):
    return lax.rsqrt(jnp.mean(x * x, axis=-1, keepdims=True) + NORM_EPS)


def _stream_cast(pieces, stage_ref, sem_ref):
    def slot(k):
        return stage_ref.at[k % 2, pl.ds(0, pieces[k][0].shape[0])]

    def copy(k):
        return pltpu.make_async_copy(pieces[k][0], slot(k), sem_ref.at[k % 2])

    copy(0).start()
    for k, (_, dst, scale) in enumerate(pieces):
        if k + 1 < len(pieces):
            copy(k + 1).start()
        copy(k).wait()
        piece = slot(k)[...]
        dst[...] = (piece if scale is None else piece * scale).astype(BF16)


HBM_SPEC = pl.BlockSpec(memory_space=pl.ANY)


def _w_in_pieces(w_hbm, w_ref):
    pieces = []
    for j in range(N_IN // ATT_MERGED):
        which, g = divmod(j, N_GROUPS)
        if which < 3:
            dst = which * ATT_MERGED + (0 if g == 0 else NAT_WIDTH + (g - 1) * QKV_WIDTH)
        else:
            dst = QKV_WIDTH + (j - 3 * N_GROUPS) * ATT_MERGED
        scale = HEAD_DIM ** -0.5 * LOG2_E if which == 0 else None
        pieces.append((w_hbm.at[:, j * ATT_MERGED:(j + 1) * ATT_MERGED],
                       w_ref.at[:, dst:dst + ATT_MERGED], scale))
    return pieces


def _pooled_minus_token(pf, prev, t):
    out = []
    for c in range(POOL_WIDTH // LANES):
        lanes = slice(c * LANES, (c + 1) * LANES)
        groups = [g for g in range(len(POOL_WINDOWS))
                  if g * POOL_GROUP_WIDTH < (c + 1) * LANES and (g + 1) * POOL_GROUP_WIDTH > c * LANES]
        acc = jnp.concatenate([prev[:, lanes], pf[:, lanes]], axis=0)
        w, means = 1, []
        for g in groups:
            while w < POOL_WINDOWS[g]:
                acc = acc + pltpu.roll(acc, w, axis=0)
                w *= 2
            count = jnp.minimum(t + 1, POOL_WINDOWS[g]).astype(F32)
            means.append(acc[POOL_HALO:, :] / count)
        pooled = means[-1]
        for g, mean in zip(groups[-2::-1], means[-2::-1]):
            lane = lax.broadcasted_iota(jnp.int32, mean.shape, 1)
            pooled = jnp.where(lane < (g + 1) * POOL_GROUP_WIDTH - c * LANES, mean, pooled)
        out.append(pooled - pf[:, lanes])
    return jnp.concatenate(out, axis=1)


def _in_proj_kernel(x_ref, g_ref, w_hbm, zn_ref, zg_ref, *rest):
    dil_refs, (slab_ref, w_ref, stage_ref, sem_ref) = rest[:-4], rest[-4:]

    @pl.when(pl.program_id(0) == 0)
    def _():
        _stream_cast(_w_in_pieces(w_hbm, w_ref), stage_ref, sem_ref)

    x = x_ref[...]
    u = (x * _rms_scale(x) * g_ref[...]).astype(BF16)
    n_slabs = QKV_WIDTH // LANES
    for gi, z_ref in reversed(list(enumerate(dil_refs))):
        dilation = ATT_GROUPS[gi + 1][1]
        lo = NAT_WIDTH + gi * QKV_WIDTH
        acc = jnp.dot(u, w_ref[:, lo:lo + QKV_WIDTH], preferred_element_type=F32)
        for sl in range(n_slabs):
            slab_ref[gi, sl] = acc[:, sl * LANES:(sl + 1) * LANES]
        for r in range(dilation):
            for sl in range(n_slabs):
                rows = slab_ref[gi, sl, pl.ds(r, ROW_TILE // dilation, stride=dilation), :]
                z_ref[r, :, sl * LANES:(sl + 1) * LANES] = rows.astype(BF16)
    lo = 0
    for out_ref, width in ((zn_ref, QKV_WIDTH), (zn_ref, POOL_WIDTH),
                           (zg_ref, D_MODEL), (zg_ref, D_MODEL)):
        off = lo - (0 if out_ref is zn_ref else ZN_WIDTH)
        acc = jnp.dot(u, w_ref[:, lo:lo + width], preferred_element_type=F32)
        out_ref[:, off:off + width] = acc.astype(BF16)
        lo += width


def _in_proj(x2d, g, w_in):
    s = x2d.shape[0]
    dils = [d for _, d in ATT_GROUPS[1:]]
    out_shape = [jax.ShapeDtypeStruct((s, ZN_WIDTH), BF16), jax.ShapeDtypeStruct((s, ZG_WIDTH), BF16)]
    out_specs = [pl.BlockSpec((ROW_TILE, ZN_WIDTH), lambda i: (i, 0)),
                 pl.BlockSpec((ROW_TILE, ZG_WIDTH), lambda i: (i, 0))]
    for d in dils:
        out_shape.append(jax.ShapeDtypeStruct((d, s // d, QKV_WIDTH), BF16))
        out_specs.append(pl.BlockSpec((d, ROW_TILE // d, QKV_WIDTH), lambda i: (0, i, 0)))
    return pl.pallas_call(
        _in_proj_kernel,
        out_shape=out_shape,
        grid=(s // ROW_TILE,),
        in_specs=[
            pl.BlockSpec((ROW_TILE, D_MODEL), lambda i: (i, 0)),
            pl.BlockSpec((1, D_MODEL), lambda i: (0, 0)),
            HBM_SPEC,
        ],
        out_specs=out_specs,
        scratch_shapes=[pltpu.VMEM((len(dils), QKV_WIDTH // LANES, ROW_TILE, LANES), F32),
                        pltpu.VMEM((D_MODEL, N_IN), BF16),
                        pltpu.VMEM((2, D_MODEL, ATT_MERGED), F32),
                        pltpu.SemaphoreType.DMA((2,))],
        compiler_params=pltpu.CompilerParams(
            dimension_semantics=("arbitrary",), vmem_limit_bytes=VMEM_LIMIT_BYTES),
        name="in_proj",
    )(x2d, g, w_in)


def _attend_block(qb, kk, vv, bias, head_masks_bf16, low_half):
    nh = HEADS_PER_GROUP
    q_stack = jnp.concatenate([qb * head_masks_bf16[h] for h in range(nh)], axis=0)
    s = lax.dot_general(q_stack, kk, (((1,), (1,)), ((), ())), preferred_element_type=F32)
    s = s + bias
    m = jnp.max(s, axis=-1, keepdims=True)
    p = jnp.exp2(s - m)
    l = jnp.sum(p, axis=-1, keepdims=True)
    pv = jnp.dot(p.astype(BF16), vv, preferred_element_type=F32)
    slabs = []
    for sl in range(ATT_MERGED // LANES):
        lanes = slice(sl * LANES, (sl + 1) * LANES)
        rows_a = slice(2 * sl * BLK, (2 * sl + 1) * BLK)
        rows_b = slice((2 * sl + 1) * BLK, (2 * sl + 2) * BLK)
        slabs.append((jnp.where(low_half, pv[rows_a, lanes], pv[rows_b, lanes]),
                      jnp.where(low_half, m[rows_a], m[rows_b]),
                      jnp.where(low_half, l[rows_a], l[rows_b])))
    return slabs


def _attn_kernel(*refs, slopes):
    z_refs = refs[0:N_GROUPS]
    kprev_refs = refs[N_GROUPS:3 * N_GROUPS:2]
    vprev_refs = refs[N_GROUPS + 1:3 * N_GROUPS:2]
    a_ref, bias_ref, bias0_ref, o_scr, m_scr, l_scr, nat_scr = refs[3 * N_GROUPS:]
    step = pl.program_id(0)
    nh = HEADS_PER_GROUP
    n_slabs = ATT_MERGED // LANES

    @pl.when(step == 0)
    def _():
        qi = lax.broadcasted_iota(jnp.int32, (BLK, 2 * BLK), 0)
        kj = lax.broadcasted_iota(jnp.int32, (BLK, 2 * BLK), 1)
        steps = BLK + qi - kj
        for g, (window, dilation) in enumerate(ATT_GROUPS):
            valid = (steps >= 0) & (steps <= window // dilation)
            dist = (steps * dilation).astype(F32)
            for h in range(nh):
                bias = jnp.where(valid, -(slopes[g * nh + h] * LOG2_E) * dist, MASKED_SCORE)
                bias_ref[g, h * BLK:(h + 1) * BLK, :] = bias
                bias0_ref[g, h * BLK:(h + 1) * BLK, :] = jnp.where(kj < BLK, MASKED_SCORE, bias)

    @pl.when(step == 1)
    def _():
        bias0_ref[...] = bias_ref[...]

    lane_head = lax.broadcasted_iota(jnp.int32, (BLK, ATT_MERGED), 1) // HEAD_DIM
    head_masks_bf16 = [(lane_head == h).astype(F32).astype(BF16) for h in range(nh)]
    low_half = lax.broadcasted_iota(jnp.int32, (BLK, LANES), 1) < HEAD_DIM
    attend = functools.partial(_attend_block, head_masks_bf16=head_masks_bf16, low_half=low_half)
    q_cols, k_cols, v_cols = (slice(c * ATT_MERGED, (c + 1) * ATT_MERGED) for c in range(3))

    def aligned(row):
        return row if isinstance(row, int) else pl.multiple_of(row, BLK)

    sub_rows = ATT_TILE // MERGE_MOD

    def store(g, dilation, r, b, slabs):
        if dilation in (1, MERGE_MOD):
            rows = pl.ds(aligned(r * (ATT_TILE // dilation) + b * BLK), BLK)
        else:
            per = dilation // MERGE_MOD
            rows = pl.ds((r % MERGE_MOD) * sub_rows + b * BLK * per + r // MERGE_MOD, BLK,
                         stride=per)
        for sl, (o, m, l) in enumerate(slabs):
            o_scr[g, sl, rows, :] = o
            m_scr[g, sl, rows, :] = m
            l_scr[g, sl, rows, :] = l

    for g, (_, dilation) in enumerate(ATT_GROUPS):
        n_blocks = ATT_TILE // dilation // BLK
        z_ref, kprev_ref, vprev_ref = z_refs[g], kprev_refs[g], vprev_refs[g]

        def load(r, row0, n_rows, cols, z_ref=z_ref, dilation=dilation):
            rows = pl.ds(aligned(row0), n_rows)
            return z_ref[rows, cols] if dilation == 1 else z_ref[r, rows, cols]

        def class_body(r, _, g=g, dilation=dilation, n_blocks=n_blocks, load=load,
                       kprev_ref=kprev_ref, vprev_ref=vprev_ref):
            kprev = kprev_ref[...] if dilation == 1 else kprev_ref[r]
            vprev = vprev_ref[...] if dilation == 1 else vprev_ref[r]
            kk = jnp.concatenate([kprev, load(r, 0, BLK, k_cols)], axis=0)
            vv = jnp.concatenate([vprev, load(r, 0, BLK, v_cols)], axis=0)
            store(g, dilation, r, 0, attend(load(r, 0, BLK, q_cols), kk, vv, bias0_ref[g]))
            for b in range(1, n_blocks):
                kk = load(r, (b - 1) * BLK, 2 * BLK, k_cols)
                vv = load(r, (b - 1) * BLK, 2 * BLK, v_cols)
                store(g, dilation, r, b, attend(load(r, b * BLK, BLK, q_cols), kk, vv, bias_ref[g]))
            return 0

        if dilation == 1:
            class_body(0, 0)
        else:
            lax.fori_loop(0, dilation, class_body, 0, unroll=max(1, ATT_UNROLL // n_blocks))

    chunks = sub_rows // MERGE_ROWS

    def merge_body(idx, _):
        c = idx // chunks
        row0 = (idx % chunks) * MERGE_ROWS
        nat_rows = pl.ds(row0 * MERGE_MOD + c, MERGE_ROWS, stride=MERGE_MOD)
        cls_rows = pl.ds(pl.multiple_of(c * sub_rows + row0, MERGE_ROWS), MERGE_ROWS)
        rows_of = lambda g: nat_rows if ATT_GROUPS[g][1] == 1 else cls_rows
        for sl in range(n_slabs):
            ms = [m_scr[g, sl, rows_of(g), :] for g in range(N_GROUPS)]
            mmax = functools.reduce(jnp.maximum, ms)
            ws = [jnp.exp2(m - mmax) for m in ms]
            num = sum(w * o_scr[g, sl, rows_of(g), :] for g, w in enumerate(ws))
            den = sum(w * l_scr[g, sl, rows_of(g), :] for g, w in enumerate(ws))
            nat_scr[sl, nat_rows, :] = num / den
        return 0

    lax.fori_loop(0, MERGE_MOD * chunks, merge_body, 0)
    for sl in range(n_slabs):
        a_ref[:, sl * LANES:(sl + 1) * LANES] = nat_scr[sl].astype(BF16)


def _attention(z_groups, slopes):
    s = z_groups[0].shape[0]
    cur_specs, prev_specs, prev_args = [], [], []
    for z, (_, d) in zip(z_groups, ATT_GROUPS):
        rows = ATT_TILE // d
        prev_row = lambda i, rows=rows: jnp.maximum(i * (rows // BLK) - 1, 0)
        for col in (1, 2):
            if d == 1:
                prev_specs.append(pl.BlockSpec((BLK, ATT_MERGED),
                                               lambda i, c=col, p=prev_row: (p(i), c)))
            else:
                prev_specs.append(pl.BlockSpec((d, BLK, ATT_MERGED),
                                               lambda i, c=col, p=prev_row: (0, p(i), c)))
            prev_args.append(z)
        if d == 1:
            cur_specs.append(pl.BlockSpec((rows, QKV_WIDTH), lambda i: (i, 0)))
        else:
            cur_specs.append(pl.BlockSpec((d, rows, QKV_WIDTH), lambda i: (0, i, 0)))
    scr_shape = (N_GROUPS, ATT_MERGED // LANES, ATT_TILE, LANES)
    bias_shape = (N_GROUPS, HEADS_PER_GROUP * BLK, 2 * BLK)
    return pl.pallas_call(
        functools.partial(_attn_kernel, slopes=slopes),
        out_shape=jax.ShapeDtypeStruct((s, ATT_MERGED), BF16),
        grid=(s // ATT_TILE,),
        in_specs=cur_specs + prev_specs,
        out_specs=pl.BlockSpec((ATT_TILE, ATT_MERGED), lambda i: (i, 0)),
        scratch_shapes=[pltpu.VMEM(bias_shape, F32), pltpu.VMEM(bias_shape, F32),
                        pltpu.VMEM(scr_shape, F32), pltpu.VMEM(scr_shape, F32),
                        pltpu.VMEM(scr_shape, F32), pltpu.VMEM(scr_shape[1:], F32)],
        compiler_params=pltpu.CompilerParams(
            dimension_semantics=("arbitrary",), vmem_limit_bytes=VMEM_LIMIT_BYTES),
        name="attn",
    )(*z_groups, *prev_args)


def _pool_weight_kernel(wgrp_ref, scale_ref, wpo_ref, wc_ref):
    c = POOL_GROUP_WIDTH
    for g in range(len(POOL_WINDOWS)):
        rows = slice(g * c, (g + 1) * c)
        wc = jnp.dot(wgrp_ref[g], scale_ref[rows, :] * wpo_ref[rows, :],
                     precision=lax.Precision.HIGHEST, preferred_element_type=F32)
        wc_ref[rows, :] = wc.astype(BF16)


def _pool_weight(w_grp, scale_col, wpo):
    return pl.pallas_call(
        _pool_weight_kernel,
        out_shape=jax.ShapeDtypeStruct((POOL_WIDTH, D_MODEL), BF16),
        compiler_params=pltpu.CompilerParams(vmem_limit_bytes=VMEM_LIMIT_BYTES),
        name="pool_weight",
    )(w_grp, scale_col, wpo)


def _post_kernel(a_ref, pz_ref, pzprev_ref, ga_ref, gp_ref, x_ref, wao_hbm, wc_ref, wout_hbm,
                 g_ref, w1_hbm, w2_hbm, gf_ref, o_ref,
                 wao_ref, wout_ref, w1_ref, w2_ref, stage_ref, sem_ref):
    i = pl.program_id(0)

    @pl.when(i == 0)
    def _():
        max_rows, cols = stage_ref.shape[1:]
        pieces = []
        for w, w_bf in ((wao_hbm, wao_ref), (wout_hbm, wout_ref), (w1_hbm, w1_ref), (w2_hbm, w2_ref)):
            rows = min(max_rows, w.shape[0])
            pieces += [(w.at[r:r + rows, c:c + cols], w_bf.at[r:r + rows, c:c + cols], None)
                       for r in range(0, w.shape[0], rows) for c in range(0, w.shape[1], cols)]
        _stream_cast(pieces, stage_ref, sem_ref)

    att = jnp.dot(a_ref[...], wao_ref[...], preferred_element_type=F32)
    pf = pz_ref[...].astype(F32)
    prev = jnp.where(i > 0, pzprev_ref[...].astype(F32), 0.0)
    t = i * ROW_TILE + lax.broadcasted_iota(jnp.int32, (ROW_TILE, 1), 0)
    pooled = _pooled_minus_token(pf, prev, t)
    pool = jnp.dot(pooled.astype(BF16), wc_ref[...], preferred_element_type=F32)
    merged = (jax.nn.sigmoid(ga_ref[...].astype(F32)) * att
              + jax.nn.sigmoid(gp_ref[...].astype(F32)) * pool)
    h = x_ref[...] + jnp.dot(merged.astype(BF16), wout_ref[...], preferred_element_type=F32)

    m = (h * _rms_scale(h) * g_ref[...]).astype(BF16)
    y = h
    for c in range(D_FF // MLP_CHUNK):
        cols = slice(c * MLP_CHUNK, (c + 1) * MLP_CHUNK)
        hid = jnp.dot(m, w1_ref[:, cols], preferred_element_type=F32)
        hid = jnp.square(jnp.maximum(hid, 0.0)).astype(BF16)
        y = y + jnp.dot(hid, w2_ref[cols, :], preferred_element_type=F32)
    o_ref[...] = y * _rms_scale(y) * gf_ref[...]


def _post(a, zn, zg, x2d, wao, wc, wout, g_mlp, w1, w2, g_final):
    s = x2d.shape[0]
    tm = ROW_TILE
    row = lambda i: (i, 0)
    const = lambda i: (0, 0)
    halo_blocks = tm // POOL_HALO
    assert QKV_WIDTH % POOL_WIDTH == 0
    pool_col = QKV_WIDTH // POOL_WIDTH
    return pl.pallas_call(
        _post_kernel,
        out_shape=jax.ShapeDtypeStruct((s, D_MODEL), F32),
        grid=(s // tm,),
        in_specs=[
            pl.BlockSpec((tm, ATT_MERGED), row),
            pl.BlockSpec((tm, POOL_WIDTH), lambda i: (i, pool_col)),
            pl.BlockSpec((POOL_HALO, POOL_WIDTH),
                         lambda i: (jnp.maximum(i * halo_blocks - 1, 0), pool_col)),
            pl.BlockSpec((tm, D_MODEL), lambda i: (i, 0)),
            pl.BlockSpec((tm, D_MODEL), lambda i: (i, 1)),
            pl.BlockSpec((tm, D_MODEL), row),
            HBM_SPEC,
            pl.BlockSpec((POOL_WIDTH, D_MODEL), const),
            HBM_SPEC,
            pl.BlockSpec((1, D_MODEL), const),
            HBM_SPEC,
            HBM_SPEC,
            pl.BlockSpec((1, D_MODEL), const),
        ],
        out_specs=pl.BlockSpec((tm, D_MODEL), row),
        scratch_shapes=[pltpu.VMEM((ATT_MERGED, D_MODEL), BF16),
                        pltpu.VMEM((D_MODEL, D_MODEL), BF16),
                        pltpu.VMEM((D_MODEL, D_FF), BF16),
                        pltpu.VMEM((D_FF, D_MODEL), BF16),
                        pltpu.VMEM((2, WEIGHT_STAGE_ROWS, D_MODEL), F32),
                        pltpu.SemaphoreType.DMA((2,))],
        compiler_params=pltpu.CompilerParams(
            dimension_semantics=("arbitrary",), vmem_limit_bytes=VMEM_LIMIT_BYTES),
        name="post",
    )(a, zn, zn, zg, zg, x2d, wao, wc, wout, g_mlp, w1, w2, g_final)


def _alibi_slopes():
    return tuple(2.0 ** (-ALIBI_MAX_BIAS * (h + 1.0) / N_ATT_HEADS) for h in range(N_ATT_HEADS))


def kernel(x, norm_mix_g, w_in, w_att_out, w_pool_grp, pool_scale, w_pool_out, w_out,
           norm_mlp_g, w_mlp_in, w_mlp_out, norm_final_g):
    batch, seq, d = x.shape
    assert d == D_MODEL and norm_mix_g.shape[0] == 1, "one layer of width D_MODEL"
    assert seq % ATT_TILE == 0 and seq % ROW_TILE == 0
    slopes = _alibi_slopes()
    outs = []
    for b in range(batch):
        h = x[b]
        zn, zg, z1, z2 = _in_proj(h, norm_mix_g[0][None, :], w_in[0])
        a = _attention((zn, z1, z2), slopes)
        wc = _pool_weight(w_pool_grp[0], pool_scale[0][:, None], w_pool_out[0])
        outs.append(_post(a, zn, zg, h, w_att_out[0], wc, w_out[0], norm_mlp_g[0][None, :],
                          w_mlp_in[0], w_mlp_out[0], norm_final_g[None, :]))
    return outs[0][None] if batch == 1 else jnp.stack(outs, axis=0)
```

```python
import functools

import jax
import jax.numpy as jnp
from jax import lax
from jax.experimental import pallas as pl
from jax.experimental.pallas import tpu as pltpu

D_MODEL = 1024
HEAD_DIM = 64
ATT_GROUPS = ((128, 1), (512, 4), (2048, 16))
N_GROUPS = len(ATT_GROUPS)
HEADS_PER_GROUP = 4
N_ATT_HEADS = HEADS_PER_GROUP * N_GROUPS
ATT_WIDTH = N_ATT_HEADS * HEAD_DIM
ATT_MERGED = HEADS_PER_GROUP * HEAD_DIM
QKV_WIDTH = 3 * ATT_MERGED
BLK = 128
POOL_WINDOWS = (2, 4, 8, 16)
POOL_GROUP_WIDTH = 3 * D_MODEL // 16
POOL_WIDTH = POOL_GROUP_WIDTH * len(POOL_WINDOWS)
D_FF = 4 * D_MODEL
N_IN = 3 * ATT_WIDTH + POOL_WIDTH + 2 * D_MODEL
NORM_EPS = 1e-6
ALIBI_MAX_BIAS = 8.0
POOL_HALO = max(POOL_WINDOWS)

VMEM_LIMIT_BYTES = 56 * 1024 * 1024
LANES = 128
MASKED_SCORE = -1e30

ROW_TILE = 512
MLP_CHUNK = 1024
WEIGHT_STAGE_ROWS = 512
MAX_DILATION = max(d for _, d in ATT_GROUPS)
ATT_TILE = BLK * MAX_DILATION
MERGE_MOD = 4
MERGE_ROWS = 128
LOG2_E = 1.4426950408889634
ATT_UNROLL = 8

ZN_WIDTH = QKV_WIDTH + POOL_WIDTH
ZG_WIDTH = 2 * D_MODEL
NAT_WIDTH = ZN_WIDTH + ZG_WIDTH

BF16 = jnp.bfloat16
F32 = jnp.float32


def _rms_scale(x):
    return lax.rsqrt(jnp.mean(x * x, axis=-1, keepdims=True) + NORM_EPS)


def _stream_cast(pieces, stage_ref, sem_ref):
    def slot(k):
        return stage_ref.at[k % 2, pl.ds(0, pieces[k][0].shape[0])]

    def copy(k):
        return pltpu.make_async_copy(pieces[k][0], slot(k), sem_ref.at[k % 2])

    copy(0).start()
    for k, (_, dst, scale) in enumerate(pieces):
        if k + 1 < len(pieces):
            copy(k + 1).start()
        copy(k).wait()
        piece = slot(k)[...]
        dst[...] = (piece if scale is None else piece * scale).astype(BF16)


HBM_SPEC = pl.BlockSpec(memory_space=pl.ANY)


def _w_in_pieces(w_hbm, w_ref):
    pieces = []
    for j in range(N_IN // ATT_MERGED):
        which, g = divmod(j, N_GROUPS)
        if which < 3:
            dst = which * ATT_MERGED + (0 if g == 0 else NAT_WIDTH + (g - 1) * QKV_WIDTH)
        else:
            dst = QKV_WIDTH + (j - 3 * N_GROUPS) * ATT_MERGED
        scale = HEAD_DIM ** -0.5 * LOG2_E if which == 0 else None
        pieces.append((w_hbm.at[:, j * ATT_MERGED:(j + 1) * ATT_MERGED],
                       w_ref.at[:, dst:dst + ATT_MERGED], scale))
    return pieces


def _pooled_minus_token(pf, prev, t, c):
    groups = [g for g in range(len(POOL_WINDOWS))
              if g * POOL_GROUP_WIDTH < (c + 1) * LANES and (g + 1) * POOL_GROUP_WIDTH > c * LANES]
    acc = jnp.concatenate([prev, pf], axis=0)
    w, means = 1, []
    for g in groups:
        while w < POOL_WINDOWS[g]:
            acc = acc + pltpu.roll(acc, w, axis=0)
            w *= 2
        count = jnp.minimum(t + 1, POOL_WINDOWS[g]).astype(F32)
        means.append(acc[POOL_HALO:, :] / count)
    pooled = means[-1]
    for g, mean in zip(groups[-2::-1], means[-2::-1]):
        lane = lax.broadcasted_iota(jnp.int32, mean.shape, 1)
        pooled = jnp.where(lane < (g + 1) * POOL_GROUP_WIDTH - c * LANES, mean, pooled)
    return pooled - pf


def _in_proj_kernel(x_ref, g_ref, w_hbm, zn_ref, zg_ref, *rest):
    dil_refs, (slab_ref, w_ref, stage_ref, sem_ref) = rest[:-4], rest[-4:]

    @pl.when(pl.program_id(0) == 0)
    def _():
        _stream_cast(_w_in_pieces(w_hbm, w_ref), stage_ref, sem_ref)

    x = x_ref[...]
    u = (x * _rms_scale(x) * g_ref[...]).astype(BF16)
    lo = 0
    for out_ref, width in ((zn_ref, QKV_WIDTH), (zn_ref, POOL_WIDTH),
                           (zg_ref, D_MODEL), (zg_ref, D_MODEL)):
        off = lo - (0 if out_ref is zn_ref else ZN_WIDTH)
        acc = jnp.dot(u, w_ref[:, lo:lo + width], preferred_element_type=F32)
        out_ref[:, off:off + width] = acc.astype(BF16)
        lo += width
    n_slabs = QKV_WIDTH // LANES
    for gi, z_ref in enumerate(dil_refs):
        dilation = ATT_GROUPS[gi + 1][1]
        lo = NAT_WIDTH + gi * QKV_WIDTH
        acc = jnp.dot(u, w_ref[:, lo:lo + QKV_WIDTH], preferred_element_type=F32)
        for sl in range(n_slabs):
            slab_ref[sl] = acc[:, sl * LANES:(sl + 1) * LANES]
        for r in range(dilation):
            for sl in range(n_slabs):
                rows = slab_ref[sl, pl.ds(r, ROW_TILE // dilation, stride=dilation), :]
                z_ref[r, :, sl * LANES:(sl + 1) * LANES] = rows.astype(BF16)


def _in_proj(x2d, g, w_in):
    s = x2d.shape[0]
    dils = [d for _, d in ATT_GROUPS[1:]]
    out_shape = [jax.ShapeDtypeStruct((s, ZN_WIDTH), BF16), jax.ShapeDtypeStruct((s, ZG_WIDTH), BF16)]
    out_specs = [pl.BlockSpec((ROW_TILE, ZN_WIDTH), lambda i: (i, 0)),
                 pl.BlockSpec((ROW_TILE, ZG_WIDTH), lambda i: (i, 0))]
    for d in dils:
        out_shape.append(jax.ShapeDtypeStruct((d, s // d, QKV_WIDTH), BF16))
        out_specs.append(pl.BlockSpec((d, ROW_TILE // d, QKV_WIDTH), lambda i: (0, i, 0)))
    return pl.pallas_call(
        _in_proj_kernel,
        out_shape=out_shape,
        grid=(s // ROW_TILE,),
        in_specs=[
            pl.BlockSpec((ROW_TILE, D_MODEL), lambda i: (i, 0)),
            pl.BlockSpec((1, D_MODEL), lambda i: (0, 0)),
            HBM_SPEC,
        ],
        out_specs=out_specs,
        scratch_shapes=[pltpu.VMEM((QKV_WIDTH // LANES, ROW_TILE, LANES), F32),
                        pltpu.VMEM((D_MODEL, N_IN), BF16),
                        pltpu.VMEM((2, D_MODEL, ATT_MERGED), F32),
                        pltpu.SemaphoreType.DMA((2,))],
        compiler_params=pltpu.CompilerParams(
            dimension_semantics=("arbitrary",), vmem_limit_bytes=VMEM_LIMIT_BYTES),
        name="in_proj",
    )(x2d, g, w_in)


def _attend_block(qb, kk, vv, bias, head_masks_bf16, low_half):
    nh = HEADS_PER_GROUP
    q_stack = jnp.concatenate([qb * head_masks_bf16[h] for h in range(nh)], axis=0)
    s = lax.dot_general(q_stack, kk, (((1,), (1,)), ((), ())), preferred_element_type=F32)
    s = s + bias
    m = jnp.max(s, axis=-1, keepdims=True)
    p = jnp.exp2(s - m)
    l = jnp.sum(p, axis=-1, keepdims=True)
    pv = jnp.dot(p.astype(BF16), vv, preferred_element_type=F32)
    slabs = []
    for sl in range(ATT_MERGED // LANES):
        lanes = slice(sl * LANES, (sl + 1) * LANES)
        rows_a = slice(2 * sl * BLK, (2 * sl + 1) * BLK)
        rows_b = slice((2 * sl + 1) * BLK, (2 * sl + 2) * BLK)
        slabs.append((jnp.where(low_half, pv[rows_a, lanes], pv[rows_b, lanes]),
                      jnp.where(low_half, m[rows_a], m[rows_b]),
                      jnp.where(low_half, l[rows_a], l[rows_b])))
    return slabs


def _attn_kernel(*refs, slopes):
    z_refs = refs[0:N_GROUPS]
    kprev_refs = refs[N_GROUPS:3 * N_GROUPS:2]
    vprev_refs = refs[N_GROUPS + 1:3 * N_GROUPS:2]
    a_ref, bias_ref, bias0_ref, o_scr, m_scr, l_scr, nat_scr = refs[3 * N_GROUPS:]
    step = pl.program_id(0)
    nh = HEADS_PER_GROUP
    n_slabs = ATT_MERGED // LANES

    @pl.when(step == 0)
    def _():
        qi = lax.broadcasted_iota(jnp.int32, (BLK, 2 * BLK), 0)
        kj = lax.broadcasted_iota(jnp.int32, (BLK, 2 * BLK), 1)
        steps = BLK + qi - kj
        for g, (window, dilation) in enumerate(ATT_GROUPS):
            valid = (steps >= 0) & (steps <= window // dilation)
            dist = (steps * dilation).astype(F32)
            for h in range(nh):
                bias = jnp.where(valid, -(slopes[g * nh + h] * LOG2_E) * dist, MASKED_SCORE)
                bias_ref[g, h * BLK:(h + 1) * BLK, :] = bias
                bias0_ref[g, h * BLK:(h + 1) * BLK, :] = jnp.where(kj < BLK, MASKED_SCORE, bias)

    @pl.when(step == 1)
    def _():
        bias0_ref[...] = bias_ref[...]

    lane_head = lax.broadcasted_iota(jnp.int32, (BLK, ATT_MERGED), 1) // HEAD_DIM
    head_masks_bf16 = [(lane_head == h).astype(F32).astype(BF16) for h in range(nh)]
    low_half = lax.broadcasted_iota(jnp.int32, (BLK, LANES), 1) < HEAD_DIM
    attend = functools.partial(_attend_block, head_masks_bf16=head_masks_bf16, low_half=low_half)
    q_cols, k_cols, v_cols = (slice(c * ATT_MERGED, (c + 1) * ATT_MERGED) for c in range(3))

    def aligned(row):
        return row if isinstance(row, int) else pl.multiple_of(row, BLK)

    sub_rows = ATT_TILE // MERGE_MOD

    def store(g, dilation, r, b, slabs):
        if dilation in (1, MERGE_MOD):
            rows = pl.ds(aligned(r * (ATT_TILE // dilation) + b * BLK), BLK)
        else:
            per = dilation // MERGE_MOD
            rows = pl.ds((r % MERGE_MOD) * sub_rows + b * BLK * per + r // MERGE_MOD, BLK,
                         stride=per)
        for sl, (o, m, l) in enumerate(slabs):
            o_scr[g, sl, rows, :] = o
            m_scr[g, sl, rows, :] = m
            l_scr[g, sl, rows, :] = l

    for g, (_, dilation) in enumerate(ATT_GROUPS):
        n_blocks = ATT_TILE // dilation // BLK
        z_ref, kprev_ref, vprev_ref = z_refs[g], kprev_refs[g], vprev_refs[g]

        def load(r, row0, n_rows, cols, z_ref=z_ref, dilation=dilation):
            rows = pl.ds(aligned(row0), n_rows)
            return z_ref[rows, cols] if dilation == 1 else z_ref[r, rows, cols]

        def class_body(r, _, g=g, dilation=dilation, n_blocks=n_blocks, load=load,
                       kprev_ref=kprev_ref, vprev_ref=vprev_ref):
            kprev = kprev_ref[...] if dilation == 1 else kprev_ref[r]
            vprev = vprev_ref[...] if dilation == 1 else vprev_ref[r]
            kk = jnp.concatenate([kprev, load(r, 0, BLK, k_cols)], axis=0)
            vv = jnp.concatenate([vprev, load(r, 0, BLK, v_cols)], axis=0)
            store(g, dilation, r, 0, attend(load(r, 0, BLK, q_cols), kk, vv, bias0_ref[g]))
            for b in range(1, n_blocks):
                kk = load(r, (b - 1) * BLK, 2 * BLK, k_cols)
                vv = load(r, (b - 1) * BLK, 2 * BLK, v_cols)
                store(g, dilation, r, b, attend(load(r, b * BLK, BLK, q_cols), kk, vv, bias_ref[g]))
            return 0

        if dilation == 1:
            class_body(0, 0)
        else:
            lax.fori_loop(0, dilation, class_body, 0, unroll=max(1, ATT_UNROLL // n_blocks))

    chunks = sub_rows // MERGE_ROWS

    def merge_body(idx, _):
        c = idx // chunks
        row0 = (idx % chunks) * MERGE_ROWS
        nat_rows = pl.ds(row0 * MERGE_MOD + c, MERGE_ROWS, stride=MERGE_MOD)
        cls_rows = pl.ds(pl.multiple_of(c * sub_rows + row0, MERGE_ROWS), MERGE_ROWS)
        rows_of = lambda g: nat_rows if ATT_GROUPS[g][1] == 1 else cls_rows
        for sl in range(n_slabs):
            ms = [m_scr[g, sl, rows_of(g), :] for g in range(N_GROUPS)]
            mmax = functools.reduce(jnp.maximum, ms)
            ws = [jnp.exp2(m - mmax) for m in ms]
            num = sum(w * o_scr[g, sl, rows_of(g), :] for g, w in enumerate(ws))
            den = sum(w * l_scr[g, sl, rows_of(g), :] for g, w in enumerate(ws))
            nat_scr[sl, nat_rows, :] = num / den
        return 0

    lax.fori_loop(0, MERGE_MOD * chunks, merge_body, 0)
    for sl in range(n_slabs):
        a_ref[:, sl * LANES:(sl + 1) * LANES] = nat_scr[sl].astype(BF16)


def _attention(z_groups, slopes):
    s = z_groups[0].shape[0]
    cur_specs, prev_specs, prev_args = [], [], []
    for z, (_, d) in zip(z_groups, ATT_GROUPS):
        rows = ATT_TILE // d
        prev_row = lambda i, rows=rows: jnp.maximum(i * (rows // BLK) - 1, 0)
        for col in (1, 2):
            if d == 1:
                prev_specs.append(pl.BlockSpec((BLK, ATT_MERGED),
                                               lambda i, c=col, p=prev_row: (p(i), c)))
            else:
                prev_specs.append(pl.BlockSpec((d, BLK, ATT_MERGED),
                                               lambda i, c=col, p=prev_row: (0, p(i), c)))
            prev_args.append(z)
        if d == 1:
            cur_specs.append(pl.BlockSpec((rows, QKV_WIDTH), lambda i: (i, 0)))
        else:
            cur_specs.append(pl.BlockSpec((d, rows, QKV_WIDTH), lambda i: (0, i, 0)))
    scr_shape = (N_GROUPS, ATT_MERGED // LANES, ATT_TILE, LANES)
    bias_shape = (N_GROUPS, HEADS_PER_GROUP * BLK, 2 * BLK)
    return pl.pallas_call(
        functools.partial(_attn_kernel, slopes=slopes),
        out_shape=jax.ShapeDtypeStruct((s, ATT_MERGED), BF16),
        grid=(s // ATT_TILE,),
        in_specs=cur_specs + prev_specs,
        out_specs=pl.BlockSpec((ATT_TILE, ATT_MERGED), lambda i: (i, 0)),
        scratch_shapes=[pltpu.VMEM(bias_shape, F32), pltpu.VMEM(bias_shape, F32),
                        pltpu.VMEM(scr_shape, F32), pltpu.VMEM(scr_shape, F32),
                        pltpu.VMEM(scr_shape, F32), pltpu.VMEM(scr_shape[1:], F32)],
        compiler_params=pltpu.CompilerParams(
            dimension_semantics=("arbitrary",), vmem_limit_bytes=VMEM_LIMIT_BYTES),
        name="attn",
    )(*z_groups, *prev_args)


def _pool_weight_kernel(wgrp_ref, scale_ref, wpo_ref, wc_ref):
    c = POOL_GROUP_WIDTH
    for g in range(len(POOL_WINDOWS)):
        rows = slice(g * c, (g + 1) * c)
        wc = jnp.dot(wgrp_ref[g], scale_ref[rows, :] * wpo_ref[rows, :],
                     precision=lax.Precision.HIGHEST, preferred_element_type=F32)
        wc_ref[rows, :] = wc.astype(BF16)


def _pool_weight(w_grp, scale_col, wpo):
    return pl.pallas_call(
        _pool_weight_kernel,
        out_shape=jax.ShapeDtypeStruct((POOL_WIDTH, D_MODEL), BF16),
        compiler_params=pltpu.CompilerParams(vmem_limit_bytes=VMEM_LIMIT_BYTES),
        name="pool_weight",
    )(w_grp, scale_col, wpo)


def _post_kernel(a_ref, pz_ref, pzprev_ref, ga_ref, gp_ref, x_ref, wao_hbm, wc_ref, wout_hbm,
                 g_ref, w1_hbm, w2_hbm, gf_ref, o_ref,
                 wao_ref, wout_ref, w1_ref, w2_ref, stage_ref, sem_ref):
    i = pl.program_id(0)

    @pl.when(i == 0)
    def _():
        max_rows, cols = stage_ref.shape[1:]
        pieces = []
        for w, w_bf in ((wao_hbm, wao_ref), (wout_hbm, wout_ref), (w1_hbm, w1_ref), (w2_hbm, w2_ref)):
            rows = min(max_rows, w.shape[0])
            pieces += [(w.at[r:r + rows, c:c + cols], w_bf.at[r:r + rows, c:c + cols], None)
                       for r in range(0, w.shape[0], rows) for c in range(0, w.shape[1], cols)]
        _stream_cast(pieces, stage_ref, sem_ref)

    att = jnp.dot(a_ref[...], wao_ref[...], preferred_element_type=F32)
    t = i * ROW_TILE + lax.broadcasted_iota(jnp.int32, (ROW_TILE, 1), 0)
    pooled = []
    for c in range(POOL_WIDTH // LANES):
        lanes = slice(c * LANES, (c + 1) * LANES)
        prev = jnp.where(i > 0, pzprev_ref[:, lanes].astype(F32), 0.0)
        pooled.append(_pooled_minus_token(pz_ref[:, lanes].astype(F32), prev, t, c).astype(BF16))
    pool = jnp.dot(jnp.concatenate(pooled, axis=1), wc_ref[...], preferred_element_type=F32)
    merged = (jax.nn.sigmoid(ga_ref[...].astype(F32)) * att
              + jax.nn.sigmoid(gp_ref[...].astype(F32)) * pool)
    h = x_ref[...] + jnp.dot(merged.astype(BF16), wout_ref[...], preferred_element_type=F32)

    m = (h * _rms_scale(h) * g_ref[...]).astype(BF16)
    y = h
    for c in range(D_FF // MLP_CHUNK):
        cols = slice(c * MLP_CHUNK, (c + 1) * MLP_CHUNK)
        hid = jnp.dot(m, w1_ref[:, cols], preferred_element_type=F32)
        hid = jnp.square(jnp.maximum(hid, 0.0)).astype(BF16)
        y = y + jnp.dot(hid, w2_ref[cols, :], preferred_element_type=F32)
    o_ref[...] = y * _rms_scale(y) * gf_ref[...]


def _post(a, zn, zg, x2d, wao, wc, wout, g_mlp, w1, w2, g_final):
    s = x2d.shape[0]
    tm = ROW_TILE
    row = lambda i: (i, 0)
    const = lambda i: (0, 0)
    halo_blocks = tm // POOL_HALO
    assert QKV_WIDTH % POOL_WIDTH == 0
    pool_col = QKV_WIDTH // POOL_WIDTH
    return pl.pallas_call(
        _post_kernel,
        out_shape=jax.ShapeDtypeStruct((s, D_MODEL), F32),
        grid=(s // tm,),
        in_specs=[
            pl.BlockSpec((tm, ATT_MERGED), row),
            pl.BlockSpec((tm, POOL_WIDTH), lambda i: (i, pool_col)),
            pl.BlockSpec((POOL_HALO, POOL_WIDTH),
                         lambda i: (jnp.maximum(i * halo_blocks - 1, 0), pool_col)),
            pl.BlockSpec((tm, D_MODEL), lambda i: (i, 0)),
            pl.BlockSpec((tm, D_MODEL), lambda i: (i, 1)),
            pl.BlockSpec((tm, D_MODEL), row),
            HBM_SPEC,
            pl.BlockSpec((POOL_WIDTH, D_MODEL), const),
            HBM_SPEC,
            pl.BlockSpec((1, D_MODEL), const),
            HBM_SPEC,
            HBM_SPEC,
            pl.BlockSpec((1, D_MODEL), const),
        ],
        out_specs=pl.BlockSpec((tm, D_MODEL), row),
        scratch_shapes=[pltpu.VMEM((ATT_MERGED, D_MODEL), BF16),
                        pltpu.VMEM((D_MODEL, D_MODEL), BF16),
                        pltpu.VMEM((D_MODEL, D_FF), BF16),
                        pltpu.VMEM((D_FF, D_MODEL), BF16),
                        pltpu.VMEM((2, WEIGHT_STAGE_ROWS, D_MODEL), F32),
                        pltpu.SemaphoreType.DMA((2,))],
        compiler_params=pltpu.CompilerParams(
            dimension_semantics=("arbitrary",), vmem_limit_bytes=VMEM_LIMIT_BYTES),
        name="post",
    )(a, zn, zn, zg, zg, x2d, wao, wc, wout, g_mlp, w1, w2, g_final)


def _alibi_slopes():
    return tuple(2.0 ** (-ALIBI_MAX_BIAS * (h + 1.0) / N_ATT_HEADS) for h in range(N_ATT_HEADS))


def kernel(x, norm_mix_g, w_in, w_att_out, w_pool_grp, pool_scale, w_pool_out, w_out,
           norm_mlp_g, w_mlp_in, w_mlp_out, norm_final_g):
    batch, seq, d = x.shape
    assert d == D_MODEL and norm_mix_g.shape[0] == 1, "one layer of width D_MODEL"
    assert seq % ATT_TILE == 0 and seq % ROW_TILE == 0
    slopes = _alibi_slopes()
    outs = []
    for b in range(batch):
        h = x[b]
        zn, zg, z1, z2 = _in_proj(h, norm_mix_g[0][None, :], w_in[0])
        a = _attention((zn, z1, z2), slopes)
        wc = _pool_weight(w_pool_grp[0], pool_scale[0][:, None], w_pool_out[0])
        outs.append(_post(a, zn, zg, h, w_att_out[0], wc, w_out[0], norm_mlp_g[0][None, :],
                          w_mlp_in[0], w_mlp_out[0], norm_final_g[None, :]))
    return outs[0][None] if batch == 1 else jnp.stack(outs, axis=0)
```

```python
import functools

import jax
import jax.numpy as jnp
from jax import lax
from jax.experimental import pallas as pl
from jax.experimental.pallas import tpu as pltpu

D_MODEL = 1024
HEAD_DIM = 64
ATT_GROUPS = ((128, 1), (512, 4), (2048, 16))
N_GROUPS = len(ATT_GROUPS)
HEADS_PER_GROUP = 4
N_ATT_HEADS = HEADS_PER_GROUP * N_GROUPS
ATT_WIDTH = N_ATT_HEADS * HEAD_DIM
ATT_MERGED = HEADS_PER_GROUP * HEAD_DIM
QKV_WIDTH = 3 * ATT_MERGED
BLK = 128
POOL_WINDOWS = (2, 4, 8, 16)
POOL_GROUP_WIDTH = 3 * D_MODEL // 16
POOL_WIDTH = POOL_GROUP_WIDTH * len(POOL_WINDOWS)
D_FF = 4 * D_MODEL
N_IN = 3 * ATT_WIDTH + POOL_WIDTH + 2 * D_MODEL
NORM_EPS = 1e-6
ALIBI_MAX_BIAS = 8.0
POOL_HALO = max(POOL_WINDOWS)

VMEM_LIMIT_BYTES = 56 * 1024 * 1024
LANES = 128
MASKED_SCORE = -1e30

ROW_TILE = 512
MLP_CHUNK = 1024
WEIGHT_STAGE_ROWS = 512
WEIGHT_STAGE_SLOTS = 4
MAX_DILATION = max(d for _, d in ATT_GROUPS)
ATT_TILE = BLK * MAX_DILATION
MERGE_MOD = 4
MERGE_ROWS = 128
LOG2_E = 1.4426950408889634
ATT_UNROLL = 8

ZN_WIDTH = QKV_WIDTH + POOL_WIDTH
ZG_WIDTH = 2 * D_MODEL
NAT_WIDTH = ZN_WIDTH + ZG_WIDTH

BF16 = jnp.bfloat16
F32 = jnp.float32


def _rms_scale(x):
    return lax.rsqrt(jnp.mean(x * x, axis=-1, keepdims=True) + NORM_EPS)


class _WeightStream:
    def __init__(self, pieces, stage_ref, sem_ref):
        self.pieces, self.stage_ref, self.sem_ref = pieces, stage_ref, sem_ref
        self.n_slots = stage_ref.shape[0]
        self.started = self.converted = 0

    def _slot(self, k):
        return self.stage_ref.at[k % self.n_slots, pl.ds(0, self.pieces[k][0].shape[0])]

    def _copy(self, k):
        return pltpu.make_async_copy(self.pieces[k][0], self._slot(k),
                                     self.sem_ref.at[k % self.n_slots])

    def _start_ahead(self):
        while self.started < min(len(self.pieces), self.converted + self.n_slots):
            self._copy(self.started).start()
            self.started += 1

    def ensure(self, count):
        self._start_ahead()
        while self.converted < count:
            k = self.converted
            _, dst, scale = self.pieces[k]
            self._copy(k).wait()
            piece = self._slot(k)[...]
            dst[...] = (piece if scale is None else piece * scale).astype(BF16)
            self.converted += 1
            self._start_ahead()

    def finish(self):
        self.ensure(len(self.pieces))


HBM_SPEC = pl.BlockSpec(memory_space=pl.ANY)


def _w_in_pieces(w_hbm, w_ref):
    pieces = []
    for j in range(N_IN // ATT_MERGED):
        which, g = divmod(j, N_GROUPS)
        if which < 3:
            dst = which * ATT_MERGED + (0 if g == 0 else NAT_WIDTH + (g - 1) * QKV_WIDTH)
        else:
            dst = QKV_WIDTH + (j - 3 * N_GROUPS) * ATT_MERGED
        scale = HEAD_DIM ** -0.5 * LOG2_E if which == 0 else None
        pieces.append((dst, w_hbm.at[:, j * ATT_MERGED:(j + 1) * ATT_MERGED],
                       w_ref.at[:, dst:dst + ATT_MERGED], scale))
    return [piece[1:] for piece in sorted(pieces, key=lambda piece: piece[0])]


def _pooled_minus_token(pf, prev, t, c):
    groups = [g for g in range(len(POOL_WINDOWS))
              if g * POOL_GROUP_WIDTH < (c + 1) * LANES and (g + 1) * POOL_GROUP_WIDTH > c * LANES]
    acc = jnp.concatenate([prev, pf], axis=0)
    w, means = 1, []
    for g in groups:
        while w < POOL_WINDOWS[g]:
            acc = acc + pltpu.roll(acc, w, axis=0)
            w *= 2
        count = jnp.minimum(t + 1, POOL_WINDOWS[g]).astype(F32)
        means.append(acc[POOL_HALO:, :] / count)
    pooled = means[-1]
    for g, mean in zip(groups[-2::-1], means[-2::-1]):
        lane = lax.broadcasted_iota(jnp.int32, mean.shape, 1)
        pooled = jnp.where(lane < (g + 1) * POOL_GROUP_WIDTH - c * LANES, mean, pooled)
    return pooled - pf


def _in_proj_kernel(x_ref, g_ref, w_hbm, zn_ref, zg_ref, *rest):
    dil_refs, (slab_ref, w_ref, stage_ref, sem_ref) = rest[:-4], rest[-4:]

    def body(stream):
        def project(lo, width):
            if stream is not None:
                stream.ensure((lo + width) // ATT_MERGED)
            return jnp.dot(u, w_ref[:, lo:lo + width], preferred_element_type=F32)

        x = x_ref[...]
        u = (x * _rms_scale(x) * g_ref[...]).astype(BF16)
        lo = 0
        for out_ref, width in ((zn_ref, QKV_WIDTH), (zn_ref, POOL_WIDTH),
                               (zg_ref, D_MODEL), (zg_ref, D_MODEL)):
            off = lo - (0 if out_ref is zn_ref else ZN_WIDTH)
            out_ref[:, off:off + width] = project(lo, width).astype(BF16)
            lo += width
        n_slabs = QKV_WIDTH // LANES
        for gi, z_ref in enumerate(dil_refs):
            dilation = ATT_GROUPS[gi + 1][1]
            acc = project(NAT_WIDTH + gi * QKV_WIDTH, QKV_WIDTH)
            for sl in range(n_slabs):
                slab_ref[sl] = acc[:, sl * LANES:(sl + 1) * LANES]
            for r in range(dilation):
                for sl in range(n_slabs):
                    rows = slab_ref[sl, pl.ds(r, ROW_TILE // dilation, stride=dilation), :]
                    z_ref[r, :, sl * LANES:(sl + 1) * LANES] = rows.astype(BF16)

    first = pl.program_id(0) == 0
    pl.when(first)(lambda: body(_WeightStream(_w_in_pieces(w_hbm, w_ref), stage_ref, sem_ref)))
    pl.when(jnp.logical_not(first))(lambda: body(None))


def _in_proj(x2d, g, w_in):
    s = x2d.shape[0]
    dils = [d for _, d in ATT_GROUPS[1:]]
    out_shape = [jax.ShapeDtypeStruct((s, ZN_WIDTH), BF16), jax.ShapeDtypeStruct((s, ZG_WIDTH), BF16)]
    out_specs = [pl.BlockSpec((ROW_TILE, ZN_WIDTH), lambda i: (i, 0)),
                 pl.BlockSpec((ROW_TILE, ZG_WIDTH), lambda i: (i, 0))]
    for d in dils:
        out_shape.append(jax.ShapeDtypeStruct((d, s // d, QKV_WIDTH), BF16))
        out_specs.append(pl.BlockSpec((d, ROW_TILE // d, QKV_WIDTH), lambda i: (0, i, 0)))
    return pl.pallas_call(
        _in_proj_kernel,
        out_shape=out_shape,
        grid=(s // ROW_TILE,),
        in_specs=[
            pl.BlockSpec((ROW_TILE, D_MODEL), lambda i: (i, 0)),
            pl.BlockSpec((1, D_MODEL), lambda i: (0, 0)),
            HBM_SPEC,
        ],
        out_specs=out_specs,
        scratch_shapes=[pltpu.VMEM((QKV_WIDTH // LANES, ROW_TILE, LANES), F32),
                        pltpu.VMEM((D_MODEL, N_IN), BF16),
                        pltpu.VMEM((WEIGHT_STAGE_SLOTS, D_MODEL, ATT_MERGED), F32),
                        pltpu.SemaphoreType.DMA((WEIGHT_STAGE_SLOTS,))],
        compiler_params=pltpu.CompilerParams(
            dimension_semantics=("arbitrary",), vmem_limit_bytes=VMEM_LIMIT_BYTES),
        name="in_proj",
    )(x2d, g, w_in)


def _attend_block(qb, kk, vv, bias, head_masks_bf16, low_half):
    nh = HEADS_PER_GROUP
    q_stack = jnp.concatenate([qb * head_masks_bf16[h] for h in range(nh)], axis=0)
    s = lax.dot_general(q_stack, kk, (((1,), (1,)), ((), ())), preferred_element_type=F32)
    s = s + bias
    m = jnp.max(s, axis=-1, keepdims=True)
    p = jnp.exp2(s - m)
    l = jnp.sum(p, axis=-1, keepdims=True)
    pv = jnp.dot(p.astype(BF16), vv, preferred_element_type=F32)
    slabs = []
    for sl in range(ATT_MERGED // LANES):
        lanes = slice(sl * LANES, (sl + 1) * LANES)
        rows_a = slice(2 * sl * BLK, (2 * sl + 1) * BLK)
        rows_b = slice((2 * sl + 1) * BLK, (2 * sl + 2) * BLK)
        slabs.append((jnp.where(low_half, pv[rows_a, lanes], pv[rows_b, lanes]),
                      jnp.where(low_half, m[rows_a], m[rows_b]),
                      jnp.where(low_half, l[rows_a], l[rows_b])))
    return slabs


def _attn_kernel(*refs, slopes):
    z_refs = refs[0:N_GROUPS]
    kprev_refs = refs[N_GROUPS:3 * N_GROUPS:2]
    vprev_refs = refs[N_GROUPS + 1:3 * N_GROUPS:2]
    a_ref, bias_ref, bias0_ref, o_scr, m_scr, l_scr, nat_scr = refs[3 * N_GROUPS:]
    step = pl.program_id(0)
    nh = HEADS_PER_GROUP
    n_slabs = ATT_MERGED // LANES

    @pl.when(step == 0)
    def _():
        qi = lax.broadcasted_iota(jnp.int32, (BLK, 2 * BLK), 0)
        kj = lax.broadcasted_iota(jnp.int32, (BLK, 2 * BLK), 1)
        steps = BLK + qi - kj
        for g, (window, dilation) in enumerate(ATT_GROUPS):
            valid = (steps >= 0) & (steps <= window // dilation)
            dist = (steps * dilation).astype(F32)
            for h in range(nh):
                bias = jnp.where(valid, -(slopes[g * nh + h] * LOG2_E) * dist, MASKED_SCORE)
                bias_ref[g, h * BLK:(h + 1) * BLK, :] = bias
                bias0_ref[g, h * BLK:(h + 1) * BLK, :] = jnp.where(kj < BLK, MASKED_SCORE, bias)

    @pl.when(step == 1)
    def _():
        bias0_ref[...] = bias_ref[...]

    lane_head = lax.broadcasted_iota(jnp.int32, (BLK, ATT_MERGED), 1) // HEAD_DIM
    head_masks_bf16 = [(lane_head == h).astype(F32).astype(BF16) for h in range(nh)]
    low_half = lax.broadcasted_iota(jnp.int32, (BLK, LANES), 1) < HEAD_DIM
    attend = functools.partial(_attend_block, head_masks_bf16=head_masks_bf16, low_half=low_half)
    q_cols, k_cols, v_cols = (slice(c * ATT_MERGED, (c + 1) * ATT_MERGED) for c in range(3))

    def aligned(row):
        return row if isinstance(row, int) else pl.multiple_of(row, BLK)

    sub_rows = ATT_TILE // MERGE_MOD

    def store(g, dilation, r, b, slabs):
        if dilation in (1, MERGE_MOD):
            rows = pl.ds(aligned(r * (ATT_TILE // dilation) + b * BLK), BLK)
        else:
            per = dilation // MERGE_MOD
            rows = pl.ds((r % MERGE_MOD) * sub_rows + b * BLK * per + r // MERGE_MOD, BLK,
                         stride=per)
        for sl, (o, m, l) in enumerate(slabs):
            o_scr[g, sl, rows, :] = o
            m_scr[g, sl, rows, :] = m
            l_scr[g, sl, rows, :] = l

    for g, (_, dilation) in enumerate(ATT_GROUPS):
        n_blocks = ATT_TILE // dilation // BLK
        z_ref, kprev_ref, vprev_ref = z_refs[g], kprev_refs[g], vprev_refs[g]

        def load(r, row0, n_rows, cols, z_ref=z_ref, dilation=dilation):
            rows = pl.ds(aligned(row0), n_rows)
            return z_ref[rows, cols] if dilation == 1 else z_ref[r, rows, cols]

        def class_body(r, _, g=g, dilation=dilation, n_blocks=n_blocks, load=load,
                       kprev_ref=kprev_ref, vprev_ref=vprev_ref):
            kprev = kprev_ref[...] if dilation == 1 else kprev_ref[r]
            vprev = vprev_ref[...] if dilation == 1 else vprev_ref[r]
            kk = jnp.concatenate([kprev, load(r, 0, BLK, k_cols)], axis=0)
            vv = jnp.concatenate([vprev, load(r, 0, BLK, v_cols)], axis=0)
            store(g, dilation, r, 0, attend(load(r, 0, BLK, q_cols), kk, vv, bias0_ref[g]))
            for b in range(1, n_blocks):
                kk = load(r, (b - 1) * BLK, 2 * BLK, k_cols)
                vv = load(r, (b - 1) * BLK, 2 * BLK, v_cols)
                store(g, dilation, r, b, attend(load(r, b * BLK, BLK, q_cols), kk, vv, bias_ref[g]))
            return 0

        if dilation == 1:
            class_body(0, 0)
        else:
            lax.fori_loop(0, dilation, class_body, 0, unroll=max(1, ATT_UNROLL // n_blocks))

    chunks = sub_rows // MERGE_ROWS

    def merge_body(idx, _):
        c = idx // chunks
        row0 = (idx % chunks) * MERGE_ROWS
        nat_rows = pl.ds(row0 * MERGE_MOD + c, MERGE_ROWS, stride=MERGE_MOD)
        cls_rows = pl.ds(pl.multiple_of(c * sub_rows + row0, MERGE_ROWS), MERGE_ROWS)
        rows_of = lambda g: nat_rows if ATT_GROUPS[g][1] == 1 else cls_rows
        for sl in range(n_slabs):
            ms = [m_scr[g, sl, rows_of(g), :] for g in range(N_GROUPS)]
            mmax = functools.reduce(jnp.maximum, ms)
            ws = [jnp.exp2(m - mmax) for m in ms]
            num = sum(w * o_scr[g, sl, rows_of(g), :] for g, w in enumerate(ws))
            den = sum(w * l_scr[g, sl, rows_of(g), :] for g, w in enumerate(ws))
            nat_scr[sl, nat_rows, :] = num / den
        return 0

    lax.fori_loop(0, MERGE_MOD * chunks, merge_body, 0)
    for sl in range(n_slabs):
        a_ref[:, sl * LANES:(sl + 1) * LANES] = nat_scr[sl].astype(BF16)


def _attention(z_groups, slopes):
    s = z_groups[0].shape[0]
    cur_specs, prev_specs, prev_args = [], [], []
    for z, (_, d) in zip(z_groups, ATT_GROUPS):
        rows = ATT_TILE // d
        prev_row = lambda i, rows=rows: jnp.maximum(i * (rows // BLK) - 1, 0)
        for col in (1, 2):
            if d == 1:
                prev_specs.append(pl.BlockSpec((BLK, ATT_MERGED),
                                               lambda i, c=col, p=prev_row: (p(i), c)))
            else:
                prev_specs.append(pl.BlockSpec((d, BLK, ATT_MERGED),
                                               lambda i, c=col, p=prev_row: (0, p(i), c)))
            prev_args.append(z)
        if d == 1:
            cur_specs.append(pl.BlockSpec((rows, QKV_WIDTH), lambda i: (i, 0)))
        else:
            cur_specs.append(pl.BlockSpec((d, rows, QKV_WIDTH), lambda i: (0, i, 0)))
    scr_shape = (N_GROUPS, ATT_MERGED // LANES, ATT_TILE, LANES)
    bias_shape = (N_GROUPS, HEADS_PER_GROUP * BLK, 2 * BLK)
    return pl.pallas_call(
        functools.partial(_attn_kernel, slopes=slopes),
        out_shape=jax.ShapeDtypeStruct((s, ATT_MERGED), BF16),
        grid=(s // ATT_TILE,),
        in_specs=cur_specs + prev_specs,
        out_specs=pl.BlockSpec((ATT_TILE, ATT_MERGED), lambda i: (i, 0)),
        scratch_shapes=[pltpu.VMEM(bias_shape, F32), pltpu.VMEM(bias_shape, F32),
                        pltpu.VMEM(scr_shape, F32), pltpu.VMEM(scr_shape, F32),
                        pltpu.VMEM(scr_shape, F32), pltpu.VMEM(scr_shape[1:], F32)],
        compiler_params=pltpu.CompilerParams(
            dimension_semantics=("arbitrary",), vmem_limit_bytes=VMEM_LIMIT_BYTES),
        name="attn",
    )(*z_groups, *prev_args)


def _pool_weight_kernel(wgrp_ref, scale_ref, wpo_ref, wc_ref):
    c = POOL_GROUP_WIDTH
    for g in range(len(POOL_WINDOWS)):
        rows = slice(g * c, (g + 1) * c)
        wc = jnp.dot(wgrp_ref[g], scale_ref[rows, :] * wpo_ref[rows, :],
                     precision=lax.Precision.HIGHEST, preferred_element_type=F32)
        wc_ref[rows, :] = wc.astype(BF16)


def _pool_weight(w_grp, scale_col, wpo):
    return pl.pallas_call(
        _pool_weight_kernel,
        out_shape=jax.ShapeDtypeStruct((POOL_WIDTH, D_MODEL), BF16),
        compiler_params=pltpu.CompilerParams(vmem_limit_bytes=VMEM_LIMIT_BYTES),
        name="pool_weight",
    )(w_grp, scale_col, wpo)


def _post_kernel(a_ref, pz_ref, pzprev_ref, ga_ref, gp_ref, x_ref, wao_hbm, wc_ref, wout_hbm,
                 g_ref, w1_hbm, w2_hbm, gf_ref, o_ref,
                 wao_ref, wout_ref, w1_ref, w2_ref, stage_ref, sem_ref):
    i = pl.program_id(0)

    def weight_pieces():
        max_rows, width = stage_ref.shape[1:]

        def split(w, w_bf, rows, cols):
            step = min(max_rows, rows.stop - rows.start)
            return [(w.at[r:r + step, cols], w_bf.at[r:r + step, cols], None)
                    for r in range(rows.start, rows.stop, step)]

        assert D_MODEL == width and MLP_CHUNK % width == 0
        full = slice(0, D_MODEL)
        groups = [split(wao_hbm, wao_ref, slice(0, ATT_MERGED), full),
                  split(wout_hbm, wout_ref, full, full)]
        for c in range(D_FF // MLP_CHUNK):
            chunk = [slice(k, k + width) for k in range(c * MLP_CHUNK, (c + 1) * MLP_CHUNK, width)]
            groups.append([p for cols in chunk for p in split(w1_hbm, w1_ref, full, cols)])
            groups.append([p for rows in chunk for p in split(w2_hbm, w2_ref, rows, full)])
        return [p for group in groups for p in group], [len(group) for group in groups]

    def body(stream, uses):
        needed = [0]

        def dot(lhs, w_view):
            if stream is not None:
                needed[0] += uses.pop(0)
                stream.ensure(needed[0])
            return jnp.dot(lhs, w_view[...], preferred_element_type=F32)

        att = dot(a_ref[...], wao_ref)
        t = i * ROW_TILE + lax.broadcasted_iota(jnp.int32, (ROW_TILE, 1), 0)
        pooled = []
        for c in range(POOL_WIDTH // LANES):
            lanes = slice(c * LANES, (c + 1) * LANES)
            prev = jnp.where(i > 0, pzprev_ref[:, lanes].astype(F32), 0.0)
            pooled.append(
                _pooled_minus_token(pz_ref[:, lanes].astype(F32), prev, t, c).astype(BF16))
        pool = jnp.dot(jnp.concatenate(pooled, axis=1), wc_ref[...], preferred_element_type=F32)
        merged = (jax.nn.sigmoid(ga_ref[...].astype(F32)) * att
                  + jax.nn.sigmoid(gp_ref[...].astype(F32)) * pool)
        h = x_ref[...] + dot(merged.astype(BF16), wout_ref)

        m = (h * _rms_scale(h) * g_ref[...]).astype(BF16)
        y = h
        for c in range(D_FF // MLP_CHUNK):
            cols = slice(c * MLP_CHUNK, (c + 1) * MLP_CHUNK)
            hid = dot(m, w1_ref.at[:, cols])
            hid = jnp.square(jnp.maximum(hid, 0.0)).astype(BF16)
            y = y + dot(hid, w2_ref.at[cols, :])
        o_ref[...] = y * _rms_scale(y) * gf_ref[...]

    def first_step():
        pieces, uses = weight_pieces()
        body(_WeightStream(pieces, stage_ref, sem_ref), uses)

    pl.when(i == 0)(first_step)
    pl.when(i > 0)(lambda: body(None, None))


def _post(a, zn, zg, x2d, wao, wc, wout, g_mlp, w1, w2, g_final):
    s = x2d.shape[0]
    tm = ROW_TILE
    row = lambda i: (i, 0)
    const = lambda i: (0, 0)
    halo_blocks = tm // POOL_HALO
    assert QKV_WIDTH % POOL_WIDTH == 0
    pool_col = QKV_WIDTH // POOL_WIDTH
    return pl.pallas_call(
        _post_kernel,
        out_shape=jax.ShapeDtypeStruct((s, D_MODEL), F32),
        grid=(s // tm,),
        in_specs=[
            pl.BlockSpec((tm, ATT_MERGED), row),
            pl.BlockSpec((tm, POOL_WIDTH), lambda i: (i, pool_col)),
            pl.BlockSpec((POOL_HALO, POOL_WIDTH),
                         lambda i: (jnp.maximum(i * halo_blocks - 1, 0), pool_col)),
            pl.BlockSpec((tm, D_MODEL), lambda i: (i, 0)),
            pl.BlockSpec((tm, D_MODEL), lambda i: (i, 1)),
            pl.BlockSpec((tm, D_MODEL), row),
            HBM_SPEC,
            pl.BlockSpec((POOL_WIDTH, D_MODEL), const),
            HBM_SPEC,
            pl.BlockSpec((1, D_MODEL), const),
            HBM_SPEC,
            HBM_SPEC,
            pl.BlockSpec((1, D_MODEL), const),
        ],
        out_specs=pl.BlockSpec((tm, D_MODEL), row),
        scratch_shapes=[pltpu.VMEM((ATT_MERGED, D_MODEL), BF16),
                        pltpu.VMEM((D_MODEL, D_MODEL), BF16),
                        pltpu.VMEM((D_MODEL, D_FF), BF16),
                        pltpu.VMEM((D_FF, D_MODEL), BF16),
                        pltpu.VMEM((WEIGHT_STAGE_SLOTS, WEIGHT_STAGE_ROWS, D_MODEL), F32),
                        pltpu.SemaphoreType.DMA((WEIGHT_STAGE_SLOTS,))],
        compiler_params=pltpu.CompilerParams(
            dimension_semantics=("arbitrary",), vmem_limit_bytes=VMEM_LIMIT_BYTES),
        name="post",
    )(a, zn, zn, zg, zg, x2d, wao, wc, wout, g_mlp, w1, w2, g_final)


def _alibi_slopes():
    return tuple(2.0 ** (-ALIBI_MAX_BIAS * (h + 1.0) / N_ATT_HEADS) for h in range(N_ATT_HEADS))


def kernel(x, norm_mix_g, w_in, w_att_out, w_pool_grp, pool_scale, w_pool_out, w_out,
           norm_mlp_g, w_mlp_in, w_mlp_out, norm_final_g):
    batch, seq, d = x.shape
    assert d == D_MODEL and norm_mix_g.shape[0] == 1, "one layer of width D_MODEL"
    assert seq % ATT_TILE == 0 and seq % ROW_TILE == 0
    slopes = _alibi_slopes()
    outs = []
    for b in range(batch):
        h = x[b]
        zn, zg, z1, z2 = _in_proj(h, norm_mix_g[0][None, :], w_in[0])
        a = _attention((zn, z1, z2), slopes)
        wc = _pool_weight(w_pool_grp[0], pool_scale[0][:, None], w_pool_out[0])
        outs.append(_post(a, zn, zg, h, w_att_out[0], wc, w_out[0], norm_mlp_g[0][None, :],
                          w_mlp_in[0], w_mlp_out[0], norm_final_g[None, :]))
    return outs[0][None] if batch == 1 else jnp.stack(outs, axis=0)
```

```python
import functools

import jax
import jax.numpy as jnp
from jax import lax
from jax.experimental import pallas as pl
from jax.experimental.pallas import tpu as pltpu

D_MODEL = 1024
HEAD_DIM = 64
ATT_GROUPS = ((128, 1), (512, 4), (2048, 16))
N_GROUPS = len(ATT_GROUPS)
HEADS_PER_GROUP = 4
N_ATT_HEADS = HEADS_PER_GROUP * N_GROUPS
ATT_WIDTH = N_ATT_HEADS * HEAD_DIM
ATT_MERGED = HEADS_PER_GROUP * HEAD_DIM
QKV_WIDTH = 3 * ATT_MERGED
BLK = 128
POOL_WINDOWS = (2, 4, 8, 16)
POOL_GROUP_WIDTH = 3 * D_MODEL // 16
POOL_WIDTH = POOL_GROUP_WIDTH * len(POOL_WINDOWS)
D_FF = 4 * D_MODEL
N_IN = 3 * ATT_WIDTH + POOL_WIDTH + 2 * D_MODEL
NORM_EPS = 1e-6
ALIBI_MAX_BIAS = 8.0
POOL_HALO = max(POOL_WINDOWS)

VMEM_LIMIT_BYTES = 56 * 1024 * 1024
LANES = 128
MASKED_SCORE = -1e30

IN_PROJ_ROWS = 1024
ROW_TILE = 512
MLP_CHUNK = 1024
WEIGHT_STAGE_ROWS = 512
WEIGHT_STAGE_SLOTS = 4
MAX_DILATION = max(d for _, d in ATT_GROUPS)
ATT_TILE = BLK * MAX_DILATION
MERGE_MOD = 4
MERGE_ROWS = 128
LOG2_E = 1.4426950408889634
ATT_UNROLL = 8

ZN_WIDTH = QKV_WIDTH + POOL_WIDTH
ZG_WIDTH = 2 * D_MODEL
NAT_WIDTH = ZN_WIDTH + ZG_WIDTH

BF16 = jnp.bfloat16
F32 = jnp.float32


def _rms_scale(x):
    return lax.rsqrt(jnp.mean(x * x, axis=-1, keepdims=True) + NORM_EPS)


class _WeightStream:
    def __init__(self, pieces, stage_ref, sem_ref):
        self.pieces, self.stage_ref, self.sem_ref = pieces, stage_ref, sem_ref
        self.n_slots = stage_ref.shape[0]
        self.started = self.converted = 0

    def _slot(self, k):
        return self.stage_ref.at[k % self.n_slots, pl.ds(0, self.pieces[k][0].shape[0])]

    def _copy(self, k):
        return pltpu.make_async_copy(self.pieces[k][0], self._slot(k),
                                     self.sem_ref.at[k % self.n_slots])

    def _start_ahead(self):
        while self.started < min(len(self.pieces), self.converted + self.n_slots):
            self._copy(self.started).start()
            self.started += 1

    def ensure(self, count):
        self._start_ahead()
        while self.converted < count:
            k = self.converted
            _, dst, scale = self.pieces[k]
            self._copy(k).wait()
            piece = self._slot(k)[...]
            dst[...] = (piece if scale is None else piece * scale).astype(BF16)
            self.converted += 1
            self._start_ahead()

    def finish(self):
        self.ensure(len(self.pieces))


HBM_SPEC = pl.BlockSpec(memory_space=pl.ANY)


def _w_in_pieces(w_hbm, w_ref):
    pieces = []
    for j in range(N_IN // ATT_MERGED):
        which, g = divmod(j, N_GROUPS)
        if which < 3:
            dst = which * ATT_MERGED + (0 if g == 0 else NAT_WIDTH + (g - 1) * QKV_WIDTH)
        else:
            dst = QKV_WIDTH + (j - 3 * N_GROUPS) * ATT_MERGED
        scale = HEAD_DIM ** -0.5 * LOG2_E if which == 0 else None
        pieces.append((dst, w_hbm.at[:, j * ATT_MERGED:(j + 1) * ATT_MERGED],
                       w_ref.at[:, dst:dst + ATT_MERGED], scale))
    return [piece[1:] for piece in sorted(pieces, key=lambda piece: piece[0])]


def _pooled_minus_token(pf, prev, t, c):
    groups = [g for g in range(len(POOL_WINDOWS))
              if g * POOL_GROUP_WIDTH < (c + 1) * LANES and (g + 1) * POOL_GROUP_WIDTH > c * LANES]
    acc = jnp.concatenate([prev, pf], axis=0)
    w, means = 1, []
    for g in groups:
        while w < POOL_WINDOWS[g]:
            acc = acc + pltpu.roll(acc, w, axis=0)
            w *= 2
        count = jnp.minimum(t + 1, POOL_WINDOWS[g]).astype(F32)
        means.append(acc[POOL_HALO:, :] / count)
    pooled = means[-1]
    for g, mean in zip(groups[-2::-1], means[-2::-1]):
        lane = lax.broadcasted_iota(jnp.int32, mean.shape, 1)
        pooled = jnp.where(lane < (g + 1) * POOL_GROUP_WIDTH - c * LANES, mean, pooled)
    return pooled - pf


def _in_proj_kernel(x_ref, g_ref, w_hbm, zn_ref, zg_ref, *rest):
    dil_refs, (slab_ref, w_ref, stage_ref, sem_ref) = rest[:-4], rest[-4:]

    def body(stream):
        def project(lo, width):
            if stream is not None:
                stream.ensure((lo + width) // ATT_MERGED)
            return jnp.dot(u, w_ref[:, lo:lo + width], preferred_element_type=F32)

        x = x_ref[...]
        u = (x * _rms_scale(x) * g_ref[...]).astype(BF16)
        lo = 0
        for out_ref, width in ((zn_ref, QKV_WIDTH), (zn_ref, POOL_WIDTH),
                               (zg_ref, D_MODEL), (zg_ref, D_MODEL)):
            off = lo - (0 if out_ref is zn_ref else ZN_WIDTH)
            out_ref[:, off:off + width] = project(lo, width).astype(BF16)
            lo += width
        n_slabs = QKV_WIDTH // LANES
        for gi, z_ref in enumerate(dil_refs):
            dilation = ATT_GROUPS[gi + 1][1]
            acc = project(NAT_WIDTH + gi * QKV_WIDTH, QKV_WIDTH)
            for sl in range(n_slabs):
                slab_ref[sl] = acc[:, sl * LANES:(sl + 1) * LANES]
            for r in range(dilation):
                for sl in range(n_slabs):
                    rows = slab_ref[sl, pl.ds(r, IN_PROJ_ROWS // dilation, stride=dilation), :]
                    z_ref[r, :, sl * LANES:(sl + 1) * LANES] = rows.astype(BF16)

    first = pl.program_id(0) == 0
    pl.when(first)(lambda: body(_WeightStream(_w_in_pieces(w_hbm, w_ref), stage_ref, sem_ref)))
    pl.when(jnp.logical_not(first))(lambda: body(None))


def _in_proj(x2d, g, w_in):
    s = x2d.shape[0]
    dils = [d for _, d in ATT_GROUPS[1:]]
    out_shape = [jax.ShapeDtypeStruct((s, ZN_WIDTH), BF16), jax.ShapeDtypeStruct((s, ZG_WIDTH), BF16)]
    tm = IN_PROJ_ROWS
    out_specs = [pl.BlockSpec((tm, ZN_WIDTH), lambda i: (i, 0)),
                 pl.BlockSpec((tm, ZG_WIDTH), lambda i: (i, 0))]
    for d in dils:
        out_shape.append(jax.ShapeDtypeStruct((d, s // d, QKV_WIDTH), BF16))
        out_specs.append(pl.BlockSpec((d, tm // d, QKV_WIDTH), lambda i: (0, i, 0)))
    return pl.pallas_call(
        _in_proj_kernel,
        out_shape=out_shape,
        grid=(s // tm,),
        in_specs=[
            pl.BlockSpec((tm, D_MODEL), lambda i: (i, 0)),
            pl.BlockSpec((1, D_MODEL), lambda i: (0, 0)),
            HBM_SPEC,
        ],
        out_specs=out_specs,
        scratch_shapes=[pltpu.VMEM((QKV_WIDTH // LANES, tm, LANES), F32),
                        pltpu.VMEM((D_MODEL, N_IN), BF16),
                        pltpu.VMEM((WEIGHT_STAGE_SLOTS, D_MODEL, ATT_MERGED), F32),
                        pltpu.SemaphoreType.DMA((WEIGHT_STAGE_SLOTS,))],
        compiler_params=pltpu.CompilerParams(
            dimension_semantics=("arbitrary",), vmem_limit_bytes=VMEM_LIMIT_BYTES),
        name="in_proj",
    )(x2d, g, w_in)


def _attend_block(qb, kk, vv, bias, head_masks_bf16, low_half):
    nh = HEADS_PER_GROUP
    q_stack = jnp.concatenate([qb * head_masks_bf16[h] for h in range(nh)], axis=0)
    s = lax.dot_general(q_stack, kk, (((1,), (1,)), ((), ())), preferred_element_type=F32)
    s = s + bias
    m = jnp.max(s, axis=-1, keepdims=True)
    p = jnp.exp2(s - m)
    l = jnp.sum(p, axis=-1, keepdims=True)
    pv = jnp.dot(p.astype(BF16), vv, preferred_element_type=F32)
    slabs = []
    for sl in range(ATT_MERGED // LANES):
        lanes = slice(sl * LANES, (sl + 1) * LANES)
        rows_a = slice(2 * sl * BLK, (2 * sl + 1) * BLK)
        rows_b = slice((2 * sl + 1) * BLK, (2 * sl + 2) * BLK)
        slabs.append((jnp.where(low_half, pv[rows_a, lanes], pv[rows_b, lanes]),
                      jnp.where(low_half, m[rows_a], m[rows_b]),
                      jnp.where(low_half, l[rows_a], l[rows_b])))
    return slabs


def _attn_kernel(*refs, slopes):
    z_refs = refs[0:N_GROUPS]
    kprev_refs = refs[N_GROUPS:3 * N_GROUPS:2]
    vprev_refs = refs[N_GROUPS + 1:3 * N_GROUPS:2]
    a_ref, bias_ref, bias0_ref, o_scr, m_scr, l_scr, nat_scr = refs[3 * N_GROUPS:]
    step = pl.program_id(0)
    nh = HEADS_PER_GROUP
    n_slabs = ATT_MERGED // LANES

    @pl.when(step == 0)
    def _():
        qi = lax.broadcasted_iota(jnp.int32, (BLK, 2 * BLK), 0)
        kj = lax.broadcasted_iota(jnp.int32, (BLK, 2 * BLK), 1)
        steps = BLK + qi - kj
        for g, (window, dilation) in enumerate(ATT_GROUPS):
            valid = (steps >= 0) & (steps <= window // dilation)
            dist = (steps * dilation).astype(F32)
            for h in range(nh):
                bias = jnp.where(valid, -(slopes[g * nh + h] * LOG2_E) * dist, MASKED_SCORE)
                bias_ref[g, h * BLK:(h + 1) * BLK, :] = bias
                bias0_ref[g, h * BLK:(h + 1) * BLK, :] = jnp.where(kj < BLK, MASKED_SCORE, bias)

    @pl.when(step == 1)
    def _():
        bias0_ref[...] = bias_ref[...]

    lane_head = lax.broadcasted_iota(jnp.int32, (BLK, ATT_MERGED), 1) // HEAD_DIM
    head_masks_bf16 = [(lane_head == h).astype(F32).astype(BF16) for h in range(nh)]
    low_half = lax.broadcasted_iota(jnp.int32, (BLK, LANES), 1) < HEAD_DIM
    attend = functools.partial(_attend_block, head_masks_bf16=head_masks_bf16, low_half=low_half)
    q_cols, k_cols, v_cols = (slice(c * ATT_MERGED, (c + 1) * ATT_MERGED) for c in range(3))

    def aligned(row):
        return row if isinstance(row, int) else pl.multiple_of(row, BLK)

    sub_rows = ATT_TILE // MERGE_MOD

    def store(g, dilation, r, b, slabs):
        if dilation in (1, MERGE_MOD):
            rows = pl.ds(aligned(r * (ATT_TILE // dilation) + b * BLK), BLK)
        else:
            per = dilation // MERGE_MOD
            rows = pl.ds((r % MERGE_MOD) * sub_rows + b * BLK * per + r // MERGE_MOD, BLK,
                         stride=per)
        for sl, (o, m, l) in enumerate(slabs):
            o_scr[g, sl, rows, :] = o
            m_scr[g, sl, rows, :] = m
            l_scr[g, sl, rows, :] = l

    for g, (_, dilation) in enumerate(ATT_GROUPS):
        n_blocks = ATT_TILE // dilation // BLK
        z_ref, kprev_ref, vprev_ref = z_refs[g], kprev_refs[g], vprev_refs[g]

        def load(r, row0, n_rows, cols, z_ref=z_ref, dilation=dilation):
            rows = pl.ds(aligned(row0), n_rows)
            return z_ref[rows, cols] if dilation == 1 else z_ref[r, rows, cols]

        def class_body(r, _, g=g, dilation=dilation, n_blocks=n_blocks, load=load,
                       kprev_ref=kprev_ref, vprev_ref=vprev_ref):
            kprev = kprev_ref[...] if dilation == 1 else kprev_ref[r]
            vprev = vprev_ref[...] if dilation == 1 else vprev_ref[r]
            kk = jnp.concatenate([kprev, load(r, 0, BLK, k_cols)], axis=0)
            vv = jnp.concatenate([vprev, load(r, 0, BLK, v_cols)], axis=0)
            store(g, dilation, r, 0, attend(load(r, 0, BLK, q_cols), kk, vv, bias0_ref[g]))
            for b in range(1, n_blocks):
                kk = load(r, (b - 1) * BLK, 2 * BLK, k_cols)
                vv = load(r, (b - 1) * BLK, 2 * BLK, v_cols)
                store(g, dilation, r, b, attend(load(r, b * BLK, BLK, q_cols), kk, vv, bias_ref[g]))
            return 0

        if dilation == 1:
            class_body(0, 0)
        else:
            lax.fori_loop(0, dilation, class_body, 0, unroll=max(1, ATT_UNROLL // n_blocks))

    chunks = sub_rows // MERGE_ROWS

    def merge_body(idx, _):
        c = idx // chunks
        row0 = (idx % chunks) * MERGE_ROWS
        nat_rows = pl.ds(row0 * MERGE_MOD + c, MERGE_ROWS, stride=MERGE_MOD)
        cls_rows = pl.ds(pl.multiple_of(c * sub_rows + row0, MERGE_ROWS), MERGE_ROWS)
        rows_of = lambda g: nat_rows if ATT_GROUPS[g][1] == 1 else cls_rows
        for sl in range(n_slabs):
            ms = [m_scr[g, sl, rows_of(g), :] for g in range(N_GROUPS)]
            mmax = functools.reduce(jnp.maximum, ms)
            ws = [jnp.exp2(m - mmax) for m in ms]
            num = sum(w * o_scr[g, sl, rows_of(g), :] for g, w in enumerate(ws))
            den = sum(w * l_scr[g, sl, rows_of(g), :] for g, w in enumerate(ws))
            nat_scr[sl, nat_rows, :] = num / den
        return 0

    lax.fori_loop(0, MERGE_MOD * chunks, merge_body, 0)
    for sl in range(n_slabs):
        a_ref[:, sl * LANES:(sl + 1) * LANES] = nat_scr[sl].astype(BF16)


def _attention(z_groups, slopes):
    s = z_groups[0].shape[0]
    cur_specs, prev_specs, prev_args = [], [], []
    for z, (_, d) in zip(z_groups, ATT_GROUPS):
        rows = ATT_TILE // d
        prev_row = lambda i, rows=rows: jnp.maximum(i * (rows // BLK) - 1, 0)
        for col in (1, 2):
            if d == 1:
                prev_specs.append(pl.BlockSpec((BLK, ATT_MERGED),
                                               lambda i, c=col, p=prev_row: (p(i), c)))
            else:
                prev_specs.append(pl.BlockSpec((d, BLK, ATT_MERGED),
                                               lambda i, c=col, p=prev_row: (0, p(i), c)))
            prev_args.append(z)
        if d == 1:
            cur_specs.append(pl.BlockSpec((rows, QKV_WIDTH), lambda i: (i, 0)))
        else:
            cur_specs.append(pl.BlockSpec((d, rows, QKV_WIDTH), lambda i: (0, i, 0)))
    scr_shape = (N_GROUPS, ATT_MERGED // LANES, ATT_TILE, LANES)
    bias_shape = (N_GROUPS, HEADS_PER_GROUP * BLK, 2 * BLK)
    return pl.pallas_call(
        functools.partial(_attn_kernel, slopes=slopes),
        out_shape=jax.ShapeDtypeStruct((s, ATT_MERGED), BF16),
        grid=(s // ATT_TILE,),
        in_specs=cur_specs + prev_specs,
        out_specs=pl.BlockSpec((ATT_TILE, ATT_MERGED), lambda i: (i, 0)),
        scratch_shapes=[pltpu.VMEM(bias_shape, F32), pltpu.VMEM(bias_shape, F32),
                        pltpu.VMEM(scr_shape, F32), pltpu.VMEM(scr_shape, F32),
                        pltpu.VMEM(scr_shape, F32), pltpu.VMEM(scr_shape[1:], F32)],
        compiler_params=pltpu.CompilerParams(
            dimension_semantics=("arbitrary",), vmem_limit_bytes=VMEM_LIMIT_BYTES),
        name="attn",
    )(*z_groups, *prev_args)


def _pool_weight_kernel(wgrp_ref, scale_ref, wpo_ref, wc_ref):
    c = POOL_GROUP_WIDTH
    for g in range(len(POOL_WINDOWS)):
        rows = slice(g * c, (g + 1) * c)
        wc = jnp.dot(wgrp_ref[g], scale_ref[rows, :] * wpo_ref[rows, :],
                     precision=lax.Precision.HIGHEST, preferred_element_type=F32)
        wc_ref[rows, :] = wc.astype(BF16)


def _pool_weight(w_grp, scale_col, wpo):
    return pl.pallas_call(
        _pool_weight_kernel,
        out_shape=jax.ShapeDtypeStruct((POOL_WIDTH, D_MODEL), BF16),
        compiler_params=pltpu.CompilerParams(vmem_limit_bytes=VMEM_LIMIT_BYTES),
        name="pool_weight",
    )(w_grp, scale_col, wpo)


def _post_kernel(a_ref, pz_ref, pzprev_ref, ga_ref, gp_ref, x_ref, wao_hbm, wc_ref, wout_hbm,
                 g_ref, w1_hbm, w2_hbm, gf_ref, o_ref,
                 wao_ref, wout_ref, w1_ref, w2_ref, stage_ref, sem_ref):
    i = pl.program_id(0)

    def weight_pieces():
        max_rows, width = stage_ref.shape[1:]

        def split(w, w_bf, rows, cols):
            step = min(max_rows, rows.stop - rows.start)
            return [(w.at[r:r + step, cols], w_bf.at[r:r + step, cols], None)
                    for r in range(rows.start, rows.stop, step)]

        assert D_MODEL == width and MLP_CHUNK % width == 0
        full = slice(0, D_MODEL)
        groups = [split(wao_hbm, wao_ref, slice(0, ATT_MERGED), full),
                  split(wout_hbm, wout_ref, full, full)]
        for c in range(D_FF // MLP_CHUNK):
            chunk = [slice(k, k + width) for k in range(c * MLP_CHUNK, (c + 1) * MLP_CHUNK, width)]
            groups.append([p for cols in chunk for p in split(w1_hbm, w1_ref, full, cols)])
            groups.append([p for rows in chunk for p in split(w2_hbm, w2_ref, rows, full)])
        return [p for group in groups for p in group], [len(group) for group in groups]

    def body(stream, uses):
        needed = [0]

        def dot(lhs, w_view):
            if stream is not None:
                needed[0] += uses.pop(0)
                stream.ensure(needed[0])
            return jnp.dot(lhs, w_view[...], preferred_element_type=F32)

        att = dot(a_ref[...], wao_ref)
        t = i * ROW_TILE + lax.broadcasted_iota(jnp.int32, (ROW_TILE, 1), 0)
        pooled = []
        for c in range(POOL_WIDTH // LANES):
            lanes = slice(c * LANES, (c + 1) * LANES)
            prev = jnp.where(i > 0, pzprev_ref[:, lanes].astype(F32), 0.0)
            pooled.append(
                _pooled_minus_token(pz_ref[:, lanes].astype(F32), prev, t, c).astype(BF16))
        pool = jnp.dot(jnp.concatenate(pooled, axis=1), wc_ref[...], preferred_element_type=F32)
        merged = (jax.nn.sigmoid(ga_ref[...].astype(F32)) * att
                  + jax.nn.sigmoid(gp_ref[...].astype(F32)) * pool)
        h = x_ref[...] + dot(merged.astype(BF16), wout_ref)

        m = (h * _rms_scale(h) * g_ref[...]).astype(BF16)
        y = h
        for c in range(D_FF // MLP_CHUNK):
            cols = slice(c * MLP_CHUNK, (c + 1) * MLP_CHUNK)
            hid = dot(m, w1_ref.at[:, cols])
            hid = jnp.square(jnp.maximum(hid, 0.0)).astype(BF16)
            y = y + dot(hid, w2_ref.at[cols, :])
        o_ref[...] = y * _rms_scale(y) * gf_ref[...]

    def first_step():
        pieces, uses = weight_pieces()
        body(_WeightStream(pieces, stage_ref, sem_ref), uses)

    pl.when(i == 0)(first_step)
    pl.when(i > 0)(lambda: body(None, None))


def _post(a, zn, zg, x2d, wao, wc, wout, g_mlp, w1, w2, g_final):
    s = x2d.shape[0]
    tm = ROW_TILE
    row = lambda i: (i, 0)
    const = lambda i: (0, 0)
    halo_blocks = tm // POOL_HALO
    assert QKV_WIDTH % POOL_WIDTH == 0
    pool_col = QKV_WIDTH // POOL_WIDTH
    return pl.pallas_call(
        _post_kernel,
        out_shape=jax.ShapeDtypeStruct((s, D_MODEL), F32),
        grid=(s // tm,),
        in_specs=[
            pl.BlockSpec((tm, ATT_MERGED), row),
            pl.BlockSpec((tm, POOL_WIDTH), lambda i: (i, pool_col)),
            pl.BlockSpec((POOL_HALO, POOL_WIDTH),
                         lambda i: (jnp.maximum(i * halo_blocks - 1, 0), pool_col)),
            pl.BlockSpec((tm, D_MODEL), lambda i: (i, 0)),
            pl.BlockSpec((tm, D_MODEL), lambda i: (i, 1)),
            pl.BlockSpec((tm, D_MODEL), row),
            HBM_SPEC,
            pl.BlockSpec((POOL_WIDTH, D_MODEL), const),
            HBM_SPEC,
            pl.BlockSpec((1, D_MODEL), const),
            HBM_SPEC,
            HBM_SPEC,
            pl.BlockSpec((1, D_MODEL), const),
        ],
        out_specs=pl.BlockSpec((tm, D_MODEL), row),
        scratch_shapes=[pltpu.VMEM((ATT_MERGED, D_MODEL), BF16),
                        pltpu.VMEM((D_MODEL, D_MODEL), BF16),
                        pltpu.VMEM((D_MODEL, D_FF), BF16),
                        pltpu.VMEM((D_FF, D_MODEL), BF16),
                        pltpu.VMEM((WEIGHT_STAGE_SLOTS, WEIGHT_STAGE_ROWS, D_MODEL), F32),
                        pltpu.SemaphoreType.DMA((WEIGHT_STAGE_SLOTS,))],
        compiler_params=pltpu.CompilerParams(
            dimension_semantics=("arbitrary",), vmem_limit_bytes=VMEM_LIMIT_BYTES),
        name="post",
    )(a, zn, zn, zg, zg, x2d, wao, wc, wout, g_mlp, w1, w2, g_final)


def _alibi_slopes():
    return tuple(2.0 ** (-ALIBI_MAX_BIAS * (h + 1.0) / N_ATT_HEADS) for h in range(N_ATT_HEADS))


def kernel(x, norm_mix_g, w_in, w_att_out, w_pool_grp, pool_scale, w_pool_out, w_out,
           norm_mlp_g, w_mlp_in, w_mlp_out, norm_final_g):
    batch, seq, d = x.shape
    assert d == D_MODEL and norm_mix_g.shape[0] == 1, "one layer of width D_MODEL"
    assert seq % ATT_TILE == 0 and seq % ROW_TILE == 0 and seq % IN_PROJ_ROWS == 0
    slopes = _alibi_slopes()
    outs = []
    for b in range(batch):
        h = x[b]
        zn, zg, z1, z2 = _in_proj(h, norm_mix_g[0][None, :], w_in[0])
        a = _attention((zn, z1, z2), slopes)
        wc = _pool_weight(w_pool_grp[0], pool_scale[0][:, None], w_pool_out[0])
        outs.append(_post(a, zn, zg, h, w_att_out[0], wc, w_out[0], norm_mlp_g[0][None, :],
                          w_mlp_in[0], w_mlp_out[0], norm_final_g[None, :]))
    return outs[0][None] if batch == 1 else jnp.stack(outs, axis=0)
```

```python
import functools

import jax
import jax.numpy as jnp
from jax import lax
from jax.experimental import pallas as pl
from jax.experimental.pallas import tpu as pltpu

D_MODEL = 1024
HEAD_DIM = 64
ATT_GROUPS = ((128, 1), (512, 4), (2048, 16))
N_GROUPS = len(ATT_GROUPS)
HEADS_PER_GROUP = 4
N_ATT_HEADS = HEADS_PER_GROUP * N_GROUPS
ATT_WIDTH = N_ATT_HEADS * HEAD_DIM
ATT_MERGED = HEADS_PER_GROUP * HEAD_DIM
QKV_WIDTH = 3 * ATT_MERGED
BLK = 128
POOL_WINDOWS = (2, 4, 8, 16)
POOL_GROUP_WIDTH = 3 * D_MODEL // 16
POOL_WIDTH = POOL_GROUP_WIDTH * len(POOL_WINDOWS)
D_FF = 4 * D_MODEL
N_IN = 3 * ATT_WIDTH + POOL_WIDTH + 2 * D_MODEL
NORM_EPS = 1e-6
ALIBI_MAX_BIAS = 8.0
POOL_HALO = max(POOL_WINDOWS)

VMEM_LIMIT_BYTES = 56 * 1024 * 1024
LANES = 128
MASKED_SCORE = -1e30

IN_PROJ_ROWS = 512
ROW_TILE = 512
MLP_CHUNK = 1024
WEIGHT_STAGE_ROWS = 512
WEIGHT_STAGE_SLOTS = 4
MAX_DILATION = max(d for _, d in ATT_GROUPS)
ATT_TILE = BLK * MAX_DILATION
STRIDE_STEP = 4
MERGE_MOD = 4
MERGE_ROWS = 128
LOG2_E = 1.4426950408889634
ATT_UNROLL = 8

ZN_WIDTH = QKV_WIDTH + POOL_WIDTH
ZG_WIDTH = 2 * D_MODEL
NAT_WIDTH = ZN_WIDTH + ZG_WIDTH

BF16 = jnp.bfloat16
F32 = jnp.float32


def _rms_scale(x):
    return lax.rsqrt(jnp.mean(x * x, axis=-1, keepdims=True) + NORM_EPS)


class _WeightStream:
    def __init__(self, pieces, stage_ref, sem_ref):
        self.pieces, self.stage_ref, self.sem_ref = pieces, stage_ref, sem_ref
        self.n_slots = stage_ref.shape[0]
        self.started = self.converted = 0

    def _slot(self, k):
        return self.stage_ref.at[k % self.n_slots, pl.ds(0, self.pieces[k][0].shape[0])]

    def _copy(self, k):
        return pltpu.make_async_copy(self.pieces[k][0], self._slot(k),
                                     self.sem_ref.at[k % self.n_slots])

    def _start_ahead(self):
        while self.started < min(len(self.pieces), self.converted + self.n_slots):
            self._copy(self.started).start()
            self.started += 1

    def ensure(self, count):
        self._start_ahead()
        while self.converted < count:
            k = self.converted
            _, dst, scale = self.pieces[k]
            self._copy(k).wait()
            piece = self._slot(k)[...]
            dst[...] = (piece if scale is None else piece * scale).astype(BF16)
            self.converted += 1
            self._start_ahead()

    def finish(self):
        self.ensure(len(self.pieces))


HBM_SPEC = pl.BlockSpec(memory_space=pl.ANY)


def _w_in_pieces(w_hbm, w_ref):
    pieces = []
    for j in range(N_IN // ATT_MERGED):
        which, g = divmod(j, N_GROUPS)
        if which < 3:
            dst = which * ATT_MERGED + (0 if g == 0 else NAT_WIDTH + (g - 1) * QKV_WIDTH)
        else:
            dst = QKV_WIDTH + (j - 3 * N_GROUPS) * ATT_MERGED
        scale = HEAD_DIM ** -0.5 * LOG2_E if which == 0 else None
        pieces.append((dst, w_hbm.at[:, j * ATT_MERGED:(j + 1) * ATT_MERGED],
                       w_ref.at[:, dst:dst + ATT_MERGED], scale))
    return [piece[1:] for piece in sorted(pieces, key=lambda piece: piece[0])]


def _pooled_minus_token(pf, prev, t, c):
    groups = [g for g in range(len(POOL_WINDOWS))
              if g * POOL_GROUP_WIDTH < (c + 1) * LANES and (g + 1) * POOL_GROUP_WIDTH > c * LANES]
    acc = jnp.concatenate([prev, pf], axis=0)
    w, means = 1, []
    for g in groups:
        while w < POOL_WINDOWS[g]:
            acc = acc + pltpu.roll(acc, w, axis=0)
            w *= 2
        count = jnp.minimum(t + 1, POOL_WINDOWS[g]).astype(F32)
        means.append(acc[POOL_HALO:, :] / count)
    pooled = means[-1]
    for g, mean in zip(groups[-2::-1], means[-2::-1]):
        lane = lax.broadcasted_iota(jnp.int32, mean.shape, 1)
        pooled = jnp.where(lane < (g + 1) * POOL_GROUP_WIDTH - c * LANES, mean, pooled)
    return pooled - pf


def _in_proj_kernel(x_ref, g_ref, w_hbm, zn_ref, zg_ref, *rest):
    dil_refs, (slab_ref, w_ref, stage_ref, sem_ref) = rest[:-4], rest[-4:]

    def body(stream):
        def project(lo, width):
            if stream is not None:
                stream.ensure((lo + width) // ATT_MERGED)
            return jnp.dot(u, w_ref[:, lo:lo + width], preferred_element_type=F32)

        x = x_ref[...]
        u = (x * _rms_scale(x) * g_ref[...]).astype(BF16)
        lo = 0
        for out_ref, width in ((zn_ref, QKV_WIDTH), (zn_ref, POOL_WIDTH),
                               (zg_ref, D_MODEL), (zg_ref, D_MODEL)):
            off = lo - (0 if out_ref is zn_ref else ZN_WIDTH)
            out_ref[:, off:off + width] = project(lo, width).astype(BF16)
            lo += width
        n_slabs = QKV_WIDTH // LANES
        for gi, z_ref in enumerate(dil_refs):
            dilation = ATT_GROUPS[gi + 1][1]
            acc = project(NAT_WIDTH + gi * QKV_WIDTH, QKV_WIDTH)
            for sl in range(n_slabs):
                slab_ref[0, sl] = acc[:, sl * LANES:(sl + 1) * LANES]
            src, n_rows, stride = 0, IN_PROJ_ROWS, dilation
            while stride > STRIDE_STEP:
                n_rows //= STRIDE_STEP
                for c in range(IN_PROJ_ROWS // n_rows):
                    base = (c // STRIDE_STEP) * n_rows * STRIDE_STEP + c % STRIDE_STEP
                    for sl in range(n_slabs):
                        slab_ref[1 - src, sl, c * n_rows:(c + 1) * n_rows, :] = \
                            slab_ref[src, sl, pl.ds(base, n_rows, stride=STRIDE_STEP), :]
                src, stride = 1 - src, stride // STRIDE_STEP
            outer = dilation // stride
            for r in range(dilation):
                lo_class, hi_class = r % outer, r // outer
                base = lo_class * (IN_PROJ_ROWS // outer) + hi_class
                for sl in range(n_slabs):
                    rows = slab_ref[src, sl, pl.ds(base, IN_PROJ_ROWS // dilation, stride=stride), :]
                    z_ref[r, :, sl * LANES:(sl + 1) * LANES] = rows.astype(BF16)

    first = pl.program_id(0) == 0
    pl.when(first)(lambda: body(_WeightStream(_w_in_pieces(w_hbm, w_ref), stage_ref, sem_ref)))
    pl.when(jnp.logical_not(first))(lambda: body(None))


def _in_proj(x2d, g, w_in):
    s = x2d.shape[0]
    dils = [d for _, d in ATT_GROUPS[1:]]
    out_shape = [jax.ShapeDtypeStruct((s, ZN_WIDTH), BF16), jax.ShapeDtypeStruct((s, ZG_WIDTH), BF16)]
    tm = IN_PROJ_ROWS
    out_specs = [pl.BlockSpec((tm, ZN_WIDTH), lambda i: (i, 0)),
                 pl.BlockSpec((tm, ZG_WIDTH), lambda i: (i, 0))]
    for d in dils:
        out_shape.append(jax.ShapeDtypeStruct((d, s // d, QKV_WIDTH), BF16))
        out_specs.append(pl.BlockSpec((d, tm // d, QKV_WIDTH), lambda i: (0, i, 0)))
    return pl.pallas_call(
        _in_proj_kernel,
        out_shape=out_shape,
        grid=(s // tm,),
        in_specs=[
            pl.BlockSpec((tm, D_MODEL), lambda i: (i, 0)),
            pl.BlockSpec((1, D_MODEL), lambda i: (0, 0)),
            HBM_SPEC,
        ],
        out_specs=out_specs,
        scratch_shapes=[pltpu.VMEM((2, QKV_WIDTH // LANES, tm, LANES), F32),
                        pltpu.VMEM((D_MODEL, N_IN), BF16),
                        pltpu.VMEM((WEIGHT_STAGE_SLOTS, D_MODEL, ATT_MERGED), F32),
                        pltpu.SemaphoreType.DMA((WEIGHT_STAGE_SLOTS,))],
        compiler_params=pltpu.CompilerParams(
            dimension_semantics=("arbitrary",), vmem_limit_bytes=VMEM_LIMIT_BYTES),
        name="in_proj",
    )(x2d, g, w_in)


def _attend_block(qb, kk, vv, bias, head_masks_bf16, low_half):
    nh = HEADS_PER_GROUP
    q_stack = jnp.concatenate([qb * head_masks_bf16[h] for h in range(nh)], axis=0)
    s = lax.dot_general(q_stack, kk, (((1,), (1,)), ((), ())), preferred_element_type=F32)
    s = s + bias
    m = jnp.max(s, axis=-1, keepdims=True)
    p = jnp.exp2(s - m)
    l = jnp.sum(p, axis=-1, keepdims=True)
    pv = jnp.dot(p.astype(BF16), vv, preferred_element_type=F32)
    slabs = []
    for sl in range(ATT_MERGED // LANES):
        lanes = slice(sl * LANES, (sl + 1) * LANES)
        rows_a = slice(2 * sl * BLK, (2 * sl + 1) * BLK)
        rows_b = slice((2 * sl + 1) * BLK, (2 * sl + 2) * BLK)
        slabs.append((jnp.where(low_half, pv[rows_a, lanes], pv[rows_b, lanes]),
                      jnp.where(low_half, m[rows_a], m[rows_b]),
                      jnp.where(low_half, l[rows_a], l[rows_b])))
    return slabs


def _attn_kernel(*refs, slopes):
    z_refs = refs[0:N_GROUPS]
    kprev_refs = refs[N_GROUPS:3 * N_GROUPS:2]
    vprev_refs = refs[N_GROUPS + 1:3 * N_GROUPS:2]
    a_ref, bias_ref, bias0_ref, o_scr, m_scr, l_scr, nat_scr = refs[3 * N_GROUPS:]
    step = pl.program_id(0)
    nh = HEADS_PER_GROUP
    n_slabs = ATT_MERGED // LANES

    @pl.when(step == 0)
    def _():
        qi = lax.broadcasted_iota(jnp.int32, (BLK, 2 * BLK), 0)
        kj = lax.broadcasted_iota(jnp.int32, (BLK, 2 * BLK), 1)
        steps = BLK + qi - kj
        for g, (window, dilation) in enumerate(ATT_GROUPS):
            valid = (steps >= 0) & (steps <= window // dilation)
            dist = (steps * dilation).astype(F32)
            for h in range(nh):
                bias = jnp.where(valid, -(slopes[g * nh + h] * LOG2_E) * dist, MASKED_SCORE)
                bias_ref[g, h * BLK:(h + 1) * BLK, :] = bias
                bias0_ref[g, h * BLK:(h + 1) * BLK, :] = jnp.where(kj < BLK, MASKED_SCORE, bias)

    @pl.when(step == 1)
    def _():
        bias0_ref[...] = bias_ref[...]

    lane_head = lax.broadcasted_iota(jnp.int32, (BLK, ATT_MERGED), 1) // HEAD_DIM
    head_masks_bf16 = [(lane_head == h).astype(F32).astype(BF16) for h in range(nh)]
    low_half = lax.broadcasted_iota(jnp.int32, (BLK, LANES), 1) < HEAD_DIM
    attend = functools.partial(_attend_block, head_masks_bf16=head_masks_bf16, low_half=low_half)
    q_cols, k_cols, v_cols = (slice(c * ATT_MERGED, (c + 1) * ATT_MERGED) for c in range(3))

    def aligned(row):
        return row if isinstance(row, int) else pl.multiple_of(row, BLK)

    sub_rows = ATT_TILE // MERGE_MOD

    def store(g, dilation, r, b, slabs):
        if dilation in (1, MERGE_MOD):
            rows = pl.ds(aligned(r * (ATT_TILE // dilation) + b * BLK), BLK)
        else:
            per = dilation // MERGE_MOD
            rows = pl.ds((r % MERGE_MOD) * sub_rows + b * BLK * per + r // MERGE_MOD, BLK,
                         stride=per)
        for sl, (o, m, l) in enumerate(slabs):
            o_scr[g, sl, rows, :] = o
            m_scr[g, sl, rows, :] = m
            l_scr[g, sl, rows, :] = l

    for g, (_, dilation) in enumerate(ATT_GROUPS):
        n_blocks = ATT_TILE // dilation // BLK
        z_ref, kprev_ref, vprev_ref = z_refs[g], kprev_refs[g], vprev_refs[g]

        def load(r, row0, n_rows, cols, z_ref=z_ref, dilation=dilation):
            rows = pl.ds(aligned(row0), n_rows)
            return z_ref[rows, cols] if dilation == 1 else z_ref[r, rows, cols]

        def class_body(r, _, g=g, dilation=dilation, n_blocks=n_blocks, load=load,
                       kprev_ref=kprev_ref, vprev_ref=vprev_ref):
            kprev = kprev_ref[...] if dilation == 1 else kprev_ref[r]
            vprev = vprev_ref[...] if dilation == 1 else vprev_ref[r]
            kk = jnp.concatenate([kprev, load(r, 0, BLK, k_cols)], axis=0)
            vv = jnp.concatenate([vprev, load(r, 0, BLK, v_cols)], axis=0)
            store(g, dilation, r, 0, attend(load(r, 0, BLK, q_cols), kk, vv, bias0_ref[g]))
            for b in range(1, n_blocks):
                kk = load(r, (b - 1) * BLK, 2 * BLK, k_cols)
                vv = load(r, (b - 1) * BLK, 2 * BLK, v_cols)
                store(g, dilation, r, b, attend(load(r, b * BLK, BLK, q_cols), kk, vv, bias_ref[g]))
            return 0

        if dilation == 1:
            class_body(0, 0)
        else:
            lax.fori_loop(0, dilation, class_body, 0, unroll=max(1, ATT_UNROLL // n_blocks))

    chunks = sub_rows // MERGE_ROWS

    def merge_body(idx, _):
        c = idx // chunks
        row0 = (idx % chunks) * MERGE_ROWS
        nat_rows = pl.ds(row0 * MERGE_MOD + c, MERGE_ROWS, stride=MERGE_MOD)
        cls_rows = pl.ds(pl.multiple_of(c * sub_rows + row0, MERGE_ROWS), MERGE_ROWS)
        rows_of = lambda g: nat_rows if ATT_GROUPS[g][1] == 1 else cls_rows
        for sl in range(n_slabs):
            ms = [m_scr[g, sl, rows_of(g), :] for g in range(N_GROUPS)]
            mmax = functools.reduce(jnp.maximum, ms)
            ws = [jnp.exp2(m - mmax) for m in ms]
            num = sum(w * o_scr[g, sl, rows_of(g), :] for g, w in enumerate(ws))
            den = sum(w * l_scr[g, sl, rows_of(g), :] for g, w in enumerate(ws))
            nat_scr[sl, nat_rows, :] = num / den
        return 0

    lax.fori_loop(0, MERGE_MOD * chunks, merge_body, 0)
    for sl in range(n_slabs):
        a_ref[:, sl * LANES:(sl + 1) * LANES] = nat_scr[sl].astype(BF16)


def _attention(z_groups, slopes):
    s = z_groups[0].shape[0]
    cur_specs, prev_specs, prev_args = [], [], []
    for z, (_, d) in zip(z_groups, ATT_GROUPS):
        rows = ATT_TILE // d
        prev_row = lambda i, rows=rows: jnp.maximum(i * (rows // BLK) - 1, 0)
        for col in (1, 2):
            if d == 1:
                prev_specs.append(pl.BlockSpec((BLK, ATT_MERGED),
                                               lambda i, c=col, p=prev_row: (p(i), c)))
            else:
                prev_specs.append(pl.BlockSpec((d, BLK, ATT_MERGED),
                                               lambda i, c=col, p=prev_row: (0, p(i), c)))
            prev_args.append(z)
        if d == 1:
            cur_specs.append(pl.BlockSpec((rows, QKV_WIDTH), lambda i: (i, 0)))
        else:
            cur_specs.append(pl.BlockSpec((d, rows, QKV_WIDTH), lambda i: (0, i, 0)))
    scr_shape = (N_GROUPS, ATT_MERGED // LANES, ATT_TILE, LANES)
    bias_shape = (N_GROUPS, HEADS_PER_GROUP * BLK, 2 * BLK)
    return pl.pallas_call(
        functools.partial(_attn_kernel, slopes=slopes),
        out_shape=jax.ShapeDtypeStruct((s, ATT_MERGED), BF16),
        grid=(s // ATT_TILE,),
        in_specs=cur_specs + prev_specs,
        out_specs=pl.BlockSpec((ATT_TILE, ATT_MERGED), lambda i: (i, 0)),
        scratch_shapes=[pltpu.VMEM(bias_shape, F32), pltpu.VMEM(bias_shape, F32),
                        pltpu.VMEM(scr_shape, F32), pltpu.VMEM(scr_shape, F32),
                        pltpu.VMEM(scr_shape, F32), pltpu.VMEM(scr_shape[1:], F32)],
        compiler_params=pltpu.CompilerParams(
            dimension_semantics=("arbitrary",), vmem_limit_bytes=VMEM_LIMIT_BYTES),
        name="attn",
    )(*z_groups, *prev_args)


def _pool_weight_kernel(wgrp_ref, scale_ref, wpo_ref, wc_ref):
    c = POOL_GROUP_WIDTH
    for g in range(len(POOL_WINDOWS)):
        rows = slice(g * c, (g + 1) * c)
        wc = jnp.dot(wgrp_ref[g], scale_ref[rows, :] * wpo_ref[rows, :],
                     precision=lax.Precision.HIGHEST, preferred_element_type=F32)
        wc_ref[rows, :] = wc.astype(BF16)


def _pool_weight(w_grp, scale_col, wpo):
    return pl.pallas_call(
        _pool_weight_kernel,
        out_shape=jax.ShapeDtypeStruct((POOL_WIDTH, D_MODEL), BF16),
        compiler_params=pltpu.CompilerParams(vmem_limit_bytes=VMEM_LIMIT_BYTES),
        name="pool_weight",
    )(w_grp, scale_col, wpo)


def _post_kernel(a_ref, pz_ref, pzprev_ref, ga_ref, gp_ref, x_ref, wao_hbm, wc_ref, wout_hbm,
                 g_ref, w1_hbm, w2_hbm, gf_ref, o_ref,
                 wao_ref, wout_ref, w1_ref, w2_ref, stage_ref, sem_ref):
    i = pl.program_id(0)

    def weight_pieces():
        max_rows, width = stage_ref.shape[1:]

        def split(w, w_bf, rows, cols):
            step = min(max_rows, rows.stop - rows.start)
            return [(w.at[r:r + step, cols], w_bf.at[r:r + step, cols], None)
                    for r in range(rows.start, rows.stop, step)]

        assert D_MODEL == width and MLP_CHUNK % width == 0
        full = slice(0, D_MODEL)
        groups = [split(wao_hbm, wao_ref, slice(0, ATT_MERGED), full),
                  split(wout_hbm, wout_ref, full, full)]
        for c in range(D_FF // MLP_CHUNK):
            chunk = [slice(k, k + width) for k in range(c * MLP_CHUNK, (c + 1) * MLP_CHUNK, width)]
            groups.append([p for cols in chunk for p in split(w1_hbm, w1_ref, full, cols)])
            groups.append([p for rows in chunk for p in split(w2_hbm, w2_ref, rows, full)])
        return [p for group in groups for p in group], [len(group) for group in groups]

    def body(stream, uses):
        needed = [0]

        def dot(lhs, w_view):
            if stream is not None:
                needed[0] += uses.pop(0)
                stream.ensure(needed[0])
            return jnp.dot(lhs, w_view[...], preferred_element_type=F32)

        att = dot(a_ref[...], wao_ref)
        t = i * ROW_TILE + lax.broadcasted_iota(jnp.int32, (ROW_TILE, 1), 0)
        pooled = []
        for c in range(POOL_WIDTH // LANES):
            lanes = slice(c * LANES, (c + 1) * LANES)
            prev = jnp.where(i > 0, pzprev_ref[:, lanes].astype(F32), 0.0)
            pooled.append(
                _pooled_minus_token(pz_ref[:, lanes].astype(F32), prev, t, c).astype(BF16))
        pool = jnp.dot(jnp.concatenate(pooled, axis=1), wc_ref[...], preferred_element_type=F32)
        merged = (jax.nn.sigmoid(ga_ref[...].astype(F32)) * att
                  + jax.nn.sigmoid(gp_ref[...].astype(F32)) * pool)
        h = x_ref[...] + dot(merged.astype(BF16), wout_ref)

        m = (h * _rms_scale(h) * g_ref[...]).astype(BF16)
        y = h
        for c in range(D_FF // MLP_CHUNK):
            cols = slice(c * MLP_CHUNK, (c + 1) * MLP_CHUNK)
            hid = dot(m, w1_ref.at[:, cols])
            hid = jnp.square(jnp.maximum(hid, 0.0)).astype(BF16)
            y = y + dot(hid, w2_ref.at[cols, :])
        o_ref[...] = y * _rms_scale(y) * gf_ref[...]

    def first_step():
        pieces, uses = weight_pieces()
        body(_WeightStream(pieces, stage_ref, sem_ref), uses)

    pl.when(i == 0)(first_step)
    pl.when(i > 0)(lambda: body(None, None))


def _post(a, zn, zg, x2d, wao, wc, wout, g_mlp, w1, w2, g_final):
    s = x2d.shape[0]
    tm = ROW_TILE
    row = lambda i: (i, 0)
    const = lambda i: (0, 0)
    halo_blocks = tm // POOL_HALO
    assert QKV_WIDTH % POOL_WIDTH == 0
    pool_col = QKV_WIDTH // POOL_WIDTH
    return pl.pallas_call(
        _post_kernel,
        out_shape=jax.ShapeDtypeStruct((s, D_MODEL), F32),
        grid=(s // tm,),
        in_specs=[
            pl.BlockSpec((tm, ATT_MERGED), row),
            pl.BlockSpec((tm, POOL_WIDTH), lambda i: (i, pool_col)),
            pl.BlockSpec((POOL_HALO, POOL_WIDTH),
                         lambda i: (jnp.maximum(i * halo_blocks - 1, 0), pool_col)),
            pl.BlockSpec((tm, D_MODEL), lambda i: (i, 0)),
            pl.BlockSpec((tm, D_MODEL), lambda i: (i, 1)),
            pl.BlockSpec((tm, D_MODEL), row),
            HBM_SPEC,
            pl.BlockSpec((POOL_WIDTH, D_MODEL), const),
            HBM_SPEC,
            pl.BlockSpec((1, D_MODEL), const),
            HBM_SPEC,
            HBM_SPEC,
            pl.BlockSpec((1, D_MODEL), const),
        ],
        out_specs=pl.BlockSpec((tm, D_MODEL), row),
        scratch_shapes=[pltpu.VMEM((ATT_MERGED, D_MODEL), BF16),
                        pltpu.VMEM((D_MODEL, D_MODEL), BF16),
                        pltpu.VMEM((D_MODEL, D_FF), BF16),
                        pltpu.VMEM((D_FF, D_MODEL), BF16),
                        pltpu.VMEM((WEIGHT_STAGE_SLOTS, WEIGHT_STAGE_ROWS, D_MODEL), F32),
                        pltpu.SemaphoreType.DMA((WEIGHT_STAGE_SLOTS,))],
        compiler_params=pltpu.CompilerParams(
            dimension_semantics=("arbitrary",), vmem_limit_bytes=VMEM_LIMIT_BYTES),
        name="post",
    )(a, zn, zn, zg, zg, x2d, wao, wc, wout, g_mlp, w1, w2, g_final)


def _alibi_slopes():
    return tuple(2.0 ** (-ALIBI_MAX_BIAS * (h + 1.0) / N_ATT_HEADS) for h in range(N_ATT_HEADS))


def kernel(x, norm_mix_g, w_in, w_att_out, w_pool_grp, pool_scale, w_pool_out, w_out,
           norm_mlp_g, w_mlp_in, w_mlp_out, norm_final_g):
    batch, seq, d = x.shape
    assert d == D_MODEL and norm_mix_g.shape[0] == 1, "one layer of width D_MODEL"
    assert seq % ATT_TILE == 0 and seq % ROW_TILE == 0 and seq % IN_PROJ_ROWS == 0
    slopes = _alibi_slopes()
    outs = []
    for b in range(batch):
        h = x[b]
        zn, zg, z1, z2 = _in_proj(h, norm_mix_g[0][None, :], w_in[0])
        a = _attention((zn, z1, z2), slopes)
        wc = _pool_weight(w_pool_grp[0], pool_scale[0][:, None], w_pool_out[0])
        outs.append(_post(a, zn, zg, h, w_att_out[0], wc, w_out[0], norm_mlp_g[0][None, :],
                          w_mlp_in[0], w_mlp_out[0], norm_final_g[None, :]))
    return outs[0][None] if batch == 1 else jnp.stack(outs, axis=0)
```

```python
import functools

import jax
import jax.numpy as jnp
from jax import lax
from jax.experimental import pallas as pl
from jax.experimental.pallas import tpu as pltpu

D_MODEL = 1024
HEAD_DIM = 64
ATT_GROUPS = ((128, 1), (512, 4), (2048, 16))
N_GROUPS = len(ATT_GROUPS)
HEADS_PER_GROUP = 4
N_ATT_HEADS = HEADS_PER_GROUP * N_GROUPS
ATT_WIDTH = N_ATT_HEADS * HEAD_DIM
ATT_MERGED = HEADS_PER_GROUP * HEAD_DIM
QKV_WIDTH = 3 * ATT_MERGED
BLK = 128
POOL_WINDOWS = (2, 4, 8, 16)
POOL_GROUP_WIDTH = 3 * D_MODEL // 16
POOL_WIDTH = POOL_GROUP_WIDTH * len(POOL_WINDOWS)
D_FF = 4 * D_MODEL
N_IN = 3 * ATT_WIDTH + POOL_WIDTH + 2 * D_MODEL
NORM_EPS = 1e-6
ALIBI_MAX_BIAS = 8.0
POOL_HALO = max(POOL_WINDOWS)

VMEM_LIMIT_BYTES = 56 * 1024 * 1024
LANES = 128
MASKED_SCORE = -1e30

IN_PROJ_ROWS = 512
ROW_TILE = 512
MLP_CHUNK = 1024
WEIGHT_STAGE_ROWS = 512
WEIGHT_STAGE_SLOTS = 4
MAX_DILATION = max(d for _, d in ATT_GROUPS)
ATT_TILE = BLK * MAX_DILATION
STRIDE_STEP = 4
MERGE_MOD = 4
MERGE_ROWS = 128
LOG2_E = 1.4426950408889634
ATT_UNROLL = 16

ZN_WIDTH = QKV_WIDTH + POOL_WIDTH
ZG_WIDTH = 2 * D_MODEL
NAT_WIDTH = ZN_WIDTH + ZG_WIDTH

BF16 = jnp.bfloat16
F32 = jnp.float32


def _rms_scale(x):
    return lax.rsqrt(jnp.mean(x * x, axis=-1, keepdims=True) + NORM_EPS)


class _WeightStream:
    def __init__(self, pieces, stage_ref, sem_ref):
        self.pieces, self.stage_ref, self.sem_ref = pieces, stage_ref, sem_ref
        self.n_slots = stage_ref.shape[0]
        self.started = self.converted = 0

    def _slot(self, k):
        return self.stage_ref.at[k % self.n_slots, pl.ds(0, self.pieces[k][0].shape[0])]

    def _copy(self, k):
        return pltpu.make_async_copy(self.pieces[k][0], self._slot(k),
                                     self.sem_ref.at[k % self.n_slots])

    def _start_ahead(self):
        while self.started < min(len(self.pieces), self.converted + self.n_slots):
            self._copy(self.started).start()
            self.started += 1

    def ensure(self, count):
        self._start_ahead()
        while self.converted < count:
            k = self.converted
            _, dst, scale = self.pieces[k]
            self._copy(k).wait()
            piece = self._slot(k)[...]
            dst[...] = (piece if scale is None else piece * scale).astype(BF16)
            self.converted += 1
            self._start_ahead()

    def finish(self):
        self.ensure(len(self.pieces))


HBM_SPEC = pl.BlockSpec(memory_space=pl.ANY)


def _w_in_pieces(w_hbm, w_ref):
    pieces = []
    for j in range(N_IN // ATT_MERGED):
        which, g = divmod(j, N_GROUPS)
        if which < 3:
            dst = which * ATT_MERGED + (0 if g == 0 else NAT_WIDTH + (g - 1) * QKV_WIDTH)
        else:
            dst = QKV_WIDTH + (j - 3 * N_GROUPS) * ATT_MERGED
        scale = HEAD_DIM ** -0.5 * LOG2_E if which == 0 else None
        pieces.append((dst, w_hbm.at[:, j * ATT_MERGED:(j + 1) * ATT_MERGED],
                       w_ref.at[:, dst:dst + ATT_MERGED], scale))
    return [piece[1:] for piece in sorted(pieces, key=lambda piece: piece[0])]


def _pooled_minus_token(pf, prev, t, c):
    groups = [g for g in range(len(POOL_WINDOWS))
              if g * POOL_GROUP_WIDTH < (c + 1) * LANES and (g + 1) * POOL_GROUP_WIDTH > c * LANES]
    acc = jnp.concatenate([prev, pf], axis=0)
    w, means = 1, []
    for g in groups:
        while w < POOL_WINDOWS[g]:
            acc = acc + pltpu.roll(acc, w, axis=0)
            w *= 2
        count = jnp.minimum(t + 1, POOL_WINDOWS[g]).astype(F32)
        means.append(acc[POOL_HALO:, :] / count)
    pooled = means[-1]
    for g, mean in zip(groups[-2::-1], means[-2::-1]):
        lane = lax.broadcasted_iota(jnp.int32, mean.shape, 1)
        pooled = jnp.where(lane < (g + 1) * POOL_GROUP_WIDTH - c * LANES, mean, pooled)
    return pooled - pf


def _in_proj_kernel(x_ref, g_ref, w_hbm, zn_ref, zg_ref, *rest):
    dil_refs, (slab_ref, w_ref, stage_ref, sem_ref) = rest[:-4], rest[-4:]

    def body(stream):
        def project(lo, width):
            if stream is not None:
                stream.ensure((lo + width) // ATT_MERGED)
            return jnp.dot(u, w_ref[:, lo:lo + width], preferred_element_type=F32)

        x = x_ref[...]
        u = (x * _rms_scale(x) * g_ref[...]).astype(BF16)
        lo = 0
        for out_ref, width in ((zn_ref, QKV_WIDTH), (zn_ref, POOL_WIDTH),
                               (zg_ref, D_MODEL), (zg_ref, D_MODEL)):
            off = lo - (0 if out_ref is zn_ref else ZN_WIDTH)
            out_ref[:, off:off + width] = project(lo, width).astype(BF16)
            lo += width
        n_slabs = QKV_WIDTH // LANES
        for gi, z_ref in enumerate(dil_refs):
            dilation = ATT_GROUPS[gi + 1][1]
            acc = project(NAT_WIDTH + gi * QKV_WIDTH, QKV_WIDTH)
            for sl in range(n_slabs):
                slab_ref[0, sl] = acc[:, sl * LANES:(sl + 1) * LANES]
            src, n_rows, stride = 0, IN_PROJ_ROWS, dilation
            while stride > STRIDE_STEP:
                n_rows //= STRIDE_STEP
                for c in range(IN_PROJ_ROWS // n_rows):
                    base = (c // STRIDE_STEP) * n_rows * STRIDE_STEP + c % STRIDE_STEP
                    for sl in range(n_slabs):
                        slab_ref[1 - src, sl, c * n_rows:(c + 1) * n_rows, :] = \
                            slab_ref[src, sl, pl.ds(base, n_rows, stride=STRIDE_STEP), :]
                src, stride = 1 - src, stride // STRIDE_STEP
            outer = dilation // stride
            for r in range(dilation):
                lo_class, hi_class = r % outer, r // outer
                base = lo_class * (IN_PROJ_ROWS // outer) + hi_class
                for sl in range(n_slabs):
                    rows = slab_ref[src, sl, pl.ds(base, IN_PROJ_ROWS // dilation, stride=stride), :]
                    z_ref[r, :, sl * LANES:(sl + 1) * LANES] = rows.astype(BF16)

    first = pl.program_id(0) == 0
    pl.when(first)(lambda: body(_WeightStream(_w_in_pieces(w_hbm, w_ref), stage_ref, sem_ref)))
    pl.when(jnp.logical_not(first))(lambda: body(None))


def _in_proj(x2d, g, w_in):
    s = x2d.shape[0]
    dils = [d for _, d in ATT_GROUPS[1:]]
    out_shape = [jax.ShapeDtypeStruct((s, ZN_WIDTH), BF16), jax.ShapeDtypeStruct((s, ZG_WIDTH), BF16)]
    tm = IN_PROJ_ROWS
    out_specs = [pl.BlockSpec((tm, ZN_WIDTH), lambda i: (i, 0)),
                 pl.BlockSpec((tm, ZG_WIDTH), lambda i: (i, 0))]
    for d in dils:
        out_shape.append(jax.ShapeDtypeStruct((d, s // d, QKV_WIDTH), BF16))
        out_specs.append(pl.BlockSpec((d, tm // d, QKV_WIDTH), lambda i: (0, i, 0)))
    return pl.pallas_call(
        _in_proj_kernel,
        out_shape=out_shape,
        grid=(s // tm,),
        in_specs=[
            pl.BlockSpec((tm, D_MODEL), lambda i: (i, 0)),
            pl.BlockSpec((1, D_MODEL), lambda i: (0, 0)),
            HBM_SPEC,
        ],
        out_specs=out_specs,
        scratch_shapes=[pltpu.VMEM((2, QKV_WIDTH // LANES, tm, LANES), F32),
                        pltpu.VMEM((D_MODEL, N_IN), BF16),
                        pltpu.VMEM((WEIGHT_STAGE_SLOTS, D_MODEL, ATT_MERGED), F32),
                        pltpu.SemaphoreType.DMA((WEIGHT_STAGE_SLOTS,))],
        compiler_params=pltpu.CompilerParams(
            dimension_semantics=("arbitrary",), vmem_limit_bytes=VMEM_LIMIT_BYTES),
        name="in_proj",
    )(x2d, g, w_in)


def _attend_block(qb, kk, vv, bias, head_masks_bf16, low_half):
    nh = HEADS_PER_GROUP
    q_stack = jnp.concatenate([qb * head_masks_bf16[h] for h in range(nh)], axis=0)
    s = lax.dot_general(q_stack, kk, (((1,), (1,)), ((), ())), preferred_element_type=F32)
    s = s + bias
    m = jnp.max(s, axis=-1, keepdims=True)
    p = jnp.exp2(s - m)
    l = jnp.sum(p, axis=-1, keepdims=True)
    pv = jnp.dot(p.astype(BF16), vv, preferred_element_type=F32)
    slabs = []
    for sl in range(ATT_MERGED // LANES):
        lanes = slice(sl * LANES, (sl + 1) * LANES)
        rows_a = slice(2 * sl * BLK, (2 * sl + 1) * BLK)
        rows_b = slice((2 * sl + 1) * BLK, (2 * sl + 2) * BLK)
        slabs.append((jnp.where(low_half, pv[rows_a, lanes], pv[rows_b, lanes]),
                      jnp.where(low_half, m[rows_a], m[rows_b]),
                      jnp.where(low_half, l[rows_a], l[rows_b])))
    return slabs


def _attn_kernel(*refs, slopes):
    z_refs = refs[0:N_GROUPS]
    kprev_refs = refs[N_GROUPS:3 * N_GROUPS:2]
    vprev_refs = refs[N_GROUPS + 1:3 * N_GROUPS:2]
    a_ref, bias_ref, bias0_ref, o_scr, m_scr, l_scr, nat_scr = refs[3 * N_GROUPS:]
    step = pl.program_id(0)
    nh = HEADS_PER_GROUP
    n_slabs = ATT_MERGED // LANES

    @pl.when(step == 0)
    def _():
        qi = lax.broadcasted_iota(jnp.int32, (BLK, 2 * BLK), 0)
        kj = lax.broadcasted_iota(jnp.int32, (BLK, 2 * BLK), 1)
        steps = BLK + qi - kj
        for g, (window, dilation) in enumerate(ATT_GROUPS):
            valid = (steps >= 0) & (steps <= window // dilation)
            dist = (steps * dilation).astype(F32)
            for h in range(nh):
                bias = jnp.where(valid, -(slopes[g * nh + h] * LOG2_E) * dist, MASKED_SCORE)
                bias_ref[g, h * BLK:(h + 1) * BLK, :] = bias
                bias0_ref[g, h * BLK:(h + 1) * BLK, :] = jnp.where(kj < BLK, MASKED_SCORE, bias)

    @pl.when(step == 1)
    def _():
        bias0_ref[...] = bias_ref[...]

    lane_head = lax.broadcasted_iota(jnp.int32, (BLK, ATT_MERGED), 1) // HEAD_DIM
    head_masks_bf16 = [(lane_head == h).astype(F32).astype(BF16) for h in range(nh)]
    low_half = lax.broadcasted_iota(jnp.int32, (BLK, LANES), 1) < HEAD_DIM
    attend = functools.partial(_attend_block, head_masks_bf16=head_masks_bf16, low_half=low_half)
    q_cols, k_cols, v_cols = (slice(c * ATT_MERGED, (c + 1) * ATT_MERGED) for c in range(3))

    def aligned(row):
        return row if isinstance(row, int) else pl.multiple_of(row, BLK)

    sub_rows = ATT_TILE // MERGE_MOD

    def store(g, dilation, r, b, slabs):
        if dilation in (1, MERGE_MOD):
            rows = pl.ds(aligned(r * (ATT_TILE // dilation) + b * BLK), BLK)
        else:
            per = dilation // MERGE_MOD
            rows = pl.ds((r % MERGE_MOD) * sub_rows + b * BLK * per + r // MERGE_MOD, BLK,
                         stride=per)
        for sl, (o, m, l) in enumerate(slabs):
            o_scr[g, sl, rows, :] = o
            m_scr[g, sl, rows, :] = m
            l_scr[g, sl, rows, :] = l

    for g, (_, dilation) in enumerate(ATT_GROUPS):
        n_blocks = ATT_TILE // dilation // BLK
        z_ref, kprev_ref, vprev_ref = z_refs[g], kprev_refs[g], vprev_refs[g]

        def load(r, row0, n_rows, cols, z_ref=z_ref, dilation=dilation):
            rows = pl.ds(aligned(row0), n_rows)
            return z_ref[rows, cols] if dilation == 1 else z_ref[r, rows, cols]

        def class_body(r, _, g=g, dilation=dilation, n_blocks=n_blocks, load=load,
                       kprev_ref=kprev_ref, vprev_ref=vprev_ref):
            kprev = kprev_ref[...] if dilation == 1 else kprev_ref[r]
            vprev = vprev_ref[...] if dilation == 1 else vprev_ref[r]
            kk = jnp.concatenate([kprev, load(r, 0, BLK, k_cols)], axis=0)
            vv = jnp.concatenate([vprev, load(r, 0, BLK, v_cols)], axis=0)
            store(g, dilation, r, 0, attend(load(r, 0, BLK, q_cols), kk, vv, bias0_ref[g]))
            for b in range(1, n_blocks):
                kk = load(r, (b - 1) * BLK, 2 * BLK, k_cols)
                vv = load(r, (b - 1) * BLK, 2 * BLK, v_cols)
                store(g, dilation, r, b, attend(load(r, b * BLK, BLK, q_cols), kk, vv, bias_ref[g]))
            return 0

        if dilation == 1:
            class_body(0, 0)
        else:
            lax.fori_loop(0, dilation, class_body, 0, unroll=max(1, ATT_UNROLL // n_blocks))

    chunks = sub_rows // MERGE_ROWS

    def merge_body(idx, _):
        c = idx // chunks
        row0 = (idx % chunks) * MERGE_ROWS
        nat_rows = pl.ds(row0 * MERGE_MOD + c, MERGE_ROWS, stride=MERGE_MOD)
        cls_rows = pl.ds(pl.multiple_of(c * sub_rows + row0, MERGE_ROWS), MERGE_ROWS)
        rows_of = lambda g: nat_rows if ATT_GROUPS[g][1] == 1 else cls_rows
        for sl in range(n_slabs):
            ms = [m_scr[g, sl, rows_of(g), :] for g in range(N_GROUPS)]
            mmax = functools.reduce(jnp.maximum, ms)
            ws = [jnp.exp2(m - mmax) for m in ms]
            num = sum(w * o_scr[g, sl, rows_of(g), :] for g, w in enumerate(ws))
            den = sum(w * l_scr[g, sl, rows_of(g), :] for g, w in enumerate(ws))
            nat_scr[sl, nat_rows, :] = num / den
        return 0

    lax.fori_loop(0, MERGE_MOD * chunks, merge_body, 0)
    for sl in range(n_slabs):
        a_ref[:, sl * LANES:(sl + 1) * LANES] = nat_scr[sl].astype(BF16)


def _attention(z_groups, slopes):
    s = z_groups[0].shape[0]
    cur_specs, prev_specs, prev_args = [], [], []
    for z, (_, d) in zip(z_groups, ATT_GROUPS):
        rows = ATT_TILE // d
        prev_row = lambda i, rows=rows: jnp.maximum(i * (rows // BLK) - 1, 0)
        for col in (1, 2):
            if d == 1:
                prev_specs.append(pl.BlockSpec((BLK, ATT_MERGED),
                                               lambda i, c=col, p=prev_row: (p(i), c)))
            else:
                prev_specs.append(pl.BlockSpec((d, BLK, ATT_MERGED),
                                               lambda i, c=col, p=prev_row: (0, p(i), c)))
            prev_args.append(z)
        if d == 1:
            cur_specs.append(pl.BlockSpec((rows, QKV_WIDTH), lambda i: (i, 0)))
        else:
            cur_specs.append(pl.BlockSpec((d, rows, QKV_WIDTH), lambda i: (0, i, 0)))
    scr_shape = (N_GROUPS, ATT_MERGED // LANES, ATT_TILE, LANES)
    bias_shape = (N_GROUPS, HEADS_PER_GROUP * BLK, 2 * BLK)
    return pl.pallas_call(
        functools.partial(_attn_kernel, slopes=slopes),
        out_shape=jax.ShapeDtypeStruct((s, ATT_MERGED), BF16),
        grid=(s // ATT_TILE,),
        in_specs=cur_specs + prev_specs,
        out_specs=pl.BlockSpec((ATT_TILE, ATT_MERGED), lambda i: (i, 0)),
        scratch_shapes=[pltpu.VMEM(bias_shape, F32), pltpu.VMEM(bias_shape, F32),
                        pltpu.VMEM(scr_shape, F32), pltpu.VMEM(scr_shape, F32),
                        pltpu.VMEM(scr_shape, F32), pltpu.VMEM(scr_shape[1:], F32)],
        compiler_params=pltpu.CompilerParams(
            dimension_semantics=("arbitrary",), vmem_limit_bytes=VMEM_LIMIT_BYTES),
        name="attn",
    )(*z_groups, *prev_args)


def _pool_weight_kernel(wgrp_ref, scale_ref, wpo_ref, wc_ref):
    c = POOL_GROUP_WIDTH
    for g in range(len(POOL_WINDOWS)):
        rows = slice(g * c, (g + 1) * c)
        wc = jnp.dot(wgrp_ref[g], scale_ref[rows, :] * wpo_ref[rows, :],
                     precision=lax.Precision.HIGHEST, preferred_element_type=F32)
        wc_ref[rows, :] = wc.astype(BF16)


def _pool_weight(w_grp, scale_col, wpo):
    return pl.pallas_call(
        _pool_weight_kernel,
        out_shape=jax.ShapeDtypeStruct((POOL_WIDTH, D_MODEL), BF16),
        compiler_params=pltpu.CompilerParams(vmem_limit_bytes=VMEM_LIMIT_BYTES),
        name="pool_weight",
    )(w_grp, scale_col, wpo)


def _post_kernel(a_ref, pz_ref, pzprev_ref, ga_ref, gp_ref, x_ref, wao_hbm, wc_ref, wout_hbm,
                 g_ref, w1_hbm, w2_hbm, gf_ref, o_ref,
                 wao_ref, wout_ref, w1_ref, w2_ref, stage_ref, sem_ref):
    i = pl.program_id(0)

    def weight_pieces():
        max_rows, width = stage_ref.shape[1:]

        def split(w, w_bf, rows, cols):
            step = min(max_rows, rows.stop - rows.start)
            return [(w.at[r:r + step, cols], w_bf.at[r:r + step, cols], None)
                    for r in range(rows.start, rows.stop, step)]

        assert D_MODEL == width and MLP_CHUNK % width == 0
        full = slice(0, D_MODEL)
        groups = [split(wao_hbm, wao_ref, slice(0, ATT_MERGED), full),
                  split(wout_hbm, wout_ref, full, full)]
        for c in range(D_FF // MLP_CHUNK):
            chunk = [slice(k, k + width) for k in range(c * MLP_CHUNK, (c + 1) * MLP_CHUNK, width)]
            groups.append([p for cols in chunk for p in split(w1_hbm, w1_ref, full, cols)])
            groups.append([p for rows in chunk for p in split(w2_hbm, w2_ref, rows, full)])
        return [p for group in groups for p in group], [len(group) for group in groups]

    def body(stream, uses):
        needed = [0]

        def dot(lhs, w_view):
            if stream is not None:
                needed[0] += uses.pop(0)
                stream.ensure(needed[0])
            return jnp.dot(lhs, w_view[...], preferred_element_type=F32)

        att = dot(a_ref[...], wao_ref)
        t = i * ROW_TILE + lax.broadcasted_iota(jnp.int32, (ROW_TILE, 1), 0)
        pooled = []
        for c in range(POOL_WIDTH // LANES):
            lanes = slice(c * LANES, (c + 1) * LANES)
            prev = jnp.where(i > 0, pzprev_ref[:, lanes].astype(F32), 0.0)
            pooled.append(
                _pooled_minus_token(pz_ref[:, lanes].astype(F32), prev, t, c).astype(BF16))
        pool = jnp.dot(jnp.concatenate(pooled, axis=1), wc_ref[...], preferred_element_type=F32)
        merged = (jax.nn.sigmoid(ga_ref[...].astype(F32)) * att
                  + jax.nn.sigmoid(gp_ref[...].astype(F32)) * pool)
        h = x_ref[...] + dot(merged.astype(BF16), wout_ref)

        m = (h * _rms_scale(h) * g_ref[...]).astype(BF16)
        y = h
        for c in range(D_FF // MLP_CHUNK):
            cols = slice(c * MLP_CHUNK, (c + 1) * MLP_CHUNK)
            hid = dot(m, w1_ref.at[:, cols])
            hid = jnp.square(jnp.maximum(hid, 0.0)).astype(BF16)
            y = y + dot(hid, w2_ref.at[cols, :])
        o_ref[...] = y * _rms_scale(y) * gf_ref[...]

    def first_step():
        pieces, uses = weight_pieces()
        body(_WeightStream(pieces, stage_ref, sem_ref), uses)

    pl.when(i == 0)(first_step)
    pl.when(i > 0)(lambda: body(None, None))


def _post(a, zn, zg, x2d, wao, wc, wout, g_mlp, w1, w2, g_final):
    s = x2d.shape[0]
    tm = ROW_TILE
    row = lambda i: (i, 0)
    const = lambda i: (0, 0)
    halo_blocks = tm // POOL_HALO
    assert QKV_WIDTH % POOL_WIDTH == 0
    pool_col = QKV_WIDTH // POOL_WIDTH
    return pl.pallas_call(
        _post_kernel,
        out_shape=jax.ShapeDtypeStruct((s, D_MODEL), F32),
        grid=(s // tm,),
        in_specs=[
            pl.BlockSpec((tm, ATT_MERGED), row),
            pl.BlockSpec((tm, POOL_WIDTH), lambda i: (i, pool_col)),
            pl.BlockSpec((POOL_HALO, POOL_WIDTH),
                         lambda i: (jnp.maximum(i * halo_blocks - 1, 0), pool_col)),
            pl.BlockSpec((tm, D_MODEL), lambda i: (i, 0)),
            pl.BlockSpec((tm, D_MODEL), lambda i: (i, 1)),
            pl.BlockSpec((tm, D_MODEL), row),
            HBM_SPEC,
            pl.BlockSpec((POOL_WIDTH, D_MODEL), const),
            HBM_SPEC,
            pl.BlockSpec((1, D_MODEL), const),
            HBM_SPEC,
            HBM_SPEC,
            pl.BlockSpec((1, D_MODEL), const),
        ],
        out_specs=pl.BlockSpec((tm, D_MODEL), row),
        scratch_shapes=[pltpu.VMEM((ATT_MERGED, D_MODEL), BF16),
                        pltpu.VMEM((D_MODEL, D_MODEL), BF16),
                        pltpu.VMEM((D_MODEL, D_FF), BF16),
                        pltpu.VMEM((D_FF, D_MODEL), BF16),
                        pltpu.VMEM((WEIGHT_STAGE_SLOTS, WEIGHT_STAGE_ROWS, D_MODEL), F32),
                        pltpu.SemaphoreType.DMA((WEIGHT_STAGE_SLOTS,))],
        compiler_params=pltpu.CompilerParams(
            dimension_semantics=("arbitrary",), vmem_limit_bytes=VMEM_LIMIT_BYTES),
        name="post",
    )(a, zn, zn, zg, zg, x2d, wao, wc, wout, g_mlp, w1, w2, g_final)


def _alibi_slopes():
    return tuple(2.0 ** (-ALIBI_MAX_BIAS * (h + 1.0) / N_ATT_HEADS) for h in range(N_ATT_HEADS))


def kernel(x, norm_mix_g, w_in, w_att_out, w_pool_grp, pool_scale, w_pool_out, w_out,
           norm_mlp_g, w_mlp_in, w_mlp_out, norm_final_g):
    batch, seq, d = x.shape
    assert d == D_MODEL and norm_mix_g.shape[0] == 1, "one layer of width D_MODEL"
    assert seq % ATT_TILE == 0 and seq % ROW_TILE == 0 and seq % IN_PROJ_ROWS == 0
    slopes = _alibi_slopes()
    outs = []
    for b in range(batch):
        h = x[b]
        zn, zg, z1, z2 = _in_proj(h, norm_mix_g[0][None, :], w_in[0])
        a = _attention((zn, z1, z2), slopes)
        wc = _pool_weight(w_pool_grp[0], pool_scale[0][:, None], w_pool_out[0])
        outs.append(_post(a, zn, zg, h, w_att_out[0], wc, w_out[0], norm_mlp_g[0][None, :],
                          w_mlp_in[0], w_mlp_out[0], norm_final_g[None, :]))
    return outs[0][None] if batch == 1 else jnp.stack(outs, axis=0)
```

```python
import functools
import math

import jax
import jax.numpy as jnp
from jax import lax
from jax.experimental import pallas as pl
from jax.experimental.pallas import tpu as pltpu

D_MODEL = 1024
HEAD_DIM = 64
ATT_GROUPS = ((128, 1), (512, 4), (2048, 16))
N_GROUPS = len(ATT_GROUPS)
HEADS_PER_GROUP = 4
N_ATT_HEADS = HEADS_PER_GROUP * N_GROUPS
ATT_WIDTH = N_ATT_HEADS * HEAD_DIM
ATT_MERGED = HEADS_PER_GROUP * HEAD_DIM
QKV_WIDTH = 3 * ATT_MERGED
BLK = 128
POOL_WINDOWS = (2, 4, 8, 16)
POOL_GROUP_WIDTH = 3 * D_MODEL // 16
POOL_WIDTH = POOL_GROUP_WIDTH * len(POOL_WINDOWS)
D_FF = 4 * D_MODEL
N_IN = 3 * ATT_WIDTH + POOL_WIDTH + 2 * D_MODEL
NORM_EPS = 1e-6
ALIBI_MAX_BIAS = 8.0
POOL_HALO = max(POOL_WINDOWS)

VMEM_LIMIT_BYTES = 56 * 1024 * 1024
LANES = 128
MASKED_SCORE = -1e30

IN_PROJ_ROWS = 512
ROW_TILE = 512
MLP_CHUNK = 1024
WEIGHT_STAGE_ROWS = 512
WEIGHT_STAGE_SLOTS = 4
MAX_DILATION = max(d for _, d in ATT_GROUPS)
ATT_TILE = BLK * MAX_DILATION
STRIDE_STEP = 4
MERGE_MOD = 4
LOG2_E = math.log2(math.e)

ZN_WIDTH = QKV_WIDTH + POOL_WIDTH
ZG_WIDTH = 2 * D_MODEL
NAT_WIDTH = ZN_WIDTH + ZG_WIDTH

BF16 = jnp.bfloat16
F32 = jnp.float32


def _rms_scale(x):
    return lax.rsqrt(jnp.mean(x * x, axis=-1, keepdims=True) + NORM_EPS)


class _WeightStream:
    def __init__(self, pieces, stage_ref, sem_ref):
        self.pieces, self.stage_ref, self.sem_ref = pieces, stage_ref, sem_ref
        self.n_slots = stage_ref.shape[0]
        self.started = self.converted = 0

    def _slot(self, k):
        return self.stage_ref.at[k % self.n_slots, pl.ds(0, self.pieces[k][0].shape[0])]

    def _copy(self, k):
        return pltpu.make_async_copy(self.pieces[k][0], self._slot(k),
                                     self.sem_ref.at[k % self.n_slots])

    def _start_ahead(self):
        while self.started < min(len(self.pieces), self.converted + self.n_slots):
            self._copy(self.started).start()
            self.started += 1

    def ensure(self, count):
        self._start_ahead()
        while self.converted < count:
            k = self.converted
            _, dst, scale = self.pieces[k]
            self._copy(k).wait()
            piece = self._slot(k)[...]
            dst[...] = (piece if scale is None else piece * scale).astype(BF16)
            self.converted += 1
            self._start_ahead()


HBM_SPEC = pl.BlockSpec(memory_space=pl.ANY)


def _w_in_pieces(w_hbm, w_ref):
    pieces = []
    for j in range(N_IN // ATT_MERGED):
        which, g = divmod(j, N_GROUPS)
        if which < 3:
            dst = which * ATT_MERGED + (0 if g == 0 else NAT_WIDTH + (g - 1) * QKV_WIDTH)
        else:
            dst = QKV_WIDTH + (j - 3 * N_GROUPS) * ATT_MERGED
        scale = HEAD_DIM ** -0.5 * LOG2_E if which == 0 else None
        pieces.append((dst, w_hbm.at[:, j * ATT_MERGED:(j + 1) * ATT_MERGED],
                       w_ref.at[:, dst:dst + ATT_MERGED], scale))
    return [piece[1:] for piece in sorted(pieces, key=lambda piece: piece[0])]


def _pooled_minus_token(pf, prev, t, c):
    groups = [g for g in range(len(POOL_WINDOWS))
              if g * POOL_GROUP_WIDTH < (c + 1) * LANES and (g + 1) * POOL_GROUP_WIDTH > c * LANES]
    acc = jnp.concatenate([prev, pf], axis=0)
    w, means = 1, []
    for g in groups:
        while w < POOL_WINDOWS[g]:
            acc = acc + pltpu.roll(acc, w, axis=0)
            w *= 2
        count = jnp.minimum(t + 1, POOL_WINDOWS[g]).astype(F32)
        means.append(acc[POOL_HALO:, :] / count)
    pooled = means[-1]
    for g, mean in zip(groups[-2::-1], means[-2::-1]):
        lane = lax.broadcasted_iota(jnp.int32, mean.shape, 1)
        pooled = jnp.where(lane < (g + 1) * POOL_GROUP_WIDTH - c * LANES, mean, pooled)
    return pooled - pf


def _in_proj_kernel(x_ref, g_ref, w_hbm, zn_ref, zg_ref, *rest):
    dil_refs, (slab_ref, w_ref, stage_ref, sem_ref) = rest[:-4], rest[-4:]

    def body(stream):
        def project(lo, width):
            if stream is not None:
                stream.ensure((lo + width) // ATT_MERGED)
            return jnp.dot(u, w_ref[:, lo:lo + width], preferred_element_type=F32)

        x = x_ref[...]
        u = (x * _rms_scale(x) * g_ref[...]).astype(BF16)
        lo = 0
        for out_ref, width in ((zn_ref, QKV_WIDTH), (zn_ref, POOL_WIDTH),
                               (zg_ref, D_MODEL), (zg_ref, D_MODEL)):
            off = lo - (0 if out_ref is zn_ref else ZN_WIDTH)
            out_ref[:, off:off + width] = project(lo, width).astype(BF16)
            lo += width
        n_slabs = QKV_WIDTH // LANES
        for gi, z_ref in enumerate(dil_refs):
            dilation = ATT_GROUPS[gi + 1][1]
            acc = project(NAT_WIDTH + gi * QKV_WIDTH, QKV_WIDTH)
            for sl in range(n_slabs):
                slab_ref[0, sl] = acc[:, sl * LANES:(sl + 1) * LANES]
            src, n_rows, stride = 0, IN_PROJ_ROWS, dilation
            while stride > STRIDE_STEP:
                n_rows //= STRIDE_STEP
                for c in range(IN_PROJ_ROWS // n_rows):
                    base = (c // STRIDE_STEP) * n_rows * STRIDE_STEP + c % STRIDE_STEP
                    for sl in range(n_slabs):
                        slab_ref[1 - src, sl, c * n_rows:(c + 1) * n_rows, :] = \
                            slab_ref[src, sl, pl.ds(base, n_rows, stride=STRIDE_STEP), :]
                src, stride = 1 - src, stride // STRIDE_STEP
            outer = dilation // stride
            for r in range(dilation):
                lo_class, hi_class = r % outer, r // outer
                base = lo_class * (IN_PROJ_ROWS // outer) + hi_class
                for sl in range(n_slabs):
                    rows = slab_ref[src, sl, pl.ds(base, IN_PROJ_ROWS // dilation, stride=stride), :]
                    z_ref[r, :, sl * LANES:(sl + 1) * LANES] = rows.astype(BF16)

    first = pl.program_id(0) == 0
    pl.when(first)(lambda: body(_WeightStream(_w_in_pieces(w_hbm, w_ref), stage_ref, sem_ref)))
    pl.when(jnp.logical_not(first))(lambda: body(None))


def _in_proj(x2d, g, w_in):
    s = x2d.shape[0]
    dils = [d for _, d in ATT_GROUPS[1:]]
    out_shape = [jax.ShapeDtypeStruct((s, ZN_WIDTH), BF16), jax.ShapeDtypeStruct((s, ZG_WIDTH), BF16)]
    tm = IN_PROJ_ROWS
    out_specs = [pl.BlockSpec((tm, ZN_WIDTH), lambda i: (i, 0)),
                 pl.BlockSpec((tm, ZG_WIDTH), lambda i: (i, 0))]
    for d in dils:
        out_shape.append(jax.ShapeDtypeStruct((d, s // d, QKV_WIDTH), BF16))
        out_specs.append(pl.BlockSpec((d, tm // d, QKV_WIDTH), lambda i: (0, i, 0)))
    return pl.pallas_call(
        _in_proj_kernel,
        out_shape=out_shape,
        grid=(s // tm,),
        in_specs=[
            pl.BlockSpec((tm, D_MODEL), lambda i: (i, 0)),
            pl.BlockSpec((1, D_MODEL), lambda i: (0, 0)),
            HBM_SPEC,
        ],
        out_specs=out_specs,
        scratch_shapes=[pltpu.VMEM((2, QKV_WIDTH // LANES, tm, LANES), F32),
                        pltpu.VMEM((D_MODEL, N_IN), BF16),
                        pltpu.VMEM((WEIGHT_STAGE_SLOTS, D_MODEL, ATT_MERGED), F32),
                        pltpu.SemaphoreType.DMA((WEIGHT_STAGE_SLOTS,))],
        compiler_params=pltpu.CompilerParams(
            dimension_semantics=("arbitrary",), vmem_limit_bytes=VMEM_LIMIT_BYTES),
        name="in_proj",
    )(x2d, g, w_in)


def _attend_block(qb, kk, vv, bias, head_masks_bf16, low_half):
    nh = HEADS_PER_GROUP
    q_stack = jnp.concatenate([qb * head_masks_bf16[h] for h in range(nh)], axis=0)
    s = lax.dot_general(q_stack, kk, (((1,), (1,)), ((), ())), preferred_element_type=F32)
    s = s + bias
    m = jnp.max(s, axis=-1, keepdims=True)
    p = jnp.exp2(s - m)
    l = jnp.sum(p, axis=-1, keepdims=True)
    pv = jnp.dot(p.astype(BF16), vv, preferred_element_type=F32)
    slabs = []
    for sl in range(ATT_MERGED // LANES):
        lanes = slice(sl * LANES, (sl + 1) * LANES)
        rows_a = slice(2 * sl * BLK, (2 * sl + 1) * BLK)
        rows_b = slice((2 * sl + 1) * BLK, (2 * sl + 2) * BLK)
        slabs.append((jnp.where(low_half, pv[rows_a, lanes], pv[rows_b, lanes]),
                      jnp.where(low_half, m[rows_a], m[rows_b]),
                      jnp.where(low_half, l[rows_a], l[rows_b])))
    return slabs


def _attn_kernel(*refs, slopes):
    z_refs = refs[0:N_GROUPS]
    kprev_refs = refs[N_GROUPS:3 * N_GROUPS:2]
    vprev_refs = refs[N_GROUPS + 1:3 * N_GROUPS:2]
    a_ref, bias_ref, bias0_ref, o_scr, m_scr, l_scr, nat_scr = refs[3 * N_GROUPS:]
    step = pl.program_id(0)
    nh = HEADS_PER_GROUP
    n_slabs = ATT_MERGED // LANES

    @pl.when(step == 0)
    def _():
        qi = lax.broadcasted_iota(jnp.int32, (BLK, 2 * BLK), 0)
        kj = lax.broadcasted_iota(jnp.int32, (BLK, 2 * BLK), 1)
        steps = BLK + qi - kj
        for g, (window, dilation) in enumerate(ATT_GROUPS):
            valid = (steps >= 0) & (steps <= window // dilation)
            dist = (steps * dilation).astype(F32)
            for h in range(nh):
                bias = jnp.where(valid, -(slopes[g * nh + h] * LOG2_E) * dist, MASKED_SCORE)
                bias_ref[g, h * BLK:(h + 1) * BLK, :] = bias
                bias0_ref[g, h * BLK:(h + 1) * BLK, :] = jnp.where(kj < BLK, MASKED_SCORE, bias)

    @pl.when(step == 1)
    def _():
        bias0_ref[...] = bias_ref[...]

    lane_head = lax.broadcasted_iota(jnp.int32, (BLK, ATT_MERGED), 1) // HEAD_DIM
    head_masks_bf16 = [(lane_head == h).astype(F32).astype(BF16) for h in range(nh)]
    low_half = lax.broadcasted_iota(jnp.int32, (BLK, LANES), 1) < HEAD_DIM
    attend = functools.partial(_attend_block, head_masks_bf16=head_masks_bf16, low_half=low_half)
    q_cols, k_cols, v_cols = (slice(c * ATT_MERGED, (c + 1) * ATT_MERGED) for c in range(3))

    sub_rows = ATT_TILE // MERGE_MOD

    def store(g, dilation, r, b, slabs):
        if dilation in (1, MERGE_MOD):
            rows = pl.ds(r * (ATT_TILE // dilation) + b * BLK, BLK)
        else:
            per = dilation // MERGE_MOD
            rows = pl.ds((r % MERGE_MOD) * sub_rows + b * BLK * per + r // MERGE_MOD, BLK,
                         stride=per)
        for sl, (o, m, l) in enumerate(slabs):
            o_scr[g, sl, rows, :] = o
            m_scr[g, sl, rows, :] = m
            l_scr[g, sl, rows, :] = l

    for g, (_, dilation) in enumerate(ATT_GROUPS):
        n_blocks = ATT_TILE // dilation // BLK
        z_ref, kprev_ref, vprev_ref = z_refs[g], kprev_refs[g], vprev_refs[g]

        def load(r, row0, n_rows, cols):
            return z_ref[row0:row0 + n_rows, cols] if dilation == 1 else \
                z_ref[r, row0:row0 + n_rows, cols]

        for r in range(dilation):
            kprev = kprev_ref[...] if dilation == 1 else kprev_ref[r]
            vprev = vprev_ref[...] if dilation == 1 else vprev_ref[r]
            kk = jnp.concatenate([kprev, load(r, 0, BLK, k_cols)], axis=0)
            vv = jnp.concatenate([vprev, load(r, 0, BLK, v_cols)], axis=0)
            store(g, dilation, r, 0, attend(load(r, 0, BLK, q_cols), kk, vv, bias0_ref[g]))
            for b in range(1, n_blocks):
                kk = load(r, (b - 1) * BLK, 2 * BLK, k_cols)
                vv = load(r, (b - 1) * BLK, 2 * BLK, v_cols)
                store(g, dilation, r, b, attend(load(r, b * BLK, BLK, q_cols), kk, vv, bias_ref[g]))

    chunks = sub_rows // BLK

    def merge_body(idx, _):
        c = idx // chunks
        row0 = (idx % chunks) * BLK
        nat_rows = pl.ds(row0 * MERGE_MOD + c, BLK, stride=MERGE_MOD)
        cls_rows = pl.ds(pl.multiple_of(c * sub_rows + row0, BLK), BLK)
        rows_of = lambda g: nat_rows if ATT_GROUPS[g][1] == 1 else cls_rows
        for sl in range(n_slabs):
            ms = [m_scr[g, sl, rows_of(g), :] for g in range(N_GROUPS)]
            mmax = functools.reduce(jnp.maximum, ms)
            ws = [jnp.exp2(m - mmax) for m in ms]
            num = sum(w * o_scr[g, sl, rows_of(g), :] for g, w in enumerate(ws))
            den = sum(w * l_scr[g, sl, rows_of(g), :] for g, w in enumerate(ws))
            nat_scr[sl, nat_rows, :] = num / den
        return 0

    lax.fori_loop(0, MERGE_MOD * chunks, merge_body, 0)
    for sl in range(n_slabs):
        a_ref[:, sl * LANES:(sl + 1) * LANES] = nat_scr[sl].astype(BF16)


def _attention(z_groups, slopes):
    s = z_groups[0].shape[0]
    cur_specs, prev_specs, prev_args = [], [], []
    for z, (_, d) in zip(z_groups, ATT_GROUPS):
        rows = ATT_TILE // d
        prev_row = lambda i, rows=rows: jnp.maximum(i * (rows // BLK) - 1, 0)
        for col in (1, 2):
            if d == 1:
                prev_specs.append(pl.BlockSpec((BLK, ATT_MERGED),
                                               lambda i, c=col, p=prev_row: (p(i), c)))
            else:
                prev_specs.append(pl.BlockSpec((d, BLK, ATT_MERGED),
                                               lambda i, c=col, p=prev_row: (0, p(i), c)))
            prev_args.append(z)
        if d == 1:
            cur_specs.append(pl.BlockSpec((rows, QKV_WIDTH), lambda i: (i, 0)))
        else:
            cur_specs.append(pl.BlockSpec((d, rows, QKV_WIDTH), lambda i: (0, i, 0)))
    scr_shape = (N_GROUPS, ATT_MERGED // LANES, ATT_TILE, LANES)
    bias_shape = (N_GROUPS, HEADS_PER_GROUP * BLK, 2 * BLK)
    return pl.pallas_call(
        functools.partial(_attn_kernel, slopes=slopes),
        out_shape=jax.ShapeDtypeStruct((s, ATT_MERGED), BF16),
        grid=(s // ATT_TILE,),
        in_specs=cur_specs + prev_specs,
        out_specs=pl.BlockSpec((ATT_TILE, ATT_MERGED), lambda i: (i, 0)),
        scratch_shapes=[pltpu.VMEM(bias_shape, F32), pltpu.VMEM(bias_shape, F32),
                        pltpu.VMEM(scr_shape, F32), pltpu.VMEM(scr_shape, F32),
                        pltpu.VMEM(scr_shape, F32), pltpu.VMEM(scr_shape[1:], F32)],
        compiler_params=pltpu.CompilerParams(
            dimension_semantics=("arbitrary",), vmem_limit_bytes=VMEM_LIMIT_BYTES),
        name="attn",
    )(*z_groups, *prev_args)


def _pool_weight_kernel(wgrp_ref, scale_ref, wpo_ref, wc_ref):
    c = POOL_GROUP_WIDTH
    for g in range(len(POOL_WINDOWS)):
        rows = slice(g * c, (g + 1) * c)
        wc = jnp.dot(wgrp_ref[g], scale_ref[rows, :] * wpo_ref[rows, :],
                     precision=lax.Precision.HIGHEST, preferred_element_type=F32)
        wc_ref[rows, :] = wc.astype(BF16)


def _pool_weight(w_grp, scale_col, wpo):
    return pl.pallas_call(
        _pool_weight_kernel,
        out_shape=jax.ShapeDtypeStruct((POOL_WIDTH, D_MODEL), BF16),
        compiler_params=pltpu.CompilerParams(vmem_limit_bytes=VMEM_LIMIT_BYTES),
        name="pool_weight",
    )(w_grp, scale_col, wpo)


def _post_kernel(a_ref, pz_ref, pzprev_ref, ga_ref, gp_ref, x_ref, wao_hbm, wc_ref, wout_hbm,
                 g_ref, w1_hbm, w2_hbm, gf_ref, o_ref,
                 wao_ref, wout_ref, w1_ref, w2_ref, stage_ref, sem_ref):
    i = pl.program_id(0)

    def weight_pieces():
        max_rows, width = stage_ref.shape[1:]

        def split(w, w_bf, rows, cols):
            step = min(max_rows, rows.stop - rows.start)
            return [(w.at[r:r + step, cols], w_bf.at[r:r + step, cols], None)
                    for r in range(rows.start, rows.stop, step)]

        assert D_MODEL == width and MLP_CHUNK % width == 0
        full = slice(0, D_MODEL)
        groups = [split(wao_hbm, wao_ref, slice(0, ATT_MERGED), full),
                  split(wout_hbm, wout_ref, full, full)]
        for c in range(D_FF // MLP_CHUNK):
            chunk = [slice(k, k + width) for k in range(c * MLP_CHUNK, (c + 1) * MLP_CHUNK, width)]
            groups.append([p for cols in chunk for p in split(w1_hbm, w1_ref, full, cols)])
            groups.append([p for rows in chunk for p in split(w2_hbm, w2_ref, rows, full)])
        return [p for group in groups for p in group], [len(group) for group in groups]

    def body(stream, uses):
        needed = [0]

        def dot(lhs, w_view):
            if stream is not None:
                needed[0] += uses.pop(0)
                stream.ensure(needed[0])
            return jnp.dot(lhs, w_view[...], preferred_element_type=F32)

        att = dot(a_ref[...], wao_ref)
        t = i * ROW_TILE + lax.broadcasted_iota(jnp.int32, (ROW_TILE, 1), 0)
        pooled = []
        for c in range(POOL_WIDTH // LANES):
            lanes = slice(c * LANES, (c + 1) * LANES)
            prev = jnp.where(i > 0, pzprev_ref[:, lanes].astype(F32), 0.0)
            pooled.append(
                _pooled_minus_token(pz_ref[:, lanes].astype(F32), prev, t, c).astype(BF16))
        pool = jnp.dot(jnp.concatenate(pooled, axis=1), wc_ref[...], preferred_element_type=F32)
        merged = (jax.nn.sigmoid(ga_ref[...].astype(F32)) * att
                  + jax.nn.sigmoid(gp_ref[...].astype(F32)) * pool)
        h = x_ref[...] + dot(merged.astype(BF16), wout_ref)

        m = (h * _rms_scale(h) * g_ref[...]).astype(BF16)
        y = h
        for c in range(D_FF // MLP_CHUNK):
            cols = slice(c * MLP_CHUNK, (c + 1) * MLP_CHUNK)
            hid = dot(m, w1_ref.at[:, cols])
            hid = jnp.square(jnp.maximum(hid, 0.0)).astype(BF16)
            y = y + dot(hid, w2_ref.at[cols, :])
        o_ref[...] = y * _rms_scale(y) * gf_ref[...]

    def first_step():
        pieces, uses = weight_pieces()
        body(_WeightStream(pieces, stage_ref, sem_ref), uses)

    pl.when(i == 0)(first_step)
    pl.when(i > 0)(lambda: body(None, None))


def _post(a, zn, zg, x2d, wao, wc, wout, g_mlp, w1, w2, g_final):
    s = x2d.shape[0]
    tm = ROW_TILE
    row = lambda i: (i, 0)
    const = lambda i: (0, 0)
    halo_blocks = tm // POOL_HALO
    assert QKV_WIDTH % POOL_WIDTH == 0
    pool_col = QKV_WIDTH // POOL_WIDTH
    return pl.pallas_call(
        _post_kernel,
        out_shape=jax.ShapeDtypeStruct((s, D_MODEL), F32),
        grid=(s // tm,),
        in_specs=[
            pl.BlockSpec((tm, ATT_MERGED), row),
            pl.BlockSpec((tm, POOL_WIDTH), lambda i: (i, pool_col)),
            pl.BlockSpec((POOL_HALO, POOL_WIDTH),
                         lambda i: (jnp.maximum(i * halo_blocks - 1, 0), pool_col)),
            pl.BlockSpec((tm, D_MODEL), lambda i: (i, 0)),
            pl.BlockSpec((tm, D_MODEL), lambda i: (i, 1)),
            pl.BlockSpec((tm, D_MODEL), row),
            HBM_SPEC,
            pl.BlockSpec((POOL_WIDTH, D_MODEL), const),
            HBM_SPEC,
            pl.BlockSpec((1, D_MODEL), const),
            HBM_SPEC,
            HBM_SPEC,
            pl.BlockSpec((1, D_MODEL), const),
        ],
        out_specs=pl.BlockSpec((tm, D_MODEL), row),
        scratch_shapes=[pltpu.VMEM((ATT_MERGED, D_MODEL), BF16),
                        pltpu.VMEM((D_MODEL, D_MODEL), BF16),
                        pltpu.VMEM((D_MODEL, D_FF), BF16),
                        pltpu.VMEM((D_FF, D_MODEL), BF16),
                        pltpu.VMEM((WEIGHT_STAGE_SLOTS, WEIGHT_STAGE_ROWS, D_MODEL), F32),
                        pltpu.SemaphoreType.DMA((WEIGHT_STAGE_SLOTS,))],
        compiler_params=pltpu.CompilerParams(
            dimension_semantics=("arbitrary",), vmem_limit_bytes=VMEM_LIMIT_BYTES),
        name="post",
    )(a, zn, zn, zg, zg, x2d, wao, wc, wout, g_mlp, w1, w2, g_final)


def _alibi_slopes():
    return tuple(2.0 ** (-ALIBI_MAX_BIAS * (h + 1.0) / N_ATT_HEADS) for h in range(N_ATT_HEADS))


def kernel(x, norm_mix_g, w_in, w_att_out, w_pool_grp, pool_scale, w_pool_out, w_out,
           norm_mlp_g, w_mlp_in, w_mlp_out, norm_final_g):
    batch, seq, d = x.shape
    assert d == D_MODEL and norm_mix_g.shape[0] == 1, "one layer of width D_MODEL"
    assert seq % ATT_TILE == 0 and seq % ROW_TILE == 0 and seq % IN_PROJ_ROWS == 0
    assert ATT_GROUPS[0][1] == 1 and all(d % MERGE_MOD == 0 for _, d in ATT_GROUPS[1:])
    slopes = _alibi_slopes()
    outs = []
    for b in range(batch):
        h = x[b]
        zn, zg, z1, z2 = _in_proj(h, norm_mix_g[0][None, :], w_in[0])
        a = _attention((zn, z1, z2), slopes)
        wc = _pool_weight(w_pool_grp[0], pool_scale[0][:, None], w_pool_out[0])
        outs.append(_post(a, zn, zg, h, w_att_out[0], wc, w_out[0], norm_mlp_g[0][None, :],
                          w_mlp_in[0], w_mlp_out[0], norm_final_g[None, :]))
    return outs[0][None] if batch == 1 else jnp.stack(outs, axis=0)
```

```python
import functools
import math

import jax
import jax.numpy as jnp
from jax import lax
from jax.experimental import pallas as pl
from jax.experimental.pallas import tpu as pltpu

D_MODEL = 1024
HEAD_DIM = 64
ATT_GROUPS = ((128, 1), (512, 4), (2048, 16))
N_GROUPS = len(ATT_GROUPS)
HEADS_PER_GROUP = 4
N_ATT_HEADS = HEADS_PER_GROUP * N_GROUPS
ATT_WIDTH = N_ATT_HEADS * HEAD_DIM
ATT_MERGED = HEADS_PER_GROUP * HEAD_DIM
QKV_WIDTH = 3 * ATT_MERGED
BLK = 128
POOL_WINDOWS = (2, 4, 8, 16)
POOL_GROUP_WIDTH = 3 * D_MODEL // 16
POOL_WIDTH = POOL_GROUP_WIDTH * len(POOL_WINDOWS)
D_FF = 4 * D_MODEL
N_IN = 3 * ATT_WIDTH + POOL_WIDTH + 2 * D_MODEL
NORM_EPS = 1e-6
ALIBI_MAX_BIAS = 8.0
POOL_HALO = max(POOL_WINDOWS)

VMEM_LIMIT_BYTES = 56 * 1024 * 1024
LANES = 128
MASKED_SCORE = -1e30

IN_PROJ_ROWS = 512
ROW_TILE = 512
MLP_CHUNK = 1024
WEIGHT_STAGE_ROWS = 512
WEIGHT_STAGE_SLOTS = 4
MAX_DILATION = max(d for _, d in ATT_GROUPS)
ATT_TILE = BLK * MAX_DILATION
STRIDE_STEP = 4
MERGE_MOD = 4
LOG2_E = math.log2(math.e)

ZN_WIDTH = QKV_WIDTH + POOL_WIDTH
ZG_WIDTH = 2 * D_MODEL
NAT_WIDTH = ZN_WIDTH + ZG_WIDTH

BF16 = jnp.bfloat16
F32 = jnp.float32


def _rms_scale(x):
    return lax.rsqrt(jnp.mean(x * x, axis=-1, keepdims=True) + NORM_EPS)


class _WeightStream:
    def __init__(self, pieces, stage_ref, sem_ref):
        self.pieces, self.stage_ref, self.sem_ref = pieces, stage_ref, sem_ref
        self.n_slots = stage_ref.shape[0]
        self.started = self.converted = 0

    def _slot(self, k):
        return self.stage_ref.at[k % self.n_slots, pl.ds(0, self.pieces[k][0].shape[0])]

    def _copy(self, k):
        return pltpu.make_async_copy(self.pieces[k][0], self._slot(k),
                                     self.sem_ref.at[k % self.n_slots])

    def _start_ahead(self):
        while self.started < min(len(self.pieces), self.converted + self.n_slots):
            self._copy(self.started).start()
            self.started += 1

    def ensure(self, count):
        self._start_ahead()
        while self.converted < count:
            k = self.converted
            _, dst, scale = self.pieces[k]
            self._copy(k).wait()
            piece = self._slot(k)[...]
            dst[...] = (piece if scale is None else piece * scale).astype(BF16)
            self.converted += 1
            self._start_ahead()


HBM_SPEC = pl.BlockSpec(memory_space=pl.ANY)


def _w_in_pieces(w_hbm, w_ref):
    pieces = []
    for j in range(N_IN // ATT_MERGED):
        which, g = divmod(j, N_GROUPS)
        if which < 3:
            dst = which * ATT_MERGED + (0 if g == 0 else NAT_WIDTH + (g - 1) * QKV_WIDTH)
        else:
            dst = QKV_WIDTH + (j - 3 * N_GROUPS) * ATT_MERGED
        scale = HEAD_DIM ** -0.5 * LOG2_E if which == 0 else None
        pieces.append((dst, w_hbm.at[:, j * ATT_MERGED:(j + 1) * ATT_MERGED],
                       w_ref.at[:, dst:dst + ATT_MERGED], scale))
    return [piece[1:] for piece in sorted(pieces, key=lambda piece: piece[0])]


def _pooled_minus_token(pf, prev, t, c):
    groups = [g for g in range(len(POOL_WINDOWS))
              if g * POOL_GROUP_WIDTH < (c + 1) * LANES and (g + 1) * POOL_GROUP_WIDTH > c * LANES]
    acc = jnp.concatenate([prev, pf], axis=0)
    w, means = 1, []
    for g in groups:
        while w < POOL_WINDOWS[g]:
            acc = acc + pltpu.roll(acc, w, axis=0)
            w *= 2
        count = jnp.minimum(t + 1, POOL_WINDOWS[g]).astype(F32)
        means.append(acc[POOL_HALO:, :] / count)
    pooled = means[-1]
    for g, mean in zip(groups[-2::-1], means[-2::-1]):
        lane = lax.broadcasted_iota(jnp.int32, mean.shape, 1)
        pooled = jnp.where(lane < (g + 1) * POOL_GROUP_WIDTH - c * LANES, mean, pooled)
    return pooled - pf


def _in_proj_kernel(x_ref, g_ref, w_hbm, zn_ref, zg_ref, *rest):
    dil_refs, (slab_ref, w_ref, stage_ref, sem_ref) = rest[:-4], rest[-4:]

    def body(stream):
        def project(lo, width):
            if stream is not None:
                stream.ensure((lo + width) // ATT_MERGED)
            return jnp.dot(u, w_ref[:, lo:lo + width], preferred_element_type=F32)

        x = x_ref[...]
        u = (x * _rms_scale(x) * g_ref[...]).astype(BF16)
        lo = 0
        for out_ref, width in ((zn_ref, QKV_WIDTH), (zn_ref, POOL_WIDTH),
                               (zg_ref, D_MODEL), (zg_ref, D_MODEL)):
            off = lo - (0 if out_ref is zn_ref else ZN_WIDTH)
            out_ref[:, off:off + width] = project(lo, width).astype(BF16)
            lo += width
        n_slabs = QKV_WIDTH // LANES
        for gi, z_ref in enumerate(dil_refs):
            dilation = ATT_GROUPS[gi + 1][1]
            acc = project(NAT_WIDTH + gi * QKV_WIDTH, QKV_WIDTH)
            for sl in range(n_slabs):
                slab_ref[0, sl] = acc[:, sl * LANES:(sl + 1) * LANES]
            src, n_rows, stride = 0, IN_PROJ_ROWS, dilation
            while stride > STRIDE_STEP:
                n_rows //= STRIDE_STEP
                for c in range(IN_PROJ_ROWS // n_rows):
                    base = (c // STRIDE_STEP) * n_rows * STRIDE_STEP + c % STRIDE_STEP
                    for sl in range(n_slabs):
                        slab_ref[1 - src, sl, c * n_rows:(c + 1) * n_rows, :] = \
                            slab_ref[src, sl, pl.ds(base, n_rows, stride=STRIDE_STEP), :]
                src, stride = 1 - src, stride // STRIDE_STEP
            outer = dilation // stride
            for r in range(dilation):
                lo_class, hi_class = r % outer, r // outer
                base = lo_class * (IN_PROJ_ROWS // outer) + hi_class
                for sl in range(n_slabs):
                    rows = slab_ref[src, sl, pl.ds(base, IN_PROJ_ROWS // dilation, stride=stride), :]
                    z_ref[r, :, sl * LANES:(sl + 1) * LANES] = rows.astype(BF16)

    first = pl.program_id(0) == 0
    pl.when(first)(lambda: body(_WeightStream(_w_in_pieces(w_hbm, w_ref), stage_ref, sem_ref)))
    pl.when(jnp.logical_not(first))(lambda: body(None))


def _in_proj(x2d, g, w_in):
    s = x2d.shape[0]
    dils = [d for _, d in ATT_GROUPS[1:]]
    out_shape = [jax.ShapeDtypeStruct((s, ZN_WIDTH), BF16), jax.ShapeDtypeStruct((s, ZG_WIDTH), BF16)]
    tm = IN_PROJ_ROWS
    out_specs = [pl.BlockSpec((tm, ZN_WIDTH), lambda i: (i, 0)),
                 pl.BlockSpec((tm, ZG_WIDTH), lambda i: (i, 0))]
    for d in dils:
        out_shape.append(jax.ShapeDtypeStruct((d, s // d, QKV_WIDTH), BF16))
        out_specs.append(pl.BlockSpec((d, tm // d, QKV_WIDTH), lambda i: (0, i, 0)))
    return pl.pallas_call(
        _in_proj_kernel,
        out_shape=out_shape,
        grid=(s // tm,),
        in_specs=[
            pl.BlockSpec((tm, D_MODEL), lambda i: (i, 0)),
            pl.BlockSpec((1, D_MODEL), lambda i: (0, 0)),
            HBM_SPEC,
        ],
        out_specs=out_specs,
        scratch_shapes=[pltpu.VMEM((2, QKV_WIDTH // LANES, tm, LANES), F32),
                        pltpu.VMEM((D_MODEL, N_IN), BF16),
                        pltpu.VMEM((WEIGHT_STAGE_SLOTS, D_MODEL, ATT_MERGED), F32),
                        pltpu.SemaphoreType.DMA((WEIGHT_STAGE_SLOTS,))],
        compiler_params=pltpu.CompilerParams(
            dimension_semantics=("arbitrary",), vmem_limit_bytes=VMEM_LIMIT_BYTES),
        name="in_proj",
    )(x2d, g, w_in)


def _attend_block(qb, kk, vv, bias, head_masks_bf16, low_half):
    nh = HEADS_PER_GROUP
    q_stack = jnp.concatenate([qb * head_masks_bf16[h] for h in range(nh)], axis=0)
    s = lax.dot_general(q_stack, kk, (((1,), (1,)), ((), ())), preferred_element_type=F32)
    s = s + bias
    m = jnp.max(s, axis=-1, keepdims=True)
    p = jnp.exp2(s - m)
    l = jnp.sum(p, axis=-1, keepdims=True)
    pv = jnp.dot(p.astype(BF16), vv, preferred_element_type=F32)
    slabs = []
    for sl in range(ATT_MERGED // LANES):
        lanes = slice(sl * LANES, (sl + 1) * LANES)
        rows_a = slice(2 * sl * BLK, (2 * sl + 1) * BLK)
        rows_b = slice((2 * sl + 1) * BLK, (2 * sl + 2) * BLK)
        slabs.append((jnp.where(low_half, pv[rows_a, lanes], pv[rows_b, lanes]),
                      jnp.where(low_half, m[rows_a], m[rows_b]),
                      jnp.where(low_half, l[rows_a], l[rows_b])))
    return slabs


def _attn_kernel(*refs, slopes):
    z_refs = refs[0:N_GROUPS]
    kprev_refs = refs[N_GROUPS:3 * N_GROUPS:2]
    vprev_refs = refs[N_GROUPS + 1:3 * N_GROUPS:2]
    a_ref, bias_ref, bias0_ref, o_scr, m_scr, l_scr, nat_scr = refs[3 * N_GROUPS:]
    step = pl.program_id(0)
    nh = HEADS_PER_GROUP
    n_slabs = ATT_MERGED // LANES

    @pl.when(step == 0)
    def _():
        qi = lax.broadcasted_iota(jnp.int32, (BLK, 2 * BLK), 0)
        kj = lax.broadcasted_iota(jnp.int32, (BLK, 2 * BLK), 1)
        steps = BLK + qi - kj
        for g, (window, dilation) in enumerate(ATT_GROUPS):
            valid = (steps >= 0) & (steps <= window // dilation)
            dist = (steps * dilation).astype(F32)
            for h in range(nh):
                bias = jnp.where(valid, -(slopes[g * nh + h] * LOG2_E) * dist, MASKED_SCORE)
                bias_ref[g, h * BLK:(h + 1) * BLK, :] = bias
                bias0_ref[g, h * BLK:(h + 1) * BLK, :] = jnp.where(kj < BLK, MASKED_SCORE, bias)

    @pl.when(step == 1)
    def _():
        bias0_ref[...] = bias_ref[...]

    lane_head = lax.broadcasted_iota(jnp.int32, (BLK, ATT_MERGED), 1) // HEAD_DIM
    head_masks_bf16 = [(lane_head == h).astype(F32).astype(BF16) for h in range(nh)]
    low_half = lax.broadcasted_iota(jnp.int32, (BLK, LANES), 1) < HEAD_DIM
    attend = functools.partial(_attend_block, head_masks_bf16=head_masks_bf16, low_half=low_half)
    q_cols, k_cols, v_cols = (slice(c * ATT_MERGED, (c + 1) * ATT_MERGED) for c in range(3))

    sub_rows = ATT_TILE // MERGE_MOD

    def store(g, dilation, r, b, slabs):
        if dilation in (1, MERGE_MOD):
            rows = pl.ds(r * (ATT_TILE // dilation) + b * BLK, BLK)
        else:
            per = dilation // MERGE_MOD
            rows = pl.ds((r % MERGE_MOD) * sub_rows + b * BLK * per + r // MERGE_MOD, BLK,
                         stride=per)
        for sl, (o, m, l) in enumerate(slabs):
            o_scr[g, sl, rows, :] = o
            m_scr[g, sl, rows, :] = m
            l_scr[g, sl, rows, :] = l

    for g, (_, dilation) in enumerate(ATT_GROUPS):
        n_blocks = ATT_TILE // dilation // BLK
        z_ref, kprev_ref, vprev_ref = z_refs[g], kprev_refs[g], vprev_refs[g]

        def load(r, row0, n_rows, cols):
            return z_ref[row0:row0 + n_rows, cols] if dilation == 1 else \
                z_ref[r, row0:row0 + n_rows, cols]

        for r in range(dilation):
            kprev = kprev_ref[...] if dilation == 1 else kprev_ref[r]
            vprev = vprev_ref[...] if dilation == 1 else vprev_ref[r]
            kk = jnp.concatenate([kprev, load(r, 0, BLK, k_cols)], axis=0)
            vv = jnp.concatenate([vprev, load(r, 0, BLK, v_cols)], axis=0)
            store(g, dilation, r, 0, attend(load(r, 0, BLK, q_cols), kk, vv, bias0_ref[g]))
            for b in range(1, n_blocks):
                kk = load(r, (b - 1) * BLK, 2 * BLK, k_cols)
                vv = load(r, (b - 1) * BLK, 2 * BLK, v_cols)
                store(g, dilation, r, b, attend(load(r, b * BLK, BLK, q_cols), kk, vv, bias_ref[g]))

    chunks = sub_rows // BLK

    def merge_body(idx, _):
        c = idx // chunks
        row0 = (idx % chunks) * BLK
        nat_rows = pl.ds(row0 * MERGE_MOD + c, BLK, stride=MERGE_MOD)
        cls_rows = pl.ds(pl.multiple_of(c * sub_rows + row0, BLK), BLK)
        rows_of = lambda g: nat_rows if ATT_GROUPS[g][1] == 1 else cls_rows
        for sl in range(n_slabs):
            ms = [m_scr[g, sl, rows_of(g), :] for g in range(N_GROUPS)]
            mmax = functools.reduce(jnp.maximum, ms)
            ws = [jnp.exp2(m - mmax) for m in ms]
            num = sum(w * o_scr[g, sl, rows_of(g), :] for g, w in enumerate(ws))
            den = sum(w * l_scr[g, sl, rows_of(g), :] for g, w in enumerate(ws))
            nat_scr[sl, nat_rows, :] = num / den
        return 0

    lax.fori_loop(0, MERGE_MOD * chunks, merge_body, 0)
    for sl in range(n_slabs):
        a_ref[:, sl * LANES:(sl + 1) * LANES] = nat_scr[sl].astype(BF16)


def _attention(z_groups, slopes):
    s = z_groups[0].shape[0]
    cur_specs, prev_specs, prev_args = [], [], []
    for z, (_, d) in zip(z_groups, ATT_GROUPS):
        rows = ATT_TILE // d
        prev_row = lambda i, rows=rows: jnp.maximum(i * (rows // BLK) - 1, 0)
        for col in (1, 2):
            if d == 1:
                prev_specs.append(pl.BlockSpec((BLK, ATT_MERGED),
                                               lambda i, c=col, p=prev_row: (p(i), c)))
            else:
                prev_specs.append(pl.BlockSpec((d, BLK, ATT_MERGED),
                                               lambda i, c=col, p=prev_row: (0, p(i), c)))
            prev_args.append(z)
        if d == 1:
            cur_specs.append(pl.BlockSpec((rows, QKV_WIDTH), lambda i: (i, 0)))
        else:
            cur_specs.append(pl.BlockSpec((d, rows, QKV_WIDTH), lambda i: (0, i, 0)))
    scr_shape = (N_GROUPS, ATT_MERGED // LANES, ATT_TILE, LANES)
    bias_shape = (N_GROUPS, HEADS_PER_GROUP * BLK, 2 * BLK)
    return pl.pallas_call(
        functools.partial(_attn_kernel, slopes=slopes),
        out_shape=jax.ShapeDtypeStruct((s, ATT_MERGED), BF16),
        grid=(s // ATT_TILE,),
        in_specs=cur_specs + prev_specs,
        out_specs=pl.BlockSpec((ATT_TILE, ATT_MERGED), lambda i: (i, 0)),
        scratch_shapes=[pltpu.VMEM(bias_shape, F32), pltpu.VMEM(bias_shape, F32),
                        pltpu.VMEM(scr_shape, F32), pltpu.VMEM(scr_shape, F32),
                        pltpu.VMEM(scr_shape, F32), pltpu.VMEM(scr_shape[1:], F32)],
        compiler_params=pltpu.CompilerParams(
            dimension_semantics=("arbitrary",), vmem_limit_bytes=VMEM_LIMIT_BYTES),
        name="attn",
    )(*z_groups, *prev_args)


def _pool_weight_kernel(wgrp_ref, scale_ref, wpo_ref, wc_ref):
    def split(w):
        hi = w.astype(BF16)
        return hi, (w - hi.astype(F32)).astype(BF16)

    c = POOL_GROUP_WIDTH
    for g in range(len(POOL_WINDOWS)):
        rows = slice(g * c, (g + 1) * c)
        a_hi, a_lo = split(wgrp_ref[g])
        b_hi, b_lo = split(scale_ref[rows, :] * wpo_ref[rows, :])
        dot = functools.partial(jnp.dot, preferred_element_type=F32)
        wc_ref[rows, :] = (dot(a_hi, b_hi) + (dot(a_hi, b_lo) + dot(a_lo, b_hi))).astype(BF16)


def _pool_weight(w_grp, scale_col, wpo):
    return pl.pallas_call(
        _pool_weight_kernel,
        out_shape=jax.ShapeDtypeStruct((POOL_WIDTH, D_MODEL), BF16),
        compiler_params=pltpu.CompilerParams(vmem_limit_bytes=VMEM_LIMIT_BYTES),
        name="pool_weight",
    )(w_grp, scale_col, wpo)


def _post_kernel(a_ref, pz_ref, pzprev_ref, ga_ref, gp_ref, x_ref, wao_hbm, wc_ref, wout_hbm,
                 g_ref, w1_hbm, w2_hbm, gf_ref, o_ref,
                 wao_ref, wout_ref, w1_ref, w2_ref, stage_ref, sem_ref):
    i = pl.program_id(0)

    def weight_pieces():
        max_rows, width = stage_ref.shape[1:]

        def split(w, w_bf, rows, cols):
            step = min(max_rows, rows.stop - rows.start)
            return [(w.at[r:r + step, cols], w_bf.at[r:r + step, cols], None)
                    for r in range(rows.start, rows.stop, step)]

        assert D_MODEL == width and MLP_CHUNK % width == 0
        full = slice(0, D_MODEL)
        groups = [split(wao_hbm, wao_ref, slice(0, ATT_MERGED), full),
                  split(wout_hbm, wout_ref, full, full)]
        for c in range(D_FF // MLP_CHUNK):
            chunk = [slice(k, k + width) for k in range(c * MLP_CHUNK, (c + 1) * MLP_CHUNK, width)]
            groups.append([p for cols in chunk for p in split(w1_hbm, w1_ref, full, cols)])
            groups.append([p for rows in chunk for p in split(w2_hbm, w2_ref, rows, full)])
        return [p for group in groups for p in group], [len(group) for group in groups]

    def body(stream, uses):
        needed = [0]

        def dot(lhs, w_view):
            if stream is not None:
                needed[0] += uses.pop(0)
                stream.ensure(needed[0])
            return jnp.dot(lhs, w_view[...], preferred_element_type=F32)

        att = dot(a_ref[...], wao_ref)
        t = i * ROW_TILE + lax.broadcasted_iota(jnp.int32, (ROW_TILE, 1), 0)
        pooled = []
        for c in range(POOL_WIDTH // LANES):
            lanes = slice(c * LANES, (c + 1) * LANES)
            prev = jnp.where(i > 0, pzprev_ref[:, lanes].astype(F32), 0.0)
            pooled.append(
                _pooled_minus_token(pz_ref[:, lanes].astype(F32), prev, t, c).astype(BF16))
        pool = jnp.dot(jnp.concatenate(pooled, axis=1), wc_ref[...], preferred_element_type=F32)
        merged = (jax.nn.sigmoid(ga_ref[...].astype(F32)) * att
                  + jax.nn.sigmoid(gp_ref[...].astype(F32)) * pool)
        h = x_ref[...] + dot(merged.astype(BF16), wout_ref)

        m = (h * _rms_scale(h) * g_ref[...]).astype(BF16)
        y = h
        for c in range(D_FF // MLP_CHUNK):
            cols = slice(c * MLP_CHUNK, (c + 1) * MLP_CHUNK)
            hid = dot(m, w1_ref.at[:, cols])
            hid = jnp.square(jnp.maximum(hid, 0.0)).astype(BF16)
            y = y + dot(hid, w2_ref.at[cols, :])
        o_ref[...] = y * _rms_scale(y) * gf_ref[...]

    def first_step():
        pieces, uses = weight_pieces()
        body(_WeightStream(pieces, stage_ref, sem_ref), uses)

    pl.when(i == 0)(first_step)
    pl.when(i > 0)(lambda: body(None, None))


def _post(a, zn, zg, x2d, wao, wc, wout, g_mlp, w1, w2, g_final):
    s = x2d.shape[0]
    tm = ROW_TILE
    row = lambda i: (i, 0)
    const = lambda i: (0, 0)
    halo_blocks = tm // POOL_HALO
    assert QKV_WIDTH % POOL_WIDTH == 0
    pool_col = QKV_WIDTH // POOL_WIDTH
    return pl.pallas_call(
        _post_kernel,
        out_shape=jax.ShapeDtypeStruct((s, D_MODEL), F32),
        grid=(s // tm,),
        in_specs=[
            pl.BlockSpec((tm, ATT_MERGED), row),
            pl.BlockSpec((tm, POOL_WIDTH), lambda i: (i, pool_col)),
            pl.BlockSpec((POOL_HALO, POOL_WIDTH),
                         lambda i: (jnp.maximum(i * halo_blocks - 1, 0), pool_col)),
            pl.BlockSpec((tm, D_MODEL), lambda i: (i, 0)),
            pl.BlockSpec((tm, D_MODEL), lambda i: (i, 1)),
            pl.BlockSpec((tm, D_MODEL), row),
            HBM_SPEC,
            pl.BlockSpec((POOL_WIDTH, D_MODEL), const),
            HBM_SPEC,
            pl.BlockSpec((1, D_MODEL), const),
            HBM_SPEC,
            HBM_SPEC,
            pl.BlockSpec((1, D_MODEL), const),
        ],
        out_specs=pl.BlockSpec((tm, D_MODEL), row),
        scratch_shapes=[pltpu.VMEM((ATT_MERGED, D_MODEL), BF16),
                        pltpu.VMEM((D_MODEL, D_MODEL), BF16),
                        pltpu.VMEM((D_MODEL, D_FF), BF16),
                        pltpu.VMEM((D_FF, D_MODEL), BF16),
                        pltpu.VMEM((WEIGHT_STAGE_SLOTS, WEIGHT_STAGE_ROWS, D_MODEL), F32),
                        pltpu.SemaphoreType.DMA((WEIGHT_STAGE_SLOTS,))],
        compiler_params=pltpu.CompilerParams(
            dimension_semantics=("arbitrary",), vmem_limit_bytes=VMEM_LIMIT_BYTES),
        name="post",
    )(a, zn, zn, zg, zg, x2d, wao, wc, wout, g_mlp, w1, w2, g_final)


def _alibi_slopes():
    return tuple(2.0 ** (-ALIBI_MAX_BIAS * (h + 1.0) / N_ATT_HEADS) for h in range(N_ATT_HEADS))


def kernel(x, norm_mix_g, w_in, w_att_out, w_pool_grp, pool_scale, w_pool_out, w_out,
           norm_mlp_g, w_mlp_in, w_mlp_out, norm_final_g):
    batch, seq, d = x.shape
    assert d == D_MODEL and norm_mix_g.shape[0] == 1, "one layer of width D_MODEL"
    assert seq % ATT_TILE == 0 and seq % ROW_TILE == 0 and seq % IN_PROJ_ROWS == 0
    assert ATT_GROUPS[0][1] == 1 and all(d % MERGE_MOD == 0 for _, d in ATT_GROUPS[1:])
    slopes = _alibi_slopes()
    outs = []
    for b in range(batch):
        h = x[b]
        zn, zg, z1, z2 = _in_proj(h, norm_mix_g[0][None, :], w_in[0])
        a = _attention((zn, z1, z2), slopes)
        wc = _pool_weight(w_pool_grp[0], pool_scale[0][:, None], w_pool_out[0])
        outs.append(_post(a, zn, zg, h, w_att_out[0], wc, w_out[0], norm_mlp_g[0][None, :],
                          w_mlp_in[0], w_mlp_out[0], norm_final_g[None, :]))
    return outs[0][None] if batch == 1 else jnp.stack(outs, axis=0)
```

```python
import functools
import math

import jax
import jax.numpy as jnp
from jax import lax
from jax.experimental import pallas as pl
from jax.experimental.pallas import tpu as pltpu

D_MODEL = 1024
HEAD_DIM = 64
ATT_GROUPS = ((128, 1), (512, 4), (2048, 16))
N_GROUPS = len(ATT_GROUPS)
HEADS_PER_GROUP = 4
N_ATT_HEADS = HEADS_PER_GROUP * N_GROUPS
ATT_WIDTH = N_ATT_HEADS * HEAD_DIM
ATT_MERGED = HEADS_PER_GROUP * HEAD_DIM
QKV_WIDTH = 3 * ATT_MERGED
BLK = 128
POOL_WINDOWS = (2, 4, 8, 16)
POOL_GROUP_WIDTH = 3 * D_MODEL // 16
POOL_WIDTH = POOL_GROUP_WIDTH * len(POOL_WINDOWS)
D_FF = 4 * D_MODEL
N_IN = 3 * ATT_WIDTH + POOL_WIDTH + 2 * D_MODEL
NORM_EPS = 1e-6
ALIBI_MAX_BIAS = 8.0
POOL_HALO = max(POOL_WINDOWS)

VMEM_LIMIT_BYTES = 56 * 1024 * 1024
LANES = 128
MASKED_SCORE = -1e30

IN_PROJ_ROWS = 512
ROW_TILE = 512
MLP_CHUNK = 1024
WEIGHT_STAGE_ROWS = 512
WEIGHT_STAGE_SLOTS = 4
MAX_DILATION = max(d for _, d in ATT_GROUPS)
ATT_TILE = BLK * MAX_DILATION
STRIDE_STEP = 4
MERGE_MOD = 4
LOG2_E = math.log2(math.e)

ZN_WIDTH = QKV_WIDTH + POOL_WIDTH
ZG_WIDTH = 2 * D_MODEL
NAT_WIDTH = ZN_WIDTH + ZG_WIDTH

BF16 = jnp.bfloat16
F32 = jnp.float32


def _rms_scale(x):
    return lax.rsqrt(jnp.mean(x * x, axis=-1, keepdims=True) + NORM_EPS)


class _WeightStream:
    def __init__(self, pieces, stage_ref, sem_ref):
        self.pieces, self.stage_ref, self.sem_ref = pieces, stage_ref, sem_ref
        self.n_slots = stage_ref.shape[0]
        self.started = self.converted = 0

    def _slot(self, k):
        return self.stage_ref.at[k % self.n_slots, pl.ds(0, self.pieces[k][0].shape[0])]

    def _copy(self, k):
        return pltpu.make_async_copy(self.pieces[k][0], self._slot(k),
                                     self.sem_ref.at[k % self.n_slots])

    def _start_ahead(self):
        while self.started < min(len(self.pieces), self.converted + self.n_slots):
            self._copy(self.started).start()
            self.started += 1

    def ensure(self, count):
        self._start_ahead()
        while self.converted < count:
            k = self.converted
            _, dst, scale = self.pieces[k]
            self._copy(k).wait()
            piece = self._slot(k)[...]
            dst[...] = (piece if scale is None else piece * scale).astype(BF16)
            self.converted += 1
            self._start_ahead()


HBM_SPEC = pl.BlockSpec(memory_space=pl.ANY)


def _w_in_pieces(w_hbm, w_ref):
    pieces = []
    for j in range(N_IN // ATT_MERGED):
        which, g = divmod(j, N_GROUPS)
        if which < 3:
            dst = which * ATT_MERGED + (0 if g == 0 else NAT_WIDTH + (g - 1) * QKV_WIDTH)
        else:
            dst = QKV_WIDTH + (j - 3 * N_GROUPS) * ATT_MERGED
        scale = HEAD_DIM ** -0.5 * LOG2_E if which == 0 else None
        pieces.append((dst, w_hbm.at[:, j * ATT_MERGED:(j + 1) * ATT_MERGED],
                       w_ref.at[:, dst:dst + ATT_MERGED], scale))
    return [piece[1:] for piece in sorted(pieces, key=lambda piece: piece[0])]


def _pooled_minus_token(pf, prev, t, c):
    groups = [g for g in range(len(POOL_WINDOWS))
              if g * POOL_GROUP_WIDTH < (c + 1) * LANES and (g + 1) * POOL_GROUP_WIDTH > c * LANES]
    acc = jnp.concatenate([prev, pf], axis=0)
    w, means = 1, []
    for g in groups:
        while w < POOL_WINDOWS[g]:
            acc = acc + pltpu.roll(acc, w, axis=0)
            w *= 2
        count = jnp.minimum(t + 1, POOL_WINDOWS[g]).astype(F32)
        means.append(acc[POOL_HALO:, :] / count)
    pooled = means[-1]
    for g, mean in zip(groups[-2::-1], means[-2::-1]):
        lane = lax.broadcasted_iota(jnp.int32, mean.shape, 1)
        pooled = jnp.where(lane < (g + 1) * POOL_GROUP_WIDTH - c * LANES, mean, pooled)
    return pooled - pf


def _in_proj_kernel(x_ref, g_ref, w_hbm, zn_ref, zg_ref, *rest):
    dil_refs, (slab_ref, w_ref, stage_ref, sem_ref) = rest[:-4], rest[-4:]

    def body(stream):
        def project(lo, width):
            if stream is not None:
                stream.ensure((lo + width) // ATT_MERGED)
            return jnp.dot(u, w_ref[:, lo:lo + width], preferred_element_type=F32) * scale

        x = x_ref[...]
        scale = _rms_scale(x)
        u = (x * g_ref[...]).astype(BF16)
        def natural(out_ref, off, lo, width):
            out_ref[:, off:off + width] = project(lo, width).astype(BF16)

        def dilated(gi):
            z_ref, dilation = dil_refs[gi], ATT_GROUPS[gi + 1][1]
            slabs = slab_ref.at[gi]
            n_slabs = QKV_WIDTH // LANES
            acc = project(NAT_WIDTH + gi * QKV_WIDTH, QKV_WIDTH)
            for sl in range(n_slabs):
                slabs[0, sl] = acc[:, sl * LANES:(sl + 1) * LANES]
            src, n_rows, stride = 0, IN_PROJ_ROWS, dilation
            while stride > STRIDE_STEP:
                n_rows //= STRIDE_STEP
                for c in range(IN_PROJ_ROWS // n_rows):
                    base = (c // STRIDE_STEP) * n_rows * STRIDE_STEP + c % STRIDE_STEP
                    for sl in range(n_slabs):
                        slabs[1 - src, sl, c * n_rows:(c + 1) * n_rows, :] = \
                            slabs[src, sl, pl.ds(base, n_rows, stride=STRIDE_STEP), :]
                src, stride = 1 - src, stride // STRIDE_STEP
            outer = dilation // stride
            for r in range(dilation):
                lo_class, hi_class = r % outer, r // outer
                base = lo_class * (IN_PROJ_ROWS // outer) + hi_class
                for sl in range(n_slabs):
                    rows = slabs[src, sl, pl.ds(base, IN_PROJ_ROWS // dilation, stride=stride), :]
                    z_ref[r, :, sl * LANES:(sl + 1) * LANES] = rows.astype(BF16)

        natural(zn_ref, 0, 0, QKV_WIDTH)
        for gi in reversed(range(len(dil_refs))):
            dilated(gi)
            if gi == len(dil_refs) - 1:
                natural(zn_ref, QKV_WIDTH, QKV_WIDTH, POOL_WIDTH)
        for c in range(ZG_WIDTH // D_MODEL):
            natural(zg_ref, c * D_MODEL, ZN_WIDTH + c * D_MODEL, D_MODEL)

    first = pl.program_id(0) == 0
    pl.when(first)(lambda: body(_WeightStream(_w_in_pieces(w_hbm, w_ref), stage_ref, sem_ref)))
    pl.when(jnp.logical_not(first))(lambda: body(None))


def _in_proj(x2d, g, w_in):
    s = x2d.shape[0]
    dils = [d for _, d in ATT_GROUPS[1:]]
    out_shape = [jax.ShapeDtypeStruct((s, ZN_WIDTH), BF16), jax.ShapeDtypeStruct((s, ZG_WIDTH), BF16)]
    tm = IN_PROJ_ROWS
    out_specs = [pl.BlockSpec((tm, ZN_WIDTH), lambda i: (i, 0)),
                 pl.BlockSpec((tm, ZG_WIDTH), lambda i: (i, 0))]
    for d in dils:
        out_shape.append(jax.ShapeDtypeStruct((d, s // d, QKV_WIDTH), BF16))
        out_specs.append(pl.BlockSpec((d, tm // d, QKV_WIDTH), lambda i: (0, i, 0)))
    return pl.pallas_call(
        _in_proj_kernel,
        out_shape=out_shape,
        grid=(s // tm,),
        in_specs=[
            pl.BlockSpec((tm, D_MODEL), lambda i: (i, 0)),
            pl.BlockSpec((1, D_MODEL), lambda i: (0, 0)),
            HBM_SPEC,
        ],
        out_specs=out_specs,
        scratch_shapes=[pltpu.VMEM((len(dils), 2, QKV_WIDTH // LANES, tm, LANES), F32),
                        pltpu.VMEM((D_MODEL, N_IN), BF16),
                        pltpu.VMEM((WEIGHT_STAGE_SLOTS, D_MODEL, ATT_MERGED), F32),
                        pltpu.SemaphoreType.DMA((WEIGHT_STAGE_SLOTS,))],
        compiler_params=pltpu.CompilerParams(
            dimension_semantics=("arbitrary",), vmem_limit_bytes=VMEM_LIMIT_BYTES),
        name="in_proj",
    )(x2d, g, w_in)


def _attend_block(qb, kk, vv, bias, head_masks_bf16, low_half):
    nh = HEADS_PER_GROUP
    q_stack = jnp.concatenate([qb * head_masks_bf16[h] for h in range(nh)], axis=0)
    s = lax.dot_general(q_stack, kk, (((1,), (1,)), ((), ())), preferred_element_type=F32)
    s = s + bias
    m = jnp.max(s, axis=-1, keepdims=True)
    p = jnp.exp2(s - m)
    l = jnp.sum(p, axis=-1, keepdims=True)
    pv = jnp.dot(p.astype(BF16), vv, preferred_element_type=F32)
    slabs = []
    for sl in range(ATT_MERGED // LANES):
        lanes = slice(sl * LANES, (sl + 1) * LANES)
        rows_a = slice(2 * sl * BLK, (2 * sl + 1) * BLK)
        rows_b = slice((2 * sl + 1) * BLK, (2 * sl + 2) * BLK)
        slabs.append((jnp.where(low_half, pv[rows_a, lanes], pv[rows_b, lanes]),
                      jnp.where(low_half, m[rows_a], m[rows_b]),
                      jnp.where(low_half, l[rows_a], l[rows_b])))
    return slabs


def _attn_kernel(*refs, slopes):
    z_refs = refs[0:N_GROUPS]
    kprev_refs = refs[N_GROUPS:3 * N_GROUPS:2]
    vprev_refs = refs[N_GROUPS + 1:3 * N_GROUPS:2]
    a_ref, bias_ref, bias0_ref, o_scr, m_scr, l_scr, nat_scr = refs[3 * N_GROUPS:]
    step = pl.program_id(0)
    nh = HEADS_PER_GROUP
    n_slabs = ATT_MERGED // LANES

    @pl.when(step == 0)
    def _():
        qi = lax.broadcasted_iota(jnp.int32, (BLK, 2 * BLK), 0)
        kj = lax.broadcasted_iota(jnp.int32, (BLK, 2 * BLK), 1)
        steps = BLK + qi - kj
        for g, (window, dilation) in enumerate(ATT_GROUPS):
            valid = (steps >= 0) & (steps <= window // dilation)
            dist = (steps * dilation).astype(F32)
            for h in range(nh):
                bias = jnp.where(valid, -(slopes[g * nh + h] * LOG2_E) * dist, MASKED_SCORE)
                bias_ref[g, h * BLK:(h + 1) * BLK, :] = bias
                bias0_ref[g, h * BLK:(h + 1) * BLK, :] = jnp.where(kj < BLK, MASKED_SCORE, bias)

    @pl.when(step == 1)
    def _():
        bias0_ref[...] = bias_ref[...]

    lane_head = lax.broadcasted_iota(jnp.int32, (BLK, ATT_MERGED), 1) // HEAD_DIM
    head_masks_bf16 = [(lane_head == h).astype(F32).astype(BF16) for h in range(nh)]
    low_half = lax.broadcasted_iota(jnp.int32, (BLK, LANES), 1) < HEAD_DIM
    attend = functools.partial(_attend_block, head_masks_bf16=head_masks_bf16, low_half=low_half)
    q_cols, k_cols, v_cols = (slice(c * ATT_MERGED, (c + 1) * ATT_MERGED) for c in range(3))

    sub_rows = ATT_TILE // MERGE_MOD

    def store(g, dilation, r, b, slabs):
        if dilation in (1, MERGE_MOD):
            rows = pl.ds(r * (ATT_TILE // dilation) + b * BLK, BLK)
        else:
            per = dilation // MERGE_MOD
            rows = pl.ds((r % MERGE_MOD) * sub_rows + b * BLK * per + r // MERGE_MOD, BLK,
                         stride=per)
        for sl, (o, m, l) in enumerate(slabs):
            o_scr[g, sl, rows, :] = o
            m_scr[g, sl, rows, :] = m
            l_scr[g, sl, rows, :] = l

    for g, (_, dilation) in enumerate(ATT_GROUPS):
        n_blocks = ATT_TILE // dilation // BLK
        z_ref, kprev_ref, vprev_ref = z_refs[g], kprev_refs[g], vprev_refs[g]

        def load(r, row0, n_rows, cols):
            return z_ref[row0:row0 + n_rows, cols] if dilation == 1 else \
                z_ref[r, row0:row0 + n_rows, cols]

        for r in range(dilation):
            kprev = kprev_ref[...] if dilation == 1 else kprev_ref[r]
            vprev = vprev_ref[...] if dilation == 1 else vprev_ref[r]
            kk = jnp.concatenate([kprev, load(r, 0, BLK, k_cols)], axis=0)
            vv = jnp.concatenate([vprev, load(r, 0, BLK, v_cols)], axis=0)
            store(g, dilation, r, 0, attend(load(r, 0, BLK, q_cols), kk, vv, bias0_ref[g]))
            for b in range(1, n_blocks):
                kk = load(r, (b - 1) * BLK, 2 * BLK, k_cols)
                vv = load(r, (b - 1) * BLK, 2 * BLK, v_cols)
                store(g, dilation, r, b, attend(load(r, b * BLK, BLK, q_cols), kk, vv, bias_ref[g]))

    chunks = sub_rows // BLK

    def merge_body(idx, _):
        c = idx // chunks
        row0 = (idx % chunks) * BLK
        nat_rows = pl.ds(row0 * MERGE_MOD + c, BLK, stride=MERGE_MOD)
        cls_rows = pl.ds(pl.multiple_of(c * sub_rows + row0, BLK), BLK)
        rows_of = lambda g: nat_rows if ATT_GROUPS[g][1] == 1 else cls_rows
        for sl in range(n_slabs):
            ms = [m_scr[g, sl, rows_of(g), :] for g in range(N_GROUPS)]
            mmax = functools.reduce(jnp.maximum, ms)
            ws = [jnp.exp2(m - mmax) for m in ms]
            num = sum(w * o_scr[g, sl, rows_of(g), :] for g, w in enumerate(ws))
            den = sum(w * l_scr[g, sl, rows_of(g), :] for g, w in enumerate(ws))
            nat_scr[sl, nat_rows, :] = num / den
        return 0

    lax.fori_loop(0, MERGE_MOD * chunks, merge_body, 0)
    for sl in range(n_slabs):
        a_ref[:, sl * LANES:(sl + 1) * LANES] = nat_scr[sl].astype(BF16)


def _attention(z_groups, slopes):
    s = z_groups[0].shape[0]
    cur_specs, prev_specs, prev_args = [], [], []
    for z, (_, d) in zip(z_groups, ATT_GROUPS):
        rows = ATT_TILE // d
        prev_row = lambda i, rows=rows: jnp.maximum(i * (rows // BLK) - 1, 0)
        for col in (1, 2):
            if d == 1:
                prev_specs.append(pl.BlockSpec((BLK, ATT_MERGED),
                                               lambda i, c=col, p=prev_row: (p(i), c)))
            else:
                prev_specs.append(pl.BlockSpec((d, BLK, ATT_MERGED),
                                               lambda i, c=col, p=prev_row: (0, p(i), c)))
            prev_args.append(z)
        if d == 1:
            cur_specs.append(pl.BlockSpec((rows, QKV_WIDTH), lambda i: (i, 0)))
        else:
            cur_specs.append(pl.BlockSpec((d, rows, QKV_WIDTH), lambda i: (0, i, 0)))
    scr_shape = (N_GROUPS, ATT_MERGED // LANES, ATT_TILE, LANES)
    bias_shape = (N_GROUPS, HEADS_PER_GROUP * BLK, 2 * BLK)
    return pl.pallas_call(
        functools.partial(_attn_kernel, slopes=slopes),
        out_shape=jax.ShapeDtypeStruct((s, ATT_MERGED), BF16),
        grid=(s // ATT_TILE,),
        in_specs=cur_specs + prev_specs,
        out_specs=pl.BlockSpec((ATT_TILE, ATT_MERGED), lambda i: (i, 0)),
        scratch_shapes=[pltpu.VMEM(bias_shape, F32), pltpu.VMEM(bias_shape, F32),
                        pltpu.VMEM(scr_shape, F32), pltpu.VMEM(scr_shape, F32),
                        pltpu.VMEM(scr_shape, F32), pltpu.VMEM(scr_shape[1:], F32)],
        compiler_params=pltpu.CompilerParams(
            dimension_semantics=("arbitrary",), vmem_limit_bytes=VMEM_LIMIT_BYTES),
        name="attn",
    )(*z_groups, *prev_args)


def _pool_weight_kernel(wgrp_ref, scale_ref, wpo_ref, wc_ref):
    c = POOL_GROUP_WIDTH
    for g in range(len(POOL_WINDOWS)):
        rows = slice(g * c, (g + 1) * c)
        wc = jnp.dot(wgrp_ref[g], scale_ref[rows, :] * wpo_ref[rows, :],
                     precision=lax.Precision.HIGHEST, preferred_element_type=F32)
        wc_ref[rows, :] = wc.astype(BF16)


def _pool_weight(w_grp, scale_col, wpo):
    return pl.pallas_call(
        _pool_weight_kernel,
        out_shape=jax.ShapeDtypeStruct((POOL_WIDTH, D_MODEL), BF16),
        compiler_params=pltpu.CompilerParams(vmem_limit_bytes=VMEM_LIMIT_BYTES),
        name="pool_weight",
    )(w_grp, scale_col, wpo)


def _post_kernel(a_ref, pz_ref, pzprev_ref, ga_ref, gp_ref, x_ref, wao_hbm, wc_ref, wout_hbm,
                 g_ref, w1_hbm, w2_hbm, gf_ref, o_ref,
                 wao_ref, wout_ref, w1_ref, w2_ref, stage_ref, sem_ref):
    i = pl.program_id(0)

    def weight_pieces():
        max_rows, width = stage_ref.shape[1:]

        def split(w, w_bf, rows, cols):
            step = min(max_rows, rows.stop - rows.start)
            return [(w.at[r:r + step, cols], w_bf.at[r:r + step, cols], None)
                    for r in range(rows.start, rows.stop, step)]

        assert D_MODEL == width and MLP_CHUNK % width == 0
        full = slice(0, D_MODEL)
        groups = [split(wao_hbm, wao_ref, slice(0, ATT_MERGED), full),
                  split(wout_hbm, wout_ref, full, full)]
        for c in range(D_FF // MLP_CHUNK):
            chunk = [slice(k, k + width) for k in range(c * MLP_CHUNK, (c + 1) * MLP_CHUNK, width)]
            groups.append([p for cols in chunk for p in split(w1_hbm, w1_ref, full, cols)])
            groups.append([p for rows in chunk for p in split(w2_hbm, w2_ref, rows, full)])
        return [p for group in groups for p in group], [len(group) for group in groups]

    def body(stream, uses):
        needed = [0]

        def dot(lhs, w_view):
            if stream is not None:
                needed[0] += uses.pop(0)
                stream.ensure(needed[0])
            return jnp.dot(lhs, w_view[...], preferred_element_type=F32)

        att = dot(a_ref[...], wao_ref)
        t = i * ROW_TILE + lax.broadcasted_iota(jnp.int32, (ROW_TILE, 1), 0)
        pooled = []
        for c in range(POOL_WIDTH // LANES):
            lanes = slice(c * LANES, (c + 1) * LANES)
            prev = jnp.where(i > 0, pzprev_ref[:, lanes].astype(F32), 0.0)
            pooled.append(
                _pooled_minus_token(pz_ref[:, lanes].astype(F32), prev, t, c).astype(BF16))
        pool = jnp.dot(jnp.concatenate(pooled, axis=1), wc_ref[...], preferred_element_type=F32)
        merged = (jax.nn.sigmoid(ga_ref[...].astype(F32)) * att
                  + jax.nn.sigmoid(gp_ref[...].astype(F32)) * pool)
        h = x_ref[...] + dot(merged.astype(BF16), wout_ref)

        m = (h * g_ref[...]).astype(BF16)
        mlp = None
        for c in range(D_FF // MLP_CHUNK):
            cols = slice(c * MLP_CHUNK, (c + 1) * MLP_CHUNK)
            hid = dot(m, w1_ref.at[:, cols])
            hid = jnp.square(jnp.maximum(hid, 0.0)).astype(BF16)
            update = dot(hid, w2_ref.at[cols, :])
            mlp = update if mlp is None else mlp + update
        y = h + jnp.square(_rms_scale(h)) * mlp
        o_ref[...] = y * _rms_scale(y) * gf_ref[...]

    def first_step():
        pieces, uses = weight_pieces()
        body(_WeightStream(pieces, stage_ref, sem_ref), uses)

    pl.when(i == 0)(first_step)
    pl.when(i > 0)(lambda: body(None, None))


def _post(a, zn, zg, x2d, wao, wc, wout, g_mlp, w1, w2, g_final):
    s = x2d.shape[0]
    tm = ROW_TILE
    row = lambda i: (i, 0)
    const = lambda i: (0, 0)
    halo_blocks = tm // POOL_HALO
    assert QKV_WIDTH % POOL_WIDTH == 0
    pool_col = QKV_WIDTH // POOL_WIDTH
    return pl.pallas_call(
        _post_kernel,
        out_shape=jax.ShapeDtypeStruct((s, D_MODEL), F32),
        grid=(s // tm,),
        in_specs=[
            pl.BlockSpec((tm, ATT_MERGED), row),
            pl.BlockSpec((tm, POOL_WIDTH), lambda i: (i, pool_col)),
            pl.BlockSpec((POOL_HALO, POOL_WIDTH),
                         lambda i: (jnp.maximum(i * halo_blocks - 1, 0), pool_col)),
            pl.BlockSpec((tm, D_MODEL), lambda i: (i, 0)),
            pl.BlockSpec((tm, D_MODEL), lambda i: (i, 1)),
            pl.BlockSpec((tm, D_MODEL), row),
            HBM_SPEC,
            pl.BlockSpec((POOL_WIDTH, D_MODEL), const),
            HBM_SPEC,
            pl.BlockSpec((1, D_MODEL), const),
            HBM_SPEC,
            HBM_SPEC,
            pl.BlockSpec((1, D_MODEL), const),
        ],
        out_specs=pl.BlockSpec((tm, D_MODEL), row),
        scratch_shapes=[pltpu.VMEM((ATT_MERGED, D_MODEL), BF16),
                        pltpu.VMEM((D_MODEL, D_MODEL), BF16),
                        pltpu.VMEM((D_MODEL, D_FF), BF16),
                        pltpu.VMEM((D_FF, D_MODEL), BF16),
                        pltpu.VMEM((WEIGHT_STAGE_SLOTS, WEIGHT_STAGE_ROWS, D_MODEL), F32),
                        pltpu.SemaphoreType.DMA((WEIGHT_STAGE_SLOTS,))],
        compiler_params=pltpu.CompilerParams(
            dimension_semantics=("arbitrary",), vmem_limit_bytes=VMEM_LIMIT_BYTES),
        name="post",
    )(a, zn, zn, zg, zg, x2d, wao, wc, wout, g_mlp, w1, w2, g_final)


def _alibi_slopes():
    return tuple(2.0 ** (-ALIBI_MAX_BIAS * (h + 1.0) / N_ATT_HEADS) for h in range(N_ATT_HEADS))


def kernel(x, norm_mix_g, w_in, w_att_out, w_pool_grp, pool_scale, w_pool_out, w_out,
           norm_mlp_g, w_mlp_in, w_mlp_out, norm_final_g):
    batch, seq, d = x.shape
    assert d == D_MODEL and norm_mix_g.shape[0] == 1, "one layer of width D_MODEL"
    assert seq % ATT_TILE == 0 and seq % ROW_TILE == 0 and seq % IN_PROJ_ROWS == 0
    assert ATT_GROUPS[0][1] == 1 and all(d % MERGE_MOD == 0 for _, d in ATT_GROUPS[1:])
    slopes = _alibi_slopes()
    outs = []
    for b in range(batch):
        h = x[b]
        zn, zg, z1, z2 = _in_proj(h, norm_mix_g[0][None, :], w_in[0])
        a = _attention((zn, z1, z2), slopes)
        wc = _pool_weight(w_pool_grp[0], pool_scale[0][:, None], w_pool_out[0])
        outs.append(_post(a, zn, zg, h, w_att_out[0], wc, w_out[0], norm_mlp_g[0][None, :],
                          w_mlp_in[0], w_mlp_out[0], norm_final_g[None, :]))
    return outs[0][None] if batch == 1 else jnp.stack(outs, axis=0)
```

```python
import functools
import math

import jax
import jax.numpy as jnp
from jax import lax
from jax.experimental import pallas as pl
from jax.experimental.pallas import tpu as pltpu

D_MODEL = 1024
HEAD_DIM = 64
ATT_GROUPS = ((128, 1), (512, 4), (2048, 16))
N_GROUPS = len(ATT_GROUPS)
HEADS_PER_GROUP = 4
N_ATT_HEADS = HEADS_PER_GROUP * N_GROUPS
ATT_WIDTH = N_ATT_HEADS * HEAD_DIM
ATT_MERGED = HEADS_PER_GROUP * HEAD_DIM
QKV_WIDTH = 3 * ATT_MERGED
BLK = 128
POOL_WINDOWS = (2, 4, 8, 16)
POOL_GROUP_WIDTH = 3 * D_MODEL // 16
POOL_WIDTH = POOL_GROUP_WIDTH * len(POOL_WINDOWS)
D_FF = 4 * D_MODEL
N_IN = 3 * ATT_WIDTH + POOL_WIDTH + 2 * D_MODEL
NORM_EPS = 1e-6
ALIBI_MAX_BIAS = 8.0
POOL_HALO = max(POOL_WINDOWS)

VMEM_LIMIT_BYTES = 56 * 1024 * 1024
LANES = 128
MASKED_SCORE = -1e30

IN_PROJ_ROWS = 512
ROW_TILE = 512
MLP_CHUNK = 1024
WEIGHT_STAGE_ROWS = 512
WEIGHT_STAGE_SLOTS = 4
MAX_DILATION = max(d for _, d in ATT_GROUPS)
ATT_TILE = BLK * MAX_DILATION
STRIDE_STEP = 4
MERGE_MOD = 4
MERGE_UNROLL = 4
LOG2_E = math.log2(math.e)

ZN_WIDTH = QKV_WIDTH + POOL_WIDTH
ZG_WIDTH = 2 * D_MODEL
NAT_WIDTH = ZN_WIDTH + ZG_WIDTH

BF16 = jnp.bfloat16
F32 = jnp.float32


def _rms_scale(x):
    return lax.rsqrt(jnp.mean(x * x, axis=-1, keepdims=True) + NORM_EPS)


class _WeightStream:
    def __init__(self, pieces, stage_ref, sem_ref):
        self.pieces, self.stage_ref, self.sem_ref = pieces, stage_ref, sem_ref
        self.n_slots = stage_ref.shape[0]
        self.started = self.converted = 0

    def _slot(self, k):
        return self.stage_ref.at[k % self.n_slots, pl.ds(0, self.pieces[k][0].shape[0])]

    def _copy(self, k):
        return pltpu.make_async_copy(self.pieces[k][0], self._slot(k),
                                     self.sem_ref.at[k % self.n_slots])

    def _start_ahead(self):
        while self.started < min(len(self.pieces), self.converted + self.n_slots):
            self._copy(self.started).start()
            self.started += 1

    def ensure(self, count):
        self._start_ahead()
        while self.converted < count:
            k = self.converted
            _, dst, scale = self.pieces[k]
            self._copy(k).wait()
            piece = self._slot(k)[...]
            dst[...] = (piece if scale is None else piece * scale).astype(BF16)
            self.converted += 1
            self._start_ahead()


HBM_SPEC = pl.BlockSpec(memory_space=pl.ANY)


def _w_in_pieces(w_hbm, w_ref):
    pieces = []
    for j in range(N_IN // ATT_MERGED):
        which, g = divmod(j, N_GROUPS)
        if which < 3:
            dst = which * ATT_MERGED + (0 if g == 0 else NAT_WIDTH + (g - 1) * QKV_WIDTH)
        else:
            dst = QKV_WIDTH + (j - 3 * N_GROUPS) * ATT_MERGED
        scale = HEAD_DIM ** -0.5 * LOG2_E if which == 0 else None
        pieces.append((dst, w_hbm.at[:, j * ATT_MERGED:(j + 1) * ATT_MERGED],
                       w_ref.at[:, dst:dst + ATT_MERGED], scale))
    return [piece[1:] for piece in sorted(pieces, key=lambda piece: piece[0])]


def _pooled_minus_token(pf, prev, t, c):
    groups = [g for g in range(len(POOL_WINDOWS))
              if g * POOL_GROUP_WIDTH < (c + 1) * LANES and (g + 1) * POOL_GROUP_WIDTH > c * LANES]
    acc = jnp.concatenate([prev, pf], axis=0)
    w, means = 1, []
    for g in groups:
        while w < POOL_WINDOWS[g]:
            acc = acc + pltpu.roll(acc, w, axis=0)
            w *= 2
        count = jnp.minimum(t + 1, POOL_WINDOWS[g]).astype(F32)
        means.append(acc[POOL_HALO:, :] / count)
    pooled = means[-1]
    for g, mean in zip(groups[-2::-1], means[-2::-1]):
        lane = lax.broadcasted_iota(jnp.int32, mean.shape, 1)
        pooled = jnp.where(lane < (g + 1) * POOL_GROUP_WIDTH - c * LANES, mean, pooled)
    return pooled - pf


def _in_proj_kernel(x_ref, g_ref, w_hbm, zn_ref, zg_ref, *rest):
    dil_refs, (slab_ref, w_ref, stage_ref, sem_ref) = rest[:-4], rest[-4:]

    def body(stream):
        def project(lo, width):
            if stream is not None:
                stream.ensure((lo + width) // ATT_MERGED)
            return jnp.dot(u, w_ref[:, lo:lo + width], preferred_element_type=F32) * scale

        x = x_ref[...]
        scale = _rms_scale(x)
        u = (x * g_ref[...]).astype(BF16)
        lo = 0
        for out_ref, width in ((zn_ref, QKV_WIDTH), (zn_ref, POOL_WIDTH),
                               (zg_ref, D_MODEL), (zg_ref, D_MODEL)):
            off = lo - (0 if out_ref is zn_ref else ZN_WIDTH)
            out_ref[:, off:off + width] = project(lo, width).astype(BF16)
            lo += width
        n_slabs = QKV_WIDTH // LANES
        for gi, z_ref in enumerate(dil_refs):
            dilation = ATT_GROUPS[gi + 1][1]
            acc = project(NAT_WIDTH + gi * QKV_WIDTH, QKV_WIDTH)
            for sl in range(n_slabs):
                slab_ref[0, sl] = acc[:, sl * LANES:(sl + 1) * LANES]
            src, n_rows, stride = 0, IN_PROJ_ROWS, dilation
            while stride > STRIDE_STEP:
                n_rows //= STRIDE_STEP
                for c in range(IN_PROJ_ROWS // n_rows):
                    base = (c // STRIDE_STEP) * n_rows * STRIDE_STEP + c % STRIDE_STEP
                    for sl in range(n_slabs):
                        slab_ref[1 - src, sl, c * n_rows:(c + 1) * n_rows, :] = \
                            slab_ref[src, sl, pl.ds(base, n_rows, stride=STRIDE_STEP), :]
                src, stride = 1 - src, stride // STRIDE_STEP
            outer = dilation // stride
            for r in range(dilation):
                lo_class, hi_class = r % outer, r // outer
                base = lo_class * (IN_PROJ_ROWS // outer) + hi_class
                for sl in range(n_slabs):
                    rows = slab_ref[src, sl, pl.ds(base, IN_PROJ_ROWS // dilation, stride=stride), :]
                    z_ref[r, :, sl * LANES:(sl + 1) * LANES] = rows.astype(BF16)

    first = pl.program_id(0) == 0
    pl.when(first)(lambda: body(_WeightStream(_w_in_pieces(w_hbm, w_ref), stage_ref, sem_ref)))
    pl.when(jnp.logical_not(first))(lambda: body(None))


def _in_proj(x2d, g, w_in):
    s = x2d.shape[0]
    dils = [d for _, d in ATT_GROUPS[1:]]
    out_shape = [jax.ShapeDtypeStruct((s, ZN_WIDTH), BF16), jax.ShapeDtypeStruct((s, ZG_WIDTH), BF16)]
    tm = IN_PROJ_ROWS
    out_specs = [pl.BlockSpec((tm, ZN_WIDTH), lambda i: (i, 0)),
                 pl.BlockSpec((tm, ZG_WIDTH), lambda i: (i, 0))]
    for d in dils:
        out_shape.append(jax.ShapeDtypeStruct((d, s // d, QKV_WIDTH), BF16))
        out_specs.append(pl.BlockSpec((d, tm // d, QKV_WIDTH), lambda i: (0, i, 0)))
    return pl.pallas_call(
        _in_proj_kernel,
        out_shape=out_shape,
        grid=(s // tm,),
        in_specs=[
            pl.BlockSpec((tm, D_MODEL), lambda i: (i, 0)),
            pl.BlockSpec((1, D_MODEL), lambda i: (0, 0)),
            HBM_SPEC,
        ],
        out_specs=out_specs,
        scratch_shapes=[pltpu.VMEM((2, QKV_WIDTH // LANES, tm, LANES), F32),
                        pltpu.VMEM((D_MODEL, N_IN), BF16),
                        pltpu.VMEM((WEIGHT_STAGE_SLOTS, D_MODEL, ATT_MERGED), F32),
                        pltpu.SemaphoreType.DMA((WEIGHT_STAGE_SLOTS,))],
        compiler_params=pltpu.CompilerParams(
            dimension_semantics=("arbitrary",), vmem_limit_bytes=VMEM_LIMIT_BYTES),
        name="in_proj",
    )(x2d, g, w_in)


def _attend_block(qb, kk, vv, bias, head_masks_bf16, low_half):
    nh = HEADS_PER_GROUP
    q_stack = jnp.concatenate([qb * head_masks_bf16[h] for h in range(nh)], axis=0)
    s = lax.dot_general(q_stack, kk, (((1,), (1,)), ((), ())), preferred_element_type=F32)
    s = s + bias
    m = jnp.max(s, axis=-1, keepdims=True)
    p = jnp.exp2(s - m)
    l = jnp.sum(p, axis=-1, keepdims=True)
    pv = jnp.dot(p.astype(BF16), vv, preferred_element_type=F32)
    slabs = []
    for sl in range(ATT_MERGED // LANES):
        lanes = slice(sl * LANES, (sl + 1) * LANES)
        rows_a = slice(2 * sl * BLK, (2 * sl + 1) * BLK)
        rows_b = slice((2 * sl + 1) * BLK, (2 * sl + 2) * BLK)
        slabs.append((jnp.where(low_half, pv[rows_a, lanes], pv[rows_b, lanes]),
                      jnp.where(low_half, m[rows_a], m[rows_b]),
                      jnp.where(low_half, l[rows_a], l[rows_b])))
    return slabs


def _attn_kernel(*refs, slopes):
    z_refs = refs[0:N_GROUPS]
    kprev_refs = refs[N_GROUPS:3 * N_GROUPS:2]
    vprev_refs = refs[N_GROUPS + 1:3 * N_GROUPS:2]
    a_ref, bias_ref, bias0_ref, o_scr, m_scr, l_scr, nat_scr = refs[3 * N_GROUPS:]
    step = pl.program_id(0)
    nh = HEADS_PER_GROUP
    n_slabs = ATT_MERGED // LANES

    @pl.when(step == 0)
    def _():
        qi = lax.broadcasted_iota(jnp.int32, (BLK, 2 * BLK), 0)
        kj = lax.broadcasted_iota(jnp.int32, (BLK, 2 * BLK), 1)
        steps = BLK + qi - kj
        for g, (window, dilation) in enumerate(ATT_GROUPS):
            valid = (steps >= 0) & (steps <= window // dilation)
            dist = (steps * dilation).astype(F32)
            for h in range(nh):
                bias = jnp.where(valid, -(slopes[g * nh + h] * LOG2_E) * dist, MASKED_SCORE)
                bias_ref[g, h * BLK:(h + 1) * BLK, :] = bias
                bias0_ref[g, h * BLK:(h + 1) * BLK, :] = jnp.where(kj < BLK, MASKED_SCORE, bias)

    @pl.when(step == 1)
    def _():
        bias0_ref[...] = bias_ref[...]

    lane_head = lax.broadcasted_iota(jnp.int32, (BLK, ATT_MERGED), 1) // HEAD_DIM
    head_masks_bf16 = [(lane_head == h).astype(F32).astype(BF16) for h in range(nh)]
    low_half = lax.broadcasted_iota(jnp.int32, (BLK, LANES), 1) < HEAD_DIM
    attend = functools.partial(_attend_block, head_masks_bf16=head_masks_bf16, low_half=low_half)
    q_cols, k_cols, v_cols = (slice(c * ATT_MERGED, (c + 1) * ATT_MERGED) for c in range(3))

    sub_rows = ATT_TILE // MERGE_MOD

    def store(g, dilation, r, b, slabs):
        if dilation in (1, MERGE_MOD):
            rows = pl.ds(r * (ATT_TILE // dilation) + b * BLK, BLK)
        else:
            per = dilation // MERGE_MOD
            rows = pl.ds((r % MERGE_MOD) * sub_rows + b * BLK * per + r // MERGE_MOD, BLK,
                         stride=per)
        for sl, (o, m, l) in enumerate(slabs):
            o_scr[g, sl, rows, :] = o
            m_scr[g, sl, rows, :] = m
            l_scr[g, sl, rows, :] = l

    for g, (_, dilation) in enumerate(ATT_GROUPS):
        n_blocks = ATT_TILE // dilation // BLK
        z_ref, kprev_ref, vprev_ref = z_refs[g], kprev_refs[g], vprev_refs[g]

        def load(r, row0, n_rows, cols):
            return z_ref[row0:row0 + n_rows, cols] if dilation == 1 else \
                z_ref[r, row0:row0 + n_rows, cols]

        for r in range(dilation):
            kprev = kprev_ref[...] if dilation == 1 else kprev_ref[r]
            vprev = vprev_ref[...] if dilation == 1 else vprev_ref[r]
            kk = jnp.concatenate([kprev, load(r, 0, BLK, k_cols)], axis=0)
            vv = jnp.concatenate([vprev, load(r, 0, BLK, v_cols)], axis=0)
            store(g, dilation, r, 0, attend(load(r, 0, BLK, q_cols), kk, vv, bias0_ref[g]))
            for b in range(1, n_blocks):
                kk = load(r, (b - 1) * BLK, 2 * BLK, k_cols)
                vv = load(r, (b - 1) * BLK, 2 * BLK, v_cols)
                store(g, dilation, r, b, attend(load(r, b * BLK, BLK, q_cols), kk, vv, bias_ref[g]))

    chunks = sub_rows // BLK

    def merge_body(idx, _):
        c = idx // chunks
        row0 = (idx % chunks) * BLK
        nat_rows = pl.ds(row0 * MERGE_MOD + c, BLK, stride=MERGE_MOD)
        cls_rows = pl.ds(pl.multiple_of(c * sub_rows + row0, BLK), BLK)
        rows_of = lambda g: nat_rows if ATT_GROUPS[g][1] == 1 else cls_rows
        for sl in range(n_slabs):
            ms = [m_scr[g, sl, rows_of(g), :] for g in range(N_GROUPS)]
            mmax = functools.reduce(jnp.maximum, ms)
            ws = [jnp.exp2(m - mmax) for m in ms]
            num = sum(w * o_scr[g, sl, rows_of(g), :] for g, w in enumerate(ws))
            den = sum(w * l_scr[g, sl, rows_of(g), :] for g, w in enumerate(ws))
            nat_scr[sl, nat_rows, :] = num / den
        return 0

    lax.fori_loop(0, MERGE_MOD * chunks, merge_body, 0, unroll=MERGE_UNROLL)
    for sl in range(n_slabs):
        a_ref[:, sl * LANES:(sl + 1) * LANES] = nat_scr[sl].astype(BF16)


def _attention(z_groups, slopes):
    s = z_groups[0].shape[0]
    cur_specs, prev_specs, prev_args = [], [], []
    for z, (_, d) in zip(z_groups, ATT_GROUPS):
        rows = ATT_TILE // d
        prev_row = lambda i, rows=rows: jnp.maximum(i * (rows // BLK) - 1, 0)
        for col in (1, 2):
            if d == 1:
                prev_specs.append(pl.BlockSpec((BLK, ATT_MERGED),
                                               lambda i, c=col, p=prev_row: (p(i), c)))
            else:
                prev_specs.append(pl.BlockSpec((d, BLK, ATT_MERGED),
                                               lambda i, c=col, p=prev_row: (0, p(i), c)))
            prev_args.append(z)
        if d == 1:
            cur_specs.append(pl.BlockSpec((rows, QKV_WIDTH), lambda i: (i, 0)))
        else:
            cur_specs.append(pl.BlockSpec((d, rows, QKV_WIDTH), lambda i: (0, i, 0)))
    scr_shape = (N_GROUPS, ATT_MERGED // LANES, ATT_TILE, LANES)
    bias_shape = (N_GROUPS, HEADS_PER_GROUP * BLK, 2 * BLK)
    return pl.pallas_call(
        functools.partial(_attn_kernel, slopes=slopes),
        out_shape=jax.ShapeDtypeStruct((s, ATT_MERGED), BF16),
        grid=(s // ATT_TILE,),
        in_specs=cur_specs + prev_specs,
        out_specs=pl.BlockSpec((ATT_TILE, ATT_MERGED), lambda i: (i, 0)),
        scratch_shapes=[pltpu.VMEM(bias_shape, F32), pltpu.VMEM(bias_shape, F32),
                        pltpu.VMEM(scr_shape, F32), pltpu.VMEM(scr_shape, F32),
                        pltpu.VMEM(scr_shape, F32), pltpu.VMEM(scr_shape[1:], F32)],
        compiler_params=pltpu.CompilerParams(
            dimension_semantics=("arbitrary",), vmem_limit_bytes=VMEM_LIMIT_BYTES),
        name="attn",
    )(*z_groups, *prev_args)


def _pool_weight_kernel(wgrp_ref, scale_ref, wpo_ref, wc_ref):
    c = POOL_GROUP_WIDTH
    for g in range(len(POOL_WINDOWS)):
        rows = slice(g * c, (g + 1) * c)
        wc = jnp.dot(wgrp_ref[g], scale_ref[rows, :] * wpo_ref[rows, :],
                     precision=lax.Precision.HIGHEST, preferred_element_type=F32)
        wc_ref[rows, :] = wc.astype(BF16)


def _pool_weight(w_grp, scale_col, wpo):
    return pl.pallas_call(
        _pool_weight_kernel,
        out_shape=jax.ShapeDtypeStruct((POOL_WIDTH, D_MODEL), BF16),
        compiler_params=pltpu.CompilerParams(vmem_limit_bytes=VMEM_LIMIT_BYTES),
        name="pool_weight",
    )(w_grp, scale_col, wpo)


def _post_kernel(a_ref, pz_ref, pzprev_ref, ga_ref, gp_ref, x_ref, wao_hbm, wc_ref, wout_hbm,
                 g_ref, w1_hbm, w2_hbm, gf_ref, o_ref,
                 wao_ref, wout_ref, w1_ref, w2_ref, stage_ref, sem_ref):
    i = pl.program_id(0)

    def weight_pieces():
        max_rows, width = stage_ref.shape[1:]

        def split(w, w_bf, rows, cols):
            step = min(max_rows, rows.stop - rows.start)
            return [(w.at[r:r + step, cols], w_bf.at[r:r + step, cols], None)
                    for r in range(rows.start, rows.stop, step)]

        assert D_MODEL == width and MLP_CHUNK % width == 0
        full = slice(0, D_MODEL)
        groups = [split(wao_hbm, wao_ref, slice(0, ATT_MERGED), full),
                  split(wout_hbm, wout_ref, full, full)]
        for c in range(D_FF // MLP_CHUNK):
            chunk = [slice(k, k + width) for k in range(c * MLP_CHUNK, (c + 1) * MLP_CHUNK, width)]
            groups.append([p for cols in chunk for p in split(w1_hbm, w1_ref, full, cols)])
            groups.append([p for rows in chunk for p in split(w2_hbm, w2_ref, rows, full)])
        return [p for group in groups for p in group], [len(group) for group in groups]

    def body(stream, uses):
        needed = [0]

        def dot(lhs, w_view):
            if stream is not None:
                needed[0] += uses.pop(0)
                stream.ensure(needed[0])
            return jnp.dot(lhs, w_view[...], preferred_element_type=F32)

        att = dot(a_ref[...], wao_ref)
        t = i * ROW_TILE + lax.broadcasted_iota(jnp.int32, (ROW_TILE, 1), 0)
        pooled = []
        for c in range(POOL_WIDTH // LANES):
            lanes = slice(c * LANES, (c + 1) * LANES)
            prev = jnp.where(i > 0, pzprev_ref[:, lanes].astype(F32), 0.0)
            pooled.append(
                _pooled_minus_token(pz_ref[:, lanes].astype(F32), prev, t, c).astype(BF16))
        pool = jnp.dot(jnp.concatenate(pooled, axis=1), wc_ref[...], preferred_element_type=F32)
        merged = (jax.nn.sigmoid(ga_ref[...].astype(F32)) * att
                  + jax.nn.sigmoid(gp_ref[...].astype(F32)) * pool)
        h = x_ref[...] + dot(merged.astype(BF16), wout_ref)

        m = (h * g_ref[...]).astype(BF16)
        mlp = None
        for c in range(D_FF // MLP_CHUNK):
            cols = slice(c * MLP_CHUNK, (c + 1) * MLP_CHUNK)
            hid = dot(m, w1_ref.at[:, cols])
            hid = jnp.square(jnp.maximum(hid, 0.0)).astype(BF16)
            update = dot(hid, w2_ref.at[cols, :])
            mlp = update if mlp is None else mlp + update
        y = h + jnp.square(_rms_scale(h)) * mlp
        o_ref[...] = y * _rms_scale(y) * gf_ref[...]

    def first_step():
        pieces, uses = weight_pieces()
        body(_WeightStream(pieces, stage_ref, sem_ref), uses)

    pl.when(i == 0)(first_step)
    pl.when(i > 0)(lambda: body(None, None))


def _post(a, zn, zg, x2d, wao, wc, wout, g_mlp, w1, w2, g_final):
    s = x2d.shape[0]
    tm = ROW_TILE
    row = lambda i: (i, 0)
    const = lambda i: (0, 0)
    halo_blocks = tm // POOL_HALO
    assert QKV_WIDTH % POOL_WIDTH == 0
    pool_col = QKV_WIDTH // POOL_WIDTH
    return pl.pallas_call(
        _post_kernel,
        out_shape=jax.ShapeDtypeStruct((s, D_MODEL), F32),
        grid=(s // tm,),
        in_specs=[
            pl.BlockSpec((tm, ATT_MERGED), row),
            pl.BlockSpec((tm, POOL_WIDTH), lambda i: (i, pool_col)),
            pl.BlockSpec((POOL_HALO, POOL_WIDTH),
                         lambda i: (jnp.maximum(i * halo_blocks - 1, 0), pool_col)),
            pl.BlockSpec((tm, D_MODEL), lambda i: (i, 0)),
            pl.BlockSpec((tm, D_MODEL), lambda i: (i, 1)),
            pl.BlockSpec((tm, D_MODEL), row),
            HBM_SPEC,
            pl.BlockSpec((POOL_WIDTH, D_MODEL), const),
            HBM_SPEC,
            pl.BlockSpec((1, D_MODEL), const),
            HBM_SPEC,
            HBM_SPEC,
            pl.BlockSpec((1, D_MODEL), const),
        ],
        out_specs=pl.BlockSpec((tm, D_MODEL), row),
        scratch_shapes=[pltpu.VMEM((ATT_MERGED, D_MODEL), BF16),
                        pltpu.VMEM((D_MODEL, D_MODEL), BF16),
                        pltpu.VMEM((D_MODEL, D_FF), BF16),
                        pltpu.VMEM((D_FF, D_MODEL), BF16),
                        pltpu.VMEM((WEIGHT_STAGE_SLOTS, WEIGHT_STAGE_ROWS, D_MODEL), F32),
                        pltpu.SemaphoreType.DMA((WEIGHT_STAGE_SLOTS,))],
        compiler_params=pltpu.CompilerParams(
            dimension_semantics=("arbitrary",), vmem_limit_bytes=VMEM_LIMIT_BYTES),
        name="post",
    )(a, zn, zn, zg, zg, x2d, wao, wc, wout, g_mlp, w1, w2, g_final)


def _alibi_slopes():
    return tuple(2.0 ** (-ALIBI_MAX_BIAS * (h + 1.0) / N_ATT_HEADS) for h in range(N_ATT_HEADS))


def kernel(x, norm_mix_g, w_in, w_att_out, w_pool_grp, pool_scale, w_pool_out, w_out,
           norm_mlp_g, w_mlp_in, w_mlp_out, norm_final_g):
    batch, seq, d = x.shape
    assert d == D_MODEL and norm_mix_g.shape[0] == 1, "one layer of width D_MODEL"
    assert seq % ATT_TILE == 0 and seq % ROW_TILE == 0 and seq % IN_PROJ_ROWS == 0
    assert ATT_GROUPS[0][1] == 1 and all(d % MERGE_MOD == 0 for _, d in ATT_GROUPS[1:])
    slopes = _alibi_slopes()
    outs = []
    for b in range(batch):
        h = x[b]
        zn, zg, z1, z2 = _in_proj(h, norm_mix_g[0][None, :], w_in[0])
        a = _attention((zn, z1, z2), slopes)
        wc = _pool_weight(w_pool_grp[0], pool_scale[0][:, None], w_pool_out[0])
        outs.append(_post(a, zn, zg, h, w_att_out[0], wc, w_out[0], norm_mlp_g[0][None, :],
                          w_mlp_in[0], w_mlp_out[0], norm_final_g[None, :]))
    return outs[0][None] if batch == 1 else jnp.stack(outs, axis=0)
```

```python
import functools
import math

import jax
import jax.numpy as jnp
from jax import lax
from jax.experimental import pallas as pl
from jax.experimental.pallas import tpu as pltpu

D_MODEL = 1024
HEAD_DIM = 64
ATT_GROUPS = ((128, 1), (512, 4), (2048, 16))
N_GROUPS = len(ATT_GROUPS)
HEADS_PER_GROUP = 4
N_ATT_HEADS = HEADS_PER_GROUP * N_GROUPS
ATT_WIDTH = N_ATT_HEADS * HEAD_DIM
ATT_MERGED = HEADS_PER_GROUP * HEAD_DIM
QKV_WIDTH = 3 * ATT_MERGED
BLK = 128
POOL_WINDOWS = (2, 4, 8, 16)
POOL_GROUP_WIDTH = 3 * D_MODEL // 16
POOL_WIDTH = POOL_GROUP_WIDTH * len(POOL_WINDOWS)
D_FF = 4 * D_MODEL
N_IN = 3 * ATT_WIDTH + POOL_WIDTH + 2 * D_MODEL
NORM_EPS = 1e-6
ALIBI_MAX_BIAS = 8.0
POOL_HALO = max(POOL_WINDOWS)

VMEM_LIMIT_BYTES = 56 * 1024 * 1024
LANES = 128
MASKED_SCORE = -1e30

IN_PROJ_ROWS = 512
ROW_TILE = 512
MLP_CHUNK = 1024
WEIGHT_STAGE_ROWS = 512
WEIGHT_STAGE_SLOTS = 4
MAX_DILATION = max(d for _, d in ATT_GROUPS)
ATT_TILE = BLK * MAX_DILATION
STRIDE_STEP = 4
MERGE_MOD = 4
MERGE_UNROLL = 4
LOG2_E = math.log2(math.e)

ZN_WIDTH = QKV_WIDTH + POOL_WIDTH
ZG_WIDTH = 2 * D_MODEL
NAT_WIDTH = ZN_WIDTH + ZG_WIDTH

BF16 = jnp.bfloat16
F32 = jnp.float32


def _rms_scale(x):
    return lax.rsqrt(jnp.mean(x * x, axis=-1, keepdims=True) + NORM_EPS)


class _WeightStream:
    def __init__(self, pieces, stage_ref, sem_ref):
        self.pieces, self.stage_ref, self.sem_ref = pieces, stage_ref, sem_ref
        self.n_slots = stage_ref.shape[0]
        self.started = self.converted = 0

    def _slot(self, k):
        return self.stage_ref.at[k % self.n_slots, pl.ds(0, self.pieces[k][0].shape[0])]

    def _copy(self, k):
        return pltpu.make_async_copy(self.pieces[k][0], self._slot(k),
                                     self.sem_ref.at[k % self.n_slots])

    def _start_ahead(self):
        while self.started < min(len(self.pieces), self.converted + self.n_slots):
            self._copy(self.started).start()
            self.started += 1

    def ensure(self, count):
        self._start_ahead()
        while self.converted < count:
            k = self.converted
            _, dst, scale = self.pieces[k]
            self._copy(k).wait()
            piece = self._slot(k)[...]
            dst[...] = (piece if scale is None else piece * scale).astype(BF16)
            self.converted += 1
            self._start_ahead()


HBM_SPEC = pl.BlockSpec(memory_space=pl.ANY)


def _w_in_pieces(w_hbm, w_ref):
    pieces = []
    for j in range(N_IN // ATT_MERGED):
        which, g = divmod(j, N_GROUPS)
        if which < 3:
            dst = which * ATT_MERGED + (0 if g == 0 else NAT_WIDTH + (g - 1) * QKV_WIDTH)
        else:
            dst = QKV_WIDTH + (j - 3 * N_GROUPS) * ATT_MERGED
        scale = HEAD_DIM ** -0.5 * LOG2_E if which == 0 else None
        pieces.append((dst, w_hbm.at[:, j * ATT_MERGED:(j + 1) * ATT_MERGED],
                       w_ref.at[:, dst:dst + ATT_MERGED], scale))
    return [piece[1:] for piece in sorted(pieces, key=lambda piece: piece[0])]


def _compose_pool_weights(wgrp_ref, scale_ref, wpo_ref, wc_ref):
    c = POOL_GROUP_WIDTH
    for g in range(len(POOL_WINDOWS)):
        span = slice(g * c, (g + 1) * c)
        wc = jnp.dot(wgrp_ref[g] * scale_ref[:, span], wpo_ref[span, :],
                     precision=lax.Precision.HIGHEST, preferred_element_type=F32)
        wc_ref[span, :] = wc.astype(BF16)


def _pooled_minus_token(pf, prev, t, c):
    groups = [g for g in range(len(POOL_WINDOWS))
              if g * POOL_GROUP_WIDTH < (c + 1) * LANES and (g + 1) * POOL_GROUP_WIDTH > c * LANES]
    acc = jnp.concatenate([prev, pf], axis=0)
    w, means = 1, []
    for g in groups:
        while w < POOL_WINDOWS[g]:
            acc = acc + pltpu.roll(acc, w, axis=0)
            w *= 2
        count = jnp.minimum(t + 1, POOL_WINDOWS[g]).astype(F32)
        means.append(acc[POOL_HALO:, :] / count)
    pooled = means[-1]
    for g, mean in zip(groups[-2::-1], means[-2::-1]):
        lane = lax.broadcasted_iota(jnp.int32, mean.shape, 1)
        pooled = jnp.where(lane < (g + 1) * POOL_GROUP_WIDTH - c * LANES, mean, pooled)
    return pooled - pf


def _in_proj_kernel(x_ref, g_ref, w_hbm, wgrp_ref, scale_ref, wpo_ref, zn_ref, zg_ref, *rest):
    dil_refs, (wc_ref, slab_ref, w_ref, stage_ref, sem_ref) = rest[:-5], rest[-5:]

    def body(stream):
        def project(lo, width):
            if stream is not None:
                stream.ensure((lo + width) // ATT_MERGED)
            return jnp.dot(u, w_ref[:, lo:lo + width], preferred_element_type=F32) * scale

        x = x_ref[...]
        scale = _rms_scale(x)
        u = (x * g_ref[...]).astype(BF16)
        lo = 0
        for out_ref, width in ((zn_ref, QKV_WIDTH), (zn_ref, POOL_WIDTH),
                               (zg_ref, D_MODEL), (zg_ref, D_MODEL)):
            off = lo - (0 if out_ref is zn_ref else ZN_WIDTH)
            out_ref[:, off:off + width] = project(lo, width).astype(BF16)
            lo += width
        n_slabs = QKV_WIDTH // LANES
        for gi, z_ref in enumerate(dil_refs):
            dilation = ATT_GROUPS[gi + 1][1]
            acc = project(NAT_WIDTH + gi * QKV_WIDTH, QKV_WIDTH)
            for sl in range(n_slabs):
                slab_ref[0, sl] = acc[:, sl * LANES:(sl + 1) * LANES]
            src, n_rows, stride = 0, IN_PROJ_ROWS, dilation
            while stride > STRIDE_STEP:
                n_rows //= STRIDE_STEP
                for c in range(IN_PROJ_ROWS // n_rows):
                    base = (c // STRIDE_STEP) * n_rows * STRIDE_STEP + c % STRIDE_STEP
                    for sl in range(n_slabs):
                        slab_ref[1 - src, sl, c * n_rows:(c + 1) * n_rows, :] = \
                            slab_ref[src, sl, pl.ds(base, n_rows, stride=STRIDE_STEP), :]
                src, stride = 1 - src, stride // STRIDE_STEP
            outer = dilation // stride
            for r in range(dilation):
                lo_class, hi_class = r % outer, r // outer
                base = lo_class * (IN_PROJ_ROWS // outer) + hi_class
                for sl in range(n_slabs):
                    rows = slab_ref[src, sl, pl.ds(base, IN_PROJ_ROWS // dilation, stride=stride), :]
                    z_ref[r, :, sl * LANES:(sl + 1) * LANES] = rows.astype(BF16)

    def first_step():
        body(_WeightStream(_w_in_pieces(w_hbm, w_ref), stage_ref, sem_ref))
        _compose_pool_weights(wgrp_ref, scale_ref, wpo_ref, wc_ref)

    first = pl.program_id(0) == 0
    pl.when(first)(first_step)
    pl.when(jnp.logical_not(first))(lambda: body(None))


def _in_proj(x2d, g, w_in, w_grp, scale, wpo):
    s = x2d.shape[0]
    dils = [d for _, d in ATT_GROUPS[1:]]
    out_shape = [jax.ShapeDtypeStruct((s, ZN_WIDTH), BF16), jax.ShapeDtypeStruct((s, ZG_WIDTH), BF16)]
    tm = IN_PROJ_ROWS
    out_specs = [pl.BlockSpec((tm, ZN_WIDTH), lambda i: (i, 0)),
                 pl.BlockSpec((tm, ZG_WIDTH), lambda i: (i, 0))]
    for d in dils:
        out_shape.append(jax.ShapeDtypeStruct((d, s // d, QKV_WIDTH), BF16))
        out_specs.append(pl.BlockSpec((d, tm // d, QKV_WIDTH), lambda i: (0, i, 0)))
    out_shape.append(jax.ShapeDtypeStruct((POOL_WIDTH, D_MODEL), BF16))
    out_specs.append(pl.BlockSpec((POOL_WIDTH, D_MODEL), lambda i: (0, 0)))
    return pl.pallas_call(
        _in_proj_kernel,
        out_shape=out_shape,
        grid=(s // tm,),
        in_specs=[
            pl.BlockSpec((tm, D_MODEL), lambda i: (i, 0)),
            pl.BlockSpec((1, D_MODEL), lambda i: (0, 0)),
            HBM_SPEC,
            pl.BlockSpec(w_grp.shape, lambda i: (0, 0, 0)),
            pl.BlockSpec((1, POOL_WIDTH), lambda i: (0, 0)),
            pl.BlockSpec((POOL_WIDTH, D_MODEL), lambda i: (0, 0)),
        ],
        out_specs=out_specs,
        scratch_shapes=[pltpu.VMEM((2, QKV_WIDTH // LANES, tm, LANES), F32),
                        pltpu.VMEM((D_MODEL, N_IN), BF16),
                        pltpu.VMEM((WEIGHT_STAGE_SLOTS, D_MODEL, ATT_MERGED), F32),
                        pltpu.SemaphoreType.DMA((WEIGHT_STAGE_SLOTS,))],
        compiler_params=pltpu.CompilerParams(
            dimension_semantics=("arbitrary",), vmem_limit_bytes=VMEM_LIMIT_BYTES),
        name="in_proj",
    )(x2d, g, w_in, w_grp, scale, wpo)


def _attend_block(qb, kk, vv, bias, head_masks_bf16, low_half):
    nh = HEADS_PER_GROUP
    q_stack = jnp.concatenate([qb * head_masks_bf16[h] for h in range(nh)], axis=0)
    s = lax.dot_general(q_stack, kk, (((1,), (1,)), ((), ())), preferred_element_type=F32)
    s = s + bias
    m = jnp.max(s, axis=-1, keepdims=True)
    p = jnp.exp2(s - m)
    l = jnp.sum(p, axis=-1, keepdims=True)
    pv = jnp.dot(p.astype(BF16), vv, preferred_element_type=F32)
    slabs = []
    for sl in range(ATT_MERGED // LANES):
        lanes = slice(sl * LANES, (sl + 1) * LANES)
        rows_a = slice(2 * sl * BLK, (2 * sl + 1) * BLK)
        rows_b = slice((2 * sl + 1) * BLK, (2 * sl + 2) * BLK)
        slabs.append((jnp.where(low_half, pv[rows_a, lanes], pv[rows_b, lanes]),
                      jnp.where(low_half, m[rows_a], m[rows_b]),
                      jnp.where(low_half, l[rows_a], l[rows_b])))
    return slabs


def _attn_kernel(*refs, slopes):
    z_refs = refs[0:N_GROUPS]
    kprev_refs = refs[N_GROUPS:3 * N_GROUPS:2]
    vprev_refs = refs[N_GROUPS + 1:3 * N_GROUPS:2]
    a_ref, bias_ref, bias0_ref, o_scr, m_scr, l_scr, nat_scr = refs[3 * N_GROUPS:]
    step = pl.program_id(0)
    nh = HEADS_PER_GROUP
    n_slabs = ATT_MERGED // LANES

    @pl.when(step == 0)
    def _():
        qi = lax.broadcasted_iota(jnp.int32, (BLK, 2 * BLK), 0)
        kj = lax.broadcasted_iota(jnp.int32, (BLK, 2 * BLK), 1)
        steps = BLK + qi - kj
        for g, (window, dilation) in enumerate(ATT_GROUPS):
            valid = (steps >= 0) & (steps <= window // dilation)
            dist = (steps * dilation).astype(F32)
            for h in range(nh):
                bias = jnp.where(valid, -(slopes[g * nh + h] * LOG2_E) * dist, MASKED_SCORE)
                bias_ref[g, h * BLK:(h + 1) * BLK, :] = bias
                bias0_ref[g, h * BLK:(h + 1) * BLK, :] = jnp.where(kj < BLK, MASKED_SCORE, bias)

    @pl.when(step == 1)
    def _():
        bias0_ref[...] = bias_ref[...]

    lane_head = lax.broadcasted_iota(jnp.int32, (BLK, ATT_MERGED), 1) // HEAD_DIM
    head_masks_bf16 = [(lane_head == h).astype(F32).astype(BF16) for h in range(nh)]
    low_half = lax.broadcasted_iota(jnp.int32, (BLK, LANES), 1) < HEAD_DIM
    attend = functools.partial(_attend_block, head_masks_bf16=head_masks_bf16, low_half=low_half)
    q_cols, k_cols, v_cols = (slice(c * ATT_MERGED, (c + 1) * ATT_MERGED) for c in range(3))

    sub_rows = ATT_TILE // MERGE_MOD

    def store(g, dilation, r, b, slabs):
        if dilation in (1, MERGE_MOD):
            rows = pl.ds(r * (ATT_TILE // dilation) + b * BLK, BLK)
        else:
            per = dilation // MERGE_MOD
            rows = pl.ds((r % MERGE_MOD) * sub_rows + b * BLK * per + r // MERGE_MOD, BLK,
                         stride=per)
        for sl, (o, m, l) in enumerate(slabs):
            o_scr[g, sl, rows, :] = o
            m_scr[g, sl, rows, :] = m
            l_scr[g, sl, rows, :] = l

    for g, (_, dilation) in enumerate(ATT_GROUPS):
        n_blocks = ATT_TILE // dilation // BLK
        z_ref, kprev_ref, vprev_ref = z_refs[g], kprev_refs[g], vprev_refs[g]

        def load(r, row0, n_rows, cols):
            return z_ref[row0:row0 + n_rows, cols] if dilation == 1 else \
                z_ref[r, row0:row0 + n_rows, cols]

        for r in range(dilation):
            kprev = kprev_ref[...] if dilation == 1 else kprev_ref[r]
            vprev = vprev_ref[...] if dilation == 1 else vprev_ref[r]
            kk = jnp.concatenate([kprev, load(r, 0, BLK, k_cols)], axis=0)
            vv = jnp.concatenate([vprev, load(r, 0, BLK, v_cols)], axis=0)
            store(g, dilation, r, 0, attend(load(r, 0, BLK, q_cols), kk, vv, bias0_ref[g]))
            for b in range(1, n_blocks):
                kk = load(r, (b - 1) * BLK, 2 * BLK, k_cols)
                vv = load(r, (b - 1) * BLK, 2 * BLK, v_cols)
                store(g, dilation, r, b, attend(load(r, b * BLK, BLK, q_cols), kk, vv, bias_ref[g]))

    chunks = sub_rows // BLK

    def merge_body(idx, _):
        c = idx // chunks
        row0 = (idx % chunks) * BLK
        nat_rows = pl.ds(row0 * MERGE_MOD + c, BLK, stride=MERGE_MOD)
        cls_rows = pl.ds(pl.multiple_of(c * sub_rows + row0, BLK), BLK)
        rows_of = lambda g: nat_rows if ATT_GROUPS[g][1] == 1 else cls_rows
        for sl in range(n_slabs):
            ms = [m_scr[g, sl, rows_of(g), :] for g in range(N_GROUPS)]
            mmax = functools.reduce(jnp.maximum, ms)
            ws = [jnp.exp2(m - mmax) for m in ms]
            num = sum(w * o_scr[g, sl, rows_of(g), :] for g, w in enumerate(ws))
            den = sum(w * l_scr[g, sl, rows_of(g), :] for g, w in enumerate(ws))
            nat_scr[sl, nat_rows, :] = num / den
        return 0

    lax.fori_loop(0, MERGE_MOD * chunks, merge_body, 0, unroll=MERGE_UNROLL)
    for sl in range(n_slabs):
        a_ref[:, sl * LANES:(sl + 1) * LANES] = nat_scr[sl].astype(BF16)


def _attention(z_groups, slopes):
    s = z_groups[0].shape[0]
    cur_specs, prev_specs, prev_args = [], [], []
    for z, (_, d) in zip(z_groups, ATT_GROUPS):
        rows = ATT_TILE // d
        prev_row = lambda i, rows=rows: jnp.maximum(i * (rows // BLK) - 1, 0)
        for col in (1, 2):
            if d == 1:
                prev_specs.append(pl.BlockSpec((BLK, ATT_MERGED),
                                               lambda i, c=col, p=prev_row: (p(i), c)))
            else:
                prev_specs.append(pl.BlockSpec((d, BLK, ATT_MERGED),
                                               lambda i, c=col, p=prev_row: (0, p(i), c)))
            prev_args.append(z)
        if d == 1:
            cur_specs.append(pl.BlockSpec((rows, QKV_WIDTH), lambda i: (i, 0)))
        else:
            cur_specs.append(pl.BlockSpec((d, rows, QKV_WIDTH), lambda i: (0, i, 0)))
    scr_shape = (N_GROUPS, ATT_MERGED // LANES, ATT_TILE, LANES)
    bias_shape = (N_GROUPS, HEADS_PER_GROUP * BLK, 2 * BLK)
    return pl.pallas_call(
        functools.partial(_attn_kernel, slopes=slopes),
        out_shape=jax.ShapeDtypeStruct((s, ATT_MERGED), BF16),
        grid=(s // ATT_TILE,),
        in_specs=cur_specs + prev_specs,
        out_specs=pl.BlockSpec((ATT_TILE, ATT_MERGED), lambda i: (i, 0)),
        scratch_shapes=[pltpu.VMEM(bias_shape, F32), pltpu.VMEM(bias_shape, F32),
                        pltpu.VMEM(scr_shape, F32), pltpu.VMEM(scr_shape, F32),
                        pltpu.VMEM(scr_shape, F32), pltpu.VMEM(scr_shape[1:], F32)],
        compiler_params=pltpu.CompilerParams(
            dimension_semantics=("arbitrary",), vmem_limit_bytes=VMEM_LIMIT_BYTES),
        name="attn",
    )(*z_groups, *prev_args)


def _post_kernel(a_ref, pz_ref, pzprev_ref, ga_ref, gp_ref, x_ref, wao_hbm, wc_ref, wout_hbm,
                 g_ref, w1_hbm, w2_hbm, gf_ref, o_ref,
                 wao_ref, wout_ref, w1_ref, w2_ref, stage_ref, sem_ref):
    i = pl.program_id(0)

    def weight_pieces():
        max_rows, width = stage_ref.shape[1:]

        def split(w, w_bf, rows, cols):
            step = min(max_rows, rows.stop - rows.start)
            return [(w.at[r:r + step, cols], w_bf.at[r:r + step, cols], None)
                    for r in range(rows.start, rows.stop, step)]

        assert D_MODEL == width and MLP_CHUNK % width == 0
        full = slice(0, D_MODEL)
        groups = [split(wao_hbm, wao_ref, slice(0, ATT_MERGED), full),
                  split(wout_hbm, wout_ref, full, full)]
        for c in range(D_FF // MLP_CHUNK):
            chunk = [slice(k, k + width) for k in range(c * MLP_CHUNK, (c + 1) * MLP_CHUNK, width)]
            groups.append([p for cols in chunk for p in split(w1_hbm, w1_ref, full, cols)])
            groups.append([p for rows in chunk for p in split(w2_hbm, w2_ref, rows, full)])
        return [p for group in groups for p in group], [len(group) for group in groups]

    def body(stream, uses):
        needed = [0]

        def dot(lhs, w_view):
            if stream is not None:
                needed[0] += uses.pop(0)
                stream.ensure(needed[0])
            return jnp.dot(lhs, w_view[...], preferred_element_type=F32)

        att = dot(a_ref[...], wao_ref)
        t = i * ROW_TILE + lax.broadcasted_iota(jnp.int32, (ROW_TILE, 1), 0)
        pooled = []
        for c in range(POOL_WIDTH // LANES):
            lanes = slice(c * LANES, (c + 1) * LANES)
            prev = jnp.where(i > 0, pzprev_ref[:, lanes].astype(F32), 0.0)
            pooled.append(
                _pooled_minus_token(pz_ref[:, lanes].astype(F32), prev, t, c).astype(BF16))
        pool = jnp.dot(jnp.concatenate(pooled, axis=1), wc_ref[...], preferred_element_type=F32)
        merged = (jax.nn.sigmoid(ga_ref[...].astype(F32)) * att
                  + jax.nn.sigmoid(gp_ref[...].astype(F32)) * pool)
        h = x_ref[...] + dot(merged.astype(BF16), wout_ref)

        m = (h * g_ref[...]).astype(BF16)
        mlp = None
        for c in range(D_FF // MLP_CHUNK):
            cols = slice(c * MLP_CHUNK, (c + 1) * MLP_CHUNK)
            hid = dot(m, w1_ref.at[:, cols])
            hid = jnp.square(jnp.maximum(hid, 0.0)).astype(BF16)
            update = dot(hid, w2_ref.at[cols, :])
            mlp = update if mlp is None else mlp + update
        y = h + jnp.square(_rms_scale(h)) * mlp
        o_ref[...] = y * _rms_scale(y) * gf_ref[...]

    def first_step():
        pieces, uses = weight_pieces()
        body(_WeightStream(pieces, stage_ref, sem_ref), uses)

    pl.when(i == 0)(first_step)
    pl.when(i > 0)(lambda: body(None, None))


def _post(a, zn, zg, x2d, wao, wc, wout, g_mlp, w1, w2, g_final):
    s = x2d.shape[0]
    tm = ROW_TILE
    row = lambda i: (i, 0)
    const = lambda i: (0, 0)
    halo_blocks = tm // POOL_HALO
    assert QKV_WIDTH % POOL_WIDTH == 0
    pool_col = QKV_WIDTH // POOL_WIDTH
    return pl.pallas_call(
        _post_kernel,
        out_shape=jax.ShapeDtypeStruct((s, D_MODEL), F32),
        grid=(s // tm,),
        in_specs=[
            pl.BlockSpec((tm, ATT_MERGED), row),
            pl.BlockSpec((tm, POOL_WIDTH), lambda i: (i, pool_col)),
            pl.BlockSpec((POOL_HALO, POOL_WIDTH),
                         lambda i: (jnp.maximum(i * halo_blocks - 1, 0), pool_col)),
            pl.BlockSpec((tm, D_MODEL), lambda i: (i, 0)),
            pl.BlockSpec((tm, D_MODEL), lambda i: (i, 1)),
            pl.BlockSpec((tm, D_MODEL), row),
            HBM_SPEC,
            pl.BlockSpec((POOL_WIDTH, D_MODEL), const),
            HBM_SPEC,
            pl.BlockSpec((1, D_MODEL), const),
            HBM_SPEC,
            HBM_SPEC,
            pl.BlockSpec((1, D_MODEL), const),
        ],
        out_specs=pl.BlockSpec((tm, D_MODEL), row),
        scratch_shapes=[pltpu.VMEM((ATT_MERGED, D_MODEL), BF16),
                        pltpu.VMEM((D_MODEL, D_MODEL), BF16),
                        pltpu.VMEM((D_MODEL, D_FF), BF16),
                        pltpu.VMEM((D_FF, D_MODEL), BF16),
                        pltpu.VMEM((WEIGHT_STAGE_SLOTS, WEIGHT_STAGE_ROWS, D_MODEL), F32),
                        pltpu.SemaphoreType.DMA((WEIGHT_STAGE_SLOTS,))],
        compiler_params=pltpu.CompilerParams(
            dimension_semantics=("arbitrary",), vmem_limit_bytes=VMEM_LIMIT_BYTES),
        name="post",
    )(a, zn, zn, zg, zg, x2d, wao, wc, wout, g_mlp, w1, w2, g_final)


def _alibi_slopes():
    return tuple(2.0 ** (-ALIBI_MAX_BIAS * (h + 1.0) / N_ATT_HEADS) for h in range(N_ATT_HEADS))


def kernel(x, norm_mix_g, w_in, w_att_out, w_pool_grp, pool_scale, w_pool_out, w_out,
           norm_mlp_g, w_mlp_in, w_mlp_out, norm_final_g):
    batch, seq, d = x.shape
    assert d == D_MODEL and norm_mix_g.shape[0] == 1, "one layer of width D_MODEL"
    assert seq % ATT_TILE == 0 and seq % ROW_TILE == 0 and seq % IN_PROJ_ROWS == 0
    assert ATT_GROUPS[0][1] == 1 and all(d % MERGE_MOD == 0 for _, d in ATT_GROUPS[1:])
    slopes = _alibi_slopes()
    outs = []
    for b in range(batch):
        h = x[b]
        zn, zg, z1, z2, wc = _in_proj(h, norm_mix_g[0][None, :], w_in[0], w_pool_grp[0],
                                      pool_scale[0][None, :], w_pool_out[0])
        a = _attention((zn, z1, z2), slopes)
        outs.append(_post(a, zn, zg, h, w_att_out[0], wc, w_out[0], norm_mlp_g[0][None, :],
                          w_mlp_in[0], w_mlp_out[0], norm_final_g[None, :]))
    return outs[0][None] if batch == 1 else jnp.stack(outs, axis=0)
```

```python
import functools
import math

import jax
import jax.numpy as jnp
from jax import lax
from jax.experimental import pallas as pl
from jax.experimental.pallas import tpu as pltpu

D_MODEL = 1024
HEAD_DIM = 64
ATT_GROUPS = ((128, 1), (512, 4), (2048, 16))
N_GROUPS = len(ATT_GROUPS)
HEADS_PER_GROUP = 4
N_ATT_HEADS = HEADS_PER_GROUP * N_GROUPS
ATT_WIDTH = N_ATT_HEADS * HEAD_DIM
ATT_MERGED = HEADS_PER_GROUP * HEAD_DIM
QKV_WIDTH = 3 * ATT_MERGED
BLK = 128
POOL_WINDOWS = (2, 4, 8, 16)
POOL_GROUP_WIDTH = 3 * D_MODEL // 16
POOL_WIDTH = POOL_GROUP_WIDTH * len(POOL_WINDOWS)
D_FF = 4 * D_MODEL
N_IN = 3 * ATT_WIDTH + POOL_WIDTH + 2 * D_MODEL
NORM_EPS = 1e-6
ALIBI_MAX_BIAS = 8.0
POOL_HALO = max(POOL_WINDOWS)

VMEM_LIMIT_BYTES = 56 * 1024 * 1024
LANES = 128
MASKED_SCORE = -1e30

IN_PROJ_ROWS = 512
ROW_TILE = 512
MLP_CHUNK = 1024
WEIGHT_STAGE_ROWS = 512
WEIGHT_STAGE_SLOTS = 4
MAX_DILATION = max(d for _, d in ATT_GROUPS)
ATT_TILE = BLK * MAX_DILATION
STRIDE_STEP = 4
MERGE_MOD = 4
MERGE_UNROLL = 4
LOG2_E = math.log2(math.e)

ZN_WIDTH = QKV_WIDTH + POOL_WIDTH
ZG_WIDTH = 2 * D_MODEL
NAT_WIDTH = ZN_WIDTH + ZG_WIDTH

BF16 = jnp.bfloat16
F32 = jnp.float32


def _rms_scale(x):
    return lax.rsqrt(jnp.mean(x * x, axis=-1, keepdims=True) + NORM_EPS)


class _WeightStream:
    def __init__(self, pieces, stage_ref, sem_ref):
        self.pieces, self.stage_ref, self.sem_ref = pieces, stage_ref, sem_ref
        self.n_slots = stage_ref.shape[0]
        self.started = self.converted = 0

    def _slot(self, k):
        return self.stage_ref.at[k % self.n_slots, pl.ds(0, self.pieces[k][0].shape[0])]

    def _copy(self, k):
        return pltpu.make_async_copy(self.pieces[k][0], self._slot(k),
                                     self.sem_ref.at[k % self.n_slots])

    def _start_ahead(self):
        while self.started < min(len(self.pieces), self.converted + self.n_slots):
            self._copy(self.started).start()
            self.started += 1

    def ensure(self, count):
        self._start_ahead()
        while self.converted < count:
            k = self.converted
            _, dst, scale = self.pieces[k]
            self._copy(k).wait()
            piece = self._slot(k)[...]
            dst[...] = (piece if scale is None else piece * scale).astype(BF16)
            self.converted += 1
            self._start_ahead()


HBM_SPEC = pl.BlockSpec(memory_space=pl.ANY)


def _w_in_pieces(w_hbm, w_ref):
    pieces = []
    for j in range(N_IN // ATT_MERGED):
        which, g = divmod(j, N_GROUPS)
        if which < 3:
            dst = which * ATT_MERGED + (0 if g == 0 else NAT_WIDTH + (g - 1) * QKV_WIDTH)
        else:
            dst = QKV_WIDTH + (j - 3 * N_GROUPS) * ATT_MERGED
        scale = HEAD_DIM ** -0.5 * LOG2_E if which == 0 else None
        pieces.append((dst, w_hbm.at[:, j * ATT_MERGED:(j + 1) * ATT_MERGED],
                       w_ref.at[:, dst:dst + ATT_MERGED], scale))
    return [piece[1:] for piece in sorted(pieces, key=lambda piece: piece[0])]


def _pooled_minus_token(pf, prev, t, c):
    groups = [g for g in range(len(POOL_WINDOWS))
              if g * POOL_GROUP_WIDTH < (c + 1) * LANES and (g + 1) * POOL_GROUP_WIDTH > c * LANES]
    acc = jnp.concatenate([prev, pf], axis=0)
    w, means = 1, []
    for g in groups:
        while w < POOL_WINDOWS[g]:
            acc = acc + pltpu.roll(acc, w, axis=0)
            w *= 2
        count = jnp.minimum(t + 1, POOL_WINDOWS[g]).astype(F32)
        means.append(acc[POOL_HALO:, :] / count)
    pooled = means[-1]
    for g, mean in zip(groups[-2::-1], means[-2::-1]):
        lane = lax.broadcasted_iota(jnp.int32, mean.shape, 1)
        pooled = jnp.where(lane < (g + 1) * POOL_GROUP_WIDTH - c * LANES, mean, pooled)
    return pooled - pf


def _in_proj_kernel(x_ref, g_ref, w_hbm, zn_ref, zg_ref, *rest):
    dil_refs, (slab_ref, w_ref, stage_ref, sem_ref) = rest[:-4], rest[-4:]

    def body(stream):
        def project(lo, width):
            if stream is not None:
                stream.ensure((lo + width) // ATT_MERGED)
            return jnp.dot(u, w_ref[:, lo:lo + width], preferred_element_type=F32) * scale

        x = x_ref[...]
        scale = _rms_scale(x)
        u = (x * g_ref[...]).astype(BF16)
        lo = 0
        for out_ref, width in ((zn_ref, QKV_WIDTH), (zn_ref, POOL_WIDTH),
                               (zg_ref, D_MODEL), (zg_ref, D_MODEL)):
            off = lo - (0 if out_ref is zn_ref else ZN_WIDTH)
            out_ref[:, off:off + width] = project(lo, width).astype(BF16)
            lo += width
        n_slabs = QKV_WIDTH // LANES
        for gi, z_ref in enumerate(dil_refs):
            dilation = ATT_GROUPS[gi + 1][1]
            acc = project(NAT_WIDTH + gi * QKV_WIDTH, QKV_WIDTH)
            for sl in range(n_slabs):
                slab_ref[0, sl] = acc[:, sl * LANES:(sl + 1) * LANES]
            src, n_rows, stride = 0, IN_PROJ_ROWS, dilation
            while stride > STRIDE_STEP:
                n_rows //= STRIDE_STEP
                for c in range(IN_PROJ_ROWS // n_rows):
                    base = (c // STRIDE_STEP) * n_rows * STRIDE_STEP + c % STRIDE_STEP
                    for sl in range(n_slabs):
                        slab_ref[1 - src, sl, c * n_rows:(c + 1) * n_rows, :] = \
                            slab_ref[src, sl, pl.ds(base, n_rows, stride=STRIDE_STEP), :]
                src, stride = 1 - src, stride // STRIDE_STEP
            outer = dilation // stride
            for r in range(dilation):
                lo_class, hi_class = r % outer, r // outer
                base = lo_class * (IN_PROJ_ROWS // outer) + hi_class
                for sl in range(n_slabs):
                    rows = slab_ref[src, sl, pl.ds(base, IN_PROJ_ROWS // dilation, stride=stride), :]
                    z_ref[r, :, sl * LANES:(sl + 1) * LANES] = rows.astype(BF16)

    first = pl.program_id(0) == 0
    pl.when(first)(lambda: body(_WeightStream(_w_in_pieces(w_hbm, w_ref), stage_ref, sem_ref)))
    pl.when(jnp.logical_not(first))(lambda: body(None))


def _in_proj(x2d, g, w_in):
    s = x2d.shape[0]
    dils = [d for _, d in ATT_GROUPS[1:]]
    out_shape = [jax.ShapeDtypeStruct((s, ZN_WIDTH), BF16), jax.ShapeDtypeStruct((s, ZG_WIDTH), BF16)]
    tm = IN_PROJ_ROWS
    out_specs = [pl.BlockSpec((tm, ZN_WIDTH), lambda i: (i, 0)),
                 pl.BlockSpec((tm, ZG_WIDTH), lambda i: (i, 0))]
    for d in dils:
        out_shape.append(jax.ShapeDtypeStruct((d, s // d, QKV_WIDTH), BF16))
        out_specs.append(pl.BlockSpec((d, tm // d, QKV_WIDTH), lambda i: (0, i, 0)))
    return pl.pallas_call(
        _in_proj_kernel,
        out_shape=out_shape,
        grid=(s // tm,),
        in_specs=[
            pl.BlockSpec((tm, D_MODEL), lambda i: (i, 0)),
            pl.BlockSpec((1, D_MODEL), lambda i: (0, 0)),
            HBM_SPEC,
        ],
        out_specs=out_specs,
        scratch_shapes=[pltpu.VMEM((2, QKV_WIDTH // LANES, tm, LANES), F32),
                        pltpu.VMEM((D_MODEL, N_IN), BF16),
                        pltpu.VMEM((WEIGHT_STAGE_SLOTS, D_MODEL, ATT_MERGED), F32),
                        pltpu.SemaphoreType.DMA((WEIGHT_STAGE_SLOTS,))],
        compiler_params=pltpu.CompilerParams(
            dimension_semantics=("arbitrary",), vmem_limit_bytes=VMEM_LIMIT_BYTES),
        name="in_proj",
    )(x2d, g, w_in)


def _attend_block(qb, kk, vv, bias, head_masks_bf16, low_half):
    nh = HEADS_PER_GROUP
    q_stack = jnp.concatenate([qb * head_masks_bf16[h] for h in range(nh)], axis=0)
    s = lax.dot_general(q_stack, kk, (((1,), (1,)), ((), ())), preferred_element_type=F32)
    s = s + bias
    m = jnp.max(s, axis=-1, keepdims=True)
    p = jnp.exp2(s - m)
    l = jnp.sum(p, axis=-1, keepdims=True)
    pv = jnp.dot(p.astype(BF16), vv, preferred_element_type=F32)
    slabs = []
    for sl in range(ATT_MERGED // LANES):
        lanes = slice(sl * LANES, (sl + 1) * LANES)
        rows_a = slice(2 * sl * BLK, (2 * sl + 1) * BLK)
        rows_b = slice((2 * sl + 1) * BLK, (2 * sl + 2) * BLK)
        slabs.append((jnp.where(low_half, pv[rows_a, lanes], pv[rows_b, lanes]),
                      jnp.where(low_half, m[rows_a], m[rows_b]),
                      jnp.where(low_half, l[rows_a], l[rows_b])))
    return slabs


def _attn_kernel(*refs, slopes):
    z_refs = refs[0:N_GROUPS]
    kprev_refs = refs[N_GROUPS:3 * N_GROUPS:2]
    vprev_refs = refs[N_GROUPS + 1:3 * N_GROUPS:2]
    a_ref, bias_ref, bias0_ref, o_scr, m_scr, l_scr, nat_scr = refs[3 * N_GROUPS:]
    step = pl.program_id(0)
    nh = HEADS_PER_GROUP
    n_slabs = ATT_MERGED // LANES

    @pl.when(step == 0)
    def _():
        qi = lax.broadcasted_iota(jnp.int32, (BLK, 2 * BLK), 0)
        kj = lax.broadcasted_iota(jnp.int32, (BLK, 2 * BLK), 1)
        steps = BLK + qi - kj
        for g, (window, dilation) in enumerate(ATT_GROUPS):
            valid = (steps >= 0) & (steps <= window // dilation)
            dist = (steps * dilation).astype(F32)
            for h in range(nh):
                bias = jnp.where(valid, -(slopes[g * nh + h] * LOG2_E) * dist, MASKED_SCORE)
                bias_ref[g, h * BLK:(h + 1) * BLK, :] = bias
                bias0_ref[g, h * BLK:(h + 1) * BLK, :] = jnp.where(kj < BLK, MASKED_SCORE, bias)

    @pl.when(step == 1)
    def _():
        bias0_ref[...] = bias_ref[...]

    lane_head = lax.broadcasted_iota(jnp.int32, (BLK, ATT_MERGED), 1) // HEAD_DIM
    head_masks_bf16 = [(lane_head == h).astype(F32).astype(BF16) for h in range(nh)]
    low_half = lax.broadcasted_iota(jnp.int32, (BLK, LANES), 1) < HEAD_DIM
    attend = functools.partial(_attend_block, head_masks_bf16=head_masks_bf16, low_half=low_half)
    q_cols, k_cols, v_cols = (slice(c * ATT_MERGED, (c + 1) * ATT_MERGED) for c in range(3))

    sub_rows = ATT_TILE // MERGE_MOD

    def store(g, dilation, r, b, slabs):
        if dilation in (1, MERGE_MOD):
            rows = pl.ds(r * (ATT_TILE // dilation) + b * BLK, BLK)
        else:
            per = dilation // MERGE_MOD
            rows = pl.ds((r % MERGE_MOD) * sub_rows + b * BLK * per + r // MERGE_MOD, BLK,
                         stride=per)
        for sl, (o, m, l) in enumerate(slabs):
            o_scr[g, sl, rows, :] = o
            m_scr[g, sl, rows, :] = m
            l_scr[g, sl, rows, :] = l

    for g, (_, dilation) in enumerate(ATT_GROUPS):
        n_blocks = ATT_TILE // dilation // BLK
        z_ref, kprev_ref, vprev_ref = z_refs[g], kprev_refs[g], vprev_refs[g]

        def load(r, row0, n_rows, cols):
            return z_ref[row0:row0 + n_rows, cols] if dilation == 1 else \
                z_ref[r, row0:row0 + n_rows, cols]

        for r in range(dilation):
            kprev = kprev_ref[...] if dilation == 1 else kprev_ref[r]
            vprev = vprev_ref[...] if dilation == 1 else vprev_ref[r]
            kk = jnp.concatenate([kprev, load(r, 0, BLK, k_cols)], axis=0)
            vv = jnp.concatenate([vprev, load(r, 0, BLK, v_cols)], axis=0)
            store(g, dilation, r, 0, attend(load(r, 0, BLK, q_cols), kk, vv, bias0_ref[g]))
            for b in range(1, n_blocks):
                kk = load(r, (b - 1) * BLK, 2 * BLK, k_cols)
                vv = load(r, (b - 1) * BLK, 2 * BLK, v_cols)
                store(g, dilation, r, b, attend(load(r, b * BLK, BLK, q_cols), kk, vv, bias_ref[g]))

    chunks = sub_rows // BLK

    def merge_body(idx, _):
        c = idx // chunks
        row0 = (idx % chunks) * BLK
        nat_rows = pl.ds(row0 * MERGE_MOD + c, BLK, stride=MERGE_MOD)
        cls_rows = pl.ds(pl.multiple_of(c * sub_rows + row0, BLK), BLK)
        rows_of = lambda g: nat_rows if ATT_GROUPS[g][1] == 1 else cls_rows
        for sl in range(n_slabs):
            ms = [m_scr[g, sl, rows_of(g), :] for g in range(N_GROUPS)]
            mmax = functools.reduce(jnp.maximum, ms)
            ws = [jnp.exp2(m - mmax) for m in ms]
            num = sum(w * o_scr[g, sl, rows_of(g), :] for g, w in enumerate(ws))
            den = sum(w * l_scr[g, sl, rows_of(g), :] for g, w in enumerate(ws))
            nat_scr[sl, nat_rows, :] = num / den
        return 0

    lax.fori_loop(0, MERGE_MOD * chunks, merge_body, 0, unroll=MERGE_UNROLL)
    for sl in range(n_slabs):
        a_ref[:, sl * LANES:(sl + 1) * LANES] = nat_scr[sl].astype(BF16)


def _attention(z_groups, slopes):
    s = z_groups[0].shape[0]
    cur_specs, prev_specs, prev_args = [], [], []
    for z, (_, d) in zip(z_groups, ATT_GROUPS):
        rows = ATT_TILE // d
        prev_row = lambda i, rows=rows: jnp.maximum(i * (rows // BLK) - 1, 0)
        for col in (1, 2):
            if d == 1:
                prev_specs.append(pl.BlockSpec((BLK, ATT_MERGED),
                                               lambda i, c=col, p=prev_row: (p(i), c)))
            else:
                prev_specs.append(pl.BlockSpec((d, BLK, ATT_MERGED),
                                               lambda i, c=col, p=prev_row: (0, p(i), c)))
            prev_args.append(z)
        if d == 1:
            cur_specs.append(pl.BlockSpec((rows, QKV_WIDTH), lambda i: (i, 0)))
        else:
            cur_specs.append(pl.BlockSpec((d, rows, QKV_WIDTH), lambda i: (0, i, 0)))
    scr_shape = (N_GROUPS, ATT_MERGED // LANES, ATT_TILE, LANES)
    bias_shape = (N_GROUPS, HEADS_PER_GROUP * BLK, 2 * BLK)
    return pl.pallas_call(
        functools.partial(_attn_kernel, slopes=slopes),
        out_shape=jax.ShapeDtypeStruct((s, ATT_MERGED), BF16),
        grid=(s // ATT_TILE,),
        in_specs=cur_specs + prev_specs,
        out_specs=pl.BlockSpec((ATT_TILE, ATT_MERGED), lambda i: (i, 0)),
        scratch_shapes=[pltpu.VMEM(bias_shape, F32), pltpu.VMEM(bias_shape, F32),
                        pltpu.VMEM(scr_shape, F32), pltpu.VMEM(scr_shape, F32),
                        pltpu.VMEM(scr_shape, F32), pltpu.VMEM(scr_shape[1:], F32)],
        compiler_params=pltpu.CompilerParams(
            dimension_semantics=("arbitrary",), vmem_limit_bytes=VMEM_LIMIT_BYTES),
        name="attn",
    )(*z_groups, *prev_args)


def _pool_weight_kernel(wgrp_ref, scale_ref, wpo_ref, wc_ref):
    c = POOL_GROUP_WIDTH
    for g in range(len(POOL_WINDOWS)):
        rows = slice(g * c, (g + 1) * c)
        wc = jnp.dot(wgrp_ref[g] * scale_ref[:, rows], wpo_ref[rows, :],
                     precision=lax.Precision.HIGHEST, preferred_element_type=F32)
        wc_ref[rows, :] = wc.astype(BF16)


def _pool_weight(w_grp, scale_col, wpo):
    return pl.pallas_call(
        _pool_weight_kernel,
        out_shape=jax.ShapeDtypeStruct((POOL_WIDTH, D_MODEL), BF16),
        compiler_params=pltpu.CompilerParams(vmem_limit_bytes=VMEM_LIMIT_BYTES),
        name="pool_weight",
    )(w_grp, scale_col, wpo)


def _post_kernel(a_ref, pz_ref, pzprev_ref, ga_ref, gp_ref, x_ref, wao_hbm, wc_ref, wout_hbm,
                 g_ref, w1_hbm, w2_hbm, gf_ref, o_ref,
                 wao_ref, wout_ref, w1_ref, w2_ref, stage_ref, sem_ref):
    i = pl.program_id(0)

    def weight_pieces():
        max_rows, width = stage_ref.shape[1:]

        def split(w, w_bf, rows, cols):
            step = min(max_rows, rows.stop - rows.start)
            return [(w.at[r:r + step, cols], w_bf.at[r:r + step, cols], None)
                    for r in range(rows.start, rows.stop, step)]

        assert D_MODEL == width and MLP_CHUNK % width == 0
        full = slice(0, D_MODEL)
        groups = [split(wao_hbm, wao_ref, slice(0, ATT_MERGED), full),
                  split(wout_hbm, wout_ref, full, full)]
        for c in range(D_FF // MLP_CHUNK):
            chunk = [slice(k, k + width) for k in range(c * MLP_CHUNK, (c + 1) * MLP_CHUNK, width)]
            groups.append([p for cols in chunk for p in split(w1_hbm, w1_ref, full, cols)])
            groups.append([p for rows in chunk for p in split(w2_hbm, w2_ref, rows, full)])
        return [p for group in groups for p in group], [len(group) for group in groups]

    def body(stream, uses):
        needed = [0]

        def dot(lhs, w_view):
            if stream is not None:
                needed[0] += uses.pop(0)
                stream.ensure(needed[0])
            return jnp.dot(lhs, w_view[...], preferred_element_type=F32)

        att = dot(a_ref[...], wao_ref)
        t = i * ROW_TILE + lax.broadcasted_iota(jnp.int32, (ROW_TILE, 1), 0)
        pooled = []
        for c in range(POOL_WIDTH // LANES):
            lanes = slice(c * LANES, (c + 1) * LANES)
            prev = jnp.where(i > 0, pzprev_ref[:, lanes].astype(F32), 0.0)
            pooled.append(
                _pooled_minus_token(pz_ref[:, lanes].astype(F32), prev, t, c).astype(BF16))
        pool = jnp.dot(jnp.concatenate(pooled, axis=1), wc_ref[...], preferred_element_type=F32)
        merged = (jax.nn.sigmoid(ga_ref[...].astype(F32)) * att
                  + jax.nn.sigmoid(gp_ref[...].astype(F32)) * pool)
        h = x_ref[...] + dot(merged.astype(BF16), wout_ref)

        m = (h * g_ref[...]).astype(BF16)
        mlp = None
        for c in range(D_FF // MLP_CHUNK):
            cols = slice(c * MLP_CHUNK, (c + 1) * MLP_CHUNK)
            hid = dot(m, w1_ref.at[:, cols])
            hid = jnp.square(jnp.maximum(hid, 0.0)).astype(BF16)
            update = dot(hid, w2_ref.at[cols, :])
            mlp = update if mlp is None else mlp + update
        y = h + jnp.square(_rms_scale(h)) * mlp
        o_ref[...] = y * _rms_scale(y) * gf_ref[...]

    def first_step():
        pieces, uses = weight_pieces()
        body(_WeightStream(pieces, stage_ref, sem_ref), uses)

    pl.when(i == 0)(first_step)
    pl.when(i > 0)(lambda: body(None, None))


def _post(a, zn, zg, x2d, wao, wc, wout, g_mlp, w1, w2, g_final):
    s = x2d.shape[0]
    tm = ROW_TILE
    row = lambda i: (i, 0)
    const = lambda i: (0, 0)
    halo_blocks = tm // POOL_HALO
    assert QKV_WIDTH % POOL_WIDTH == 0
    pool_col = QKV_WIDTH // POOL_WIDTH
    return pl.pallas_call(
        _post_kernel,
        out_shape=jax.ShapeDtypeStruct((s, D_MODEL), F32),
        grid=(s // tm,),
        in_specs=[
            pl.BlockSpec((tm, ATT_MERGED), row),
            pl.BlockSpec((tm, POOL_WIDTH), lambda i: (i, pool_col)),
            pl.BlockSpec((POOL_HALO, POOL_WIDTH),
                         lambda i: (jnp.maximum(i * halo_blocks - 1, 0), pool_col)),
            pl.BlockSpec((tm, D_MODEL), lambda i: (i, 0)),
            pl.BlockSpec((tm, D_MODEL), lambda i: (i, 1)),
            pl.BlockSpec((tm, D_MODEL), row),
            HBM_SPEC,
            pl.BlockSpec((POOL_WIDTH, D_MODEL), const),
            HBM_SPEC,
            pl.BlockSpec((1, D_MODEL), const),
            HBM_SPEC,
            HBM_SPEC,
            pl.BlockSpec((1, D_MODEL), const),
        ],
        out_specs=pl.BlockSpec((tm, D_MODEL), row),
        scratch_shapes=[pltpu.VMEM((ATT_MERGED, D_MODEL), BF16),
                        pltpu.VMEM((D_MODEL, D_MODEL), BF16),
                        pltpu.VMEM((D_MODEL, D_FF), BF16),
                        pltpu.VMEM((D_FF, D_MODEL), BF16),
                        pltpu.VMEM((WEIGHT_STAGE_SLOTS, WEIGHT_STAGE_ROWS, D_MODEL), F32),
                        pltpu.SemaphoreType.DMA((WEIGHT_STAGE_SLOTS,))],
        compiler_params=pltpu.CompilerParams(
            dimension_semantics=("arbitrary",), vmem_limit_bytes=VMEM_LIMIT_BYTES),
        name="post",
    )(a, zn, zn, zg, zg, x2d, wao, wc, wout, g_mlp, w1, w2, g_final)


def _alibi_slopes():
    return tuple(2.0 ** (-ALIBI_MAX_BIAS * (h + 1.0) / N_ATT_HEADS) for h in range(N_ATT_HEADS))


def kernel(x, norm_mix_g, w_in, w_att_out, w_pool_grp, pool_scale, w_pool_out, w_out,
           norm_mlp_g, w_mlp_in, w_mlp_out, norm_final_g):
    batch, seq, d = x.shape
    assert d == D_MODEL and norm_mix_g.shape[0] == 1, "one layer of width D_MODEL"
    assert seq % ATT_TILE == 0 and seq % ROW_TILE == 0 and seq % IN_PROJ_ROWS == 0
    assert ATT_GROUPS[0][1] == 1 and all(d % MERGE_MOD == 0 for _, d in ATT_GROUPS[1:])
    slopes = _alibi_slopes()
    outs = []
    for b in range(batch):
        h = x[b]
        zn, zg, z1, z2 = _in_proj(h, norm_mix_g[0][None, :], w_in[0])
        a = _attention((zn, z1, z2), slopes)
        wc = _pool_weight(w_pool_grp[0], pool_scale[0][None, :], w_pool_out[0])
        outs.append(_post(a, zn, zg, h, w_att_out[0], wc, w_out[0], norm_mlp_g[0][None, :],
                          w_mlp_in[0], w_mlp_out[0], norm_final_g[None, :]))
    return outs[0][None] if batch == 1 else jnp.stack(outs, axis=0)
```
